```python
import math
import jax, jax.numpy as jnp
from jax import lax
import numpy as np

D_MODEL = 1024
BATCH = 8
SEQ = 2048
DEPTH = 4
DEC_BATCH = 128
DEC_SEQ = 1
PAST_LEN = 16384
PAGE_SIZE = 128

N_M_HEADS = 4
M_HEAD_DIM = D_MODEL // N_M_HEADS
M_WIDTH = N_M_HEADS * M_HEAD_DIM
CHUNK = 64
POOL_WINDOWS = (2, 4, 8, 16)
N_POOL_GROUPS = len(POOL_WINDOWS)
POOL_GROUP_DIM = D_MODEL // N_POOL_GROUPS
POOL_WIDTH = N_POOL_GROUPS * POOL_GROUP_DIM
POOL_BUF = max(POOL_WINDOWS) - 1
N_MEM = 256
N_X_HEADS = 4
X_HEAD_DIM = D_MODEL // N_X_HEADS
X_WIDTH = N_X_HEADS * X_HEAD_DIM
D_FF = ((8 * D_MODEL // 3 + 127) // 128) * 128
CONV_WIDTH = 3
N_BRANCH = 3
ALPHA = (2.0 * DEPTH) ** 0.25
BETA = (8.0 * DEPTH) ** -0.25
LN_EPS = 1e-5
NEG = -1e30
IN_SPLITS = (M_WIDTH, M_WIDTH, M_WIDTH, M_WIDTH, N_M_HEADS, N_M_HEADS, POOL_WIDTH, X_WIDTH, N_BRANCH * D_MODEL)
IN_WIDTH = sum(IN_SPLITS)
IN_OFFSETS = tuple(int(v) for v in np.cumsum(IN_SPLITS)[:-1])
F_GATE_OFF = 4 * M_WIDTH + N_M_HEADS

kernel_name = "hybrid_mlstm_pool_memxattn_step"


def layer_norm(x, g, b):
    xf = x.astype(jnp.float32)
    mu = jnp.mean(xf, axis=-1, keepdims=True)
    var = jnp.mean(jnp.square(xf - mu), axis=-1, keepdims=True)
    return ((xf - mu) * lax.rsqrt(var + LN_EPS) * g.astype(jnp.float32) + b.astype(jnp.float32)).astype(x.dtype)


def mlstm_chunkwise(q, k, v, ig, lf, C0, n0, m0):
    B, L, H, Dk = q.shape
    f32 = jnp.float32
    Lc = min(CHUNK, L)
    nc = -(-L // Lc)
    pad = nc * Lc - L
    q = q.astype(f32)
    k = k.astype(f32) * (Dk ** -0.5)
    v = v.astype(f32)
    ig = ig.astype(f32)
    lf = lf.astype(f32)
    if pad:
        p4 = ((0, 0), (0, pad), (0, 0), (0, 0))
        p3 = ((0, 0), (0, pad), (0, 0))
        q, k, v = jnp.pad(q, p4), jnp.pad(k, p4), jnp.pad(v, p4)
        ig = jnp.pad(ig, p3, constant_values=NEG)
        lf = jnp.pad(lf, p3)

    def to_chunks(t):
        return jnp.moveaxis(t.reshape((B, nc, Lc) + t.shape[2:]), 1, 0)

    causal = jnp.tril(jnp.ones((Lc, Lc), dtype=bool))[None, :, :, None]

    def step(carry, inp):
        C, n, m = carry
        qc, kc, vc, ic, fc = inp
        b = jnp.cumsum(fc, axis=1)
        a = ic - b
        mt = b + jnp.maximum(m[:, None, :], lax.cummax(a, axis=1))
        inter = jnp.exp(m[:, None, :] + b - mt)
        logd = a[:, None, :, :] + b[:, :, None, :] - mt[:, :, None, :]
        dmat = jnp.exp(jnp.where(causal, logd, NEG))
        s = jnp.einsum('bthd,bshd->btsh', qc, kc) * dmat
        num = jnp.einsum('btsh,bshd->bthd', s, vc) + inter[..., None] * jnp.einsum('bthd,bhde->bthe', qc, C)
        den = jnp.sum(s, axis=2) + inter * jnp.einsum('bthd,bhd->bth', qc, n)
        h = num / jnp.maximum(jnp.abs(den), jnp.exp(-mt))[..., None]
        mL = mt[:, -1]
        bL = b[:, -1]
        ws = jnp.exp(a + bL[:, None] - mL[:, None])
        decay = jnp.exp(m + bL - mL)
        kw = kc * ws[..., None]
        C_new = decay[..., None, None] * C + jnp.einsum('bshd,bshe->bhde', kw, vc)
        n_new = decay[..., None] * n + jnp.sum(kw, axis=1)
        return (C_new, n_new, mL), h

    (C, n, m), h = lax.scan(step, (C0.astype(f32), n0.astype(f32), m0.astype(f32)),
                            tuple(to_chunks(t) for t in (q, k, v, ig, lf)))
    h = jnp.moveaxis(h, 0, 1).reshape(B, nc * Lc, H, Dk)[:, :L]
    return h, C, n, m


def pool_mix(u, buf, pos0, w_pool, pool_scale):
    B, L, _ = u.shape
    P = POOL_BUF
    ext = jnp.concatenate([buf.astype(u.dtype), u], axis=1)
    extf = ext.astype(jnp.float32)
    cs = jnp.concatenate([jnp.zeros((B, 1, POOL_WIDTH), jnp.float32), jnp.cumsum(extf, axis=1)], axis=1)
    pos = pos0 + jnp.arange(L, dtype=jnp.int32)
    outs = []
    for g, w in enumerate(POOL_WINDOWS):
        lo, hi = g * POOL_GROUP_DIM, (g + 1) * POOL_GROUP_DIM
        wsum = cs[:, P + 1:P + 1 + L, lo:hi] - cs[:, P + 1 - w:P + 1 - w + L, lo:hi]
        cnt = jnp.minimum(w, pos + 1).astype(jnp.float32)[None, :, None]
        outs.append(wsum / cnt - extf[:, P:, lo:hi])
    d = jnp.stack(outs, axis=2).astype(u.dtype)
    y = jnp.einsum('blgc,gce->blge', d, w_pool).reshape(B, L, POOL_WIDTH) * pool_scale
    return y, ext[:, -P:]


def mem_kv(mem, w_kv):
    B = mem.shape[0]
    kv = (mem @ w_kv).reshape(B, N_MEM, 2, N_X_HEADS, X_HEAD_DIM)
    return kv[:, :, 0], kv[:, :, 1]


def mem_xattn(xq, mk, mv, w_x_out):
    B, L, _ = xq.shape
    q = xq.reshape(B, L, N_X_HEADS, X_HEAD_DIM).astype(jnp.float32)
    s = jnp.einsum('blhd,bmhd->bhlm', q, mk.astype(jnp.float32)) * (X_HEAD_DIM ** -0.5)
    p = jax.nn.softmax(s, axis=-1)
    o = jnp.einsum('bhlm,bmhd->blhd', p, mv.astype(jnp.float32)).astype(xq.dtype)
    return o.reshape(B, L, X_WIDTH) @ w_x_out


def conv_ffn(x, buf, w_up, conv_w, conv_b, w_down):
    L = x.shape[1]
    hup = x @ w_up
    ext = jnp.concatenate([buf.astype(hup.dtype), hup], axis=1)
    c = ext[:, 0:L] * conv_w[0] + ext[:, 1:L + 1] * conv_w[1] + ext[:, 2:L + 2] * conv_w[2] + conv_b
    a, b = jnp.split(c, 2, axis=-1)
    return (jax.nn.gelu(a) * b) @ w_down, ext[:, -(CONV_WIDTH - 1):]


def trunk_layer(x, mk, mv, C0, n0, m0, pbuf, cbuf, pos0,
                w_in, b_in, mh_g, w_m_out, w_pool, pool_scale, w_x_out, w_o,
                ln1_g, ln1_b, w_up, conv_w, conv_b, w_down, ln2_g, ln2_b):
    B, L, _ = x.shape
    proj = x @ w_in + b_in
    q, k, v, o, ig, fg, u, xq, gates = jnp.split(proj, IN_OFFSETS, axis=-1)
    hs = lambda t: t.reshape(B, L, N_M_HEADS, M_HEAD_DIM)
    lf = jax.nn.log_sigmoid(fg.astype(jnp.float32))
    ht, C, n, m = mlstm_chunkwise(hs(q), hs(k), hs(v), ig, lf, C0, n0, m0)
    hc = jax.nn.sigmoid(hs(o).astype(jnp.float32)) * ht
    mu = jnp.mean(hc, axis=-1, keepdims=True)
    var = jnp.mean(jnp.square(hc - mu), axis=-1, keepdims=True)
    hn = (hc - mu) * lax.rsqrt(var + LN_EPS) * mh_g.reshape(N_M_HEADS, M_HEAD_DIM).astype(jnp.float32)
    y_m = hn.reshape(B, L, M_WIDTH).astype(x.dtype) @ w_m_out
    y_p, pbuf_new = pool_mix(u, pbuf, pos0, w_pool, pool_scale)
    y_x = mem_xattn(xq, mk, mv, w_x_out)
    g = jax.nn.sigmoid(gates.reshape(B, L, N_BRANCH, D_MODEL))
    mix = (g[:, :, 0] * y_m + g[:, :, 1] * y_p + g[:, :, 2] * y_x) @ w_o
    x = layer_norm(ALPHA * x + mix, ln1_g, ln1_b)
    f, cbuf_new = conv_ffn(x, cbuf, w_up, conv_w, conv_b, w_down)
    x = layer_norm(ALPHA * x + f, ln2_g, ln2_b)
    return (x, C.astype(C0.dtype), n.astype(n0.dtype), m.astype(m0.dtype),
            pbuf_new.astype(pbuf.dtype), cbuf_new.astype(cbuf.dtype))


def setup_inputs(seed: int = 0) -> dict:
    key = jax.random.key(seed)
    ks = iter(jax.random.split(key, 40))

    def nrm(shape, scale):
        return scale * jax.random.normal(next(ks), shape, jnp.float32)

    d = {}
    d['x_prompt'] = nrm((BATCH, SEQ, D_MODEL), 1.0)
    d['mem_prompt'] = nrm((BATCH, N_MEM, D_MODEL), 1.0)
    d['x_sample'] = nrm((DEC_BATCH, DEC_SEQ, D_MODEL), 1.0)
    d['cache_mem_k'] = nrm((DEPTH, DEC_BATCH, N_MEM, N_X_HEADS, X_HEAD_DIM), 1.0)
    d['cache_mem_v'] = nrm((DEPTH, DEC_BATCH, N_MEM, N_X_HEADS, X_HEAD_DIM), 1.0)
    d['state_C'] = nrm((DEPTH, DEC_BATCH, N_M_HEADS, M_HEAD_DIM, M_HEAD_DIM), 1.0)
    d['state_n'] = nrm((DEPTH, DEC_BATCH, N_M_HEADS, M_HEAD_DIM), 1.0)
    d['state_m'] = nrm((DEPTH, DEC_BATCH, N_M_HEADS), 1.0)
    d['state_pool'] = nrm((DEPTH, DEC_BATCH, POOL_BUF, POOL_WIDTH), 1.0)
    d['state_conv'] = nrm((DEPTH, DEC_BATCH, CONV_WIDTH - 1, 2 * D_FF), 1.0)
    d['ln_in_g'] = 1.0 + nrm((D_MODEL,), 0.02)
    d['ln_in_b'] = nrm((D_MODEL,), 0.02)
    d['w_in'] = nrm((DEPTH, D_MODEL, IN_WIDTH), D_MODEL ** -0.5)
    b_in = nrm((DEPTH, IN_WIDTH), 0.02)
    d['b_in'] = b_in.at[:, F_GATE_OFF:F_GATE_OFF + N_M_HEADS].add(jnp.linspace(3.0, 6.0, N_M_HEADS))
    d['mh_g'] = 1.0 + nrm((DEPTH, M_WIDTH), 0.02)
    d['w_m_out'] = nrm((DEPTH, M_WIDTH, D_MODEL), M_WIDTH ** -0.5)
    d['w_pool'] = nrm((DEPTH, N_POOL_GROUPS, POOL_GROUP_DIM, POOL_GROUP_DIM), POOL_GROUP_DIM ** -0.5)
    d['pool_scale'] = 1.0 + nrm((DEPTH, POOL_WIDTH), 0.02)
    d['w_mem_kv'] = nrm((DEPTH, D_MODEL, 2 * X_WIDTH), D_MODEL ** -0.5)
    d['w_x_out'] = nrm((DEPTH, X_WIDTH, D_MODEL), X_WIDTH ** -0.5)
    d['w_o'] = nrm((DEPTH, D_MODEL, D_MODEL), BETA * D_MODEL ** -0.5)
    d['ln1_g'] = 1.0 + nrm((DEPTH, D_MODEL), 0.02)
    d['ln1_b'] = nrm((DEPTH, D_MODEL), 0.02)
    d['w_up'] = nrm((DEPTH, D_MODEL, 2 * D_FF), D_MODEL ** -0.5)
    d['conv_w'] = nrm((DEPTH, CONV_WIDTH, 2 * D_FF), CONV_WIDTH ** -0.5)
    d['conv_b'] = nrm((DEPTH, 2 * D_FF), 0.02)
    d['w_down'] = nrm((DEPTH, D_FF, D_MODEL), BETA * D_FF ** -0.5)
    d['ln2_g'] = 1.0 + nrm((DEPTH, D_MODEL), 0.02)
    d['ln2_b'] = nrm((DEPTH, D_MODEL), 0.02)
    return d


def reference(x_prompt, mem_prompt, x_sample, cache_mem_k, cache_mem_v, state_C, state_n, state_m,
              state_pool, state_conv, ln_in_g, ln_in_b, w_in, b_in, mh_g, w_m_out, w_pool, pool_scale,
              w_mem_kv, w_x_out, w_o, ln1_g, ln1_b, w_up, conv_w, conv_b, w_down, ln2_g, ln2_b):
    dt = x_prompt.dtype
    xp = layer_norm(x_prompt, ln_in_g, ln_in_b)
    xs = layer_norm(x_sample, ln_in_g, ln_in_b)
    pC, pn, pm, ppool, pconv, pmk, pmv = [], [], [], [], [], [], []
    sC, sn, sm, spool, sconv = [], [], [], [], []
    for l in range(DEPTH):
        w = (w_in[l], b_in[l], mh_g[l], w_m_out[l], w_pool[l], pool_scale[l], w_x_out[l], w_o[l],
             ln1_g[l], ln1_b[l], w_up[l], conv_w[l], conv_b[l], w_down[l], ln2_g[l], ln2_b[l])
        mk, mv = mem_kv(mem_prompt, w_mem_kv[l])
        xp, C1, n1, m1, pb1, cb1 = trunk_layer(
            xp, mk, mv,
            jnp.zeros((BATCH, N_M_HEADS, M_HEAD_DIM, M_HEAD_DIM), dt),
            jnp.zeros((BATCH, N_M_HEADS, M_HEAD_DIM), dt),
            jnp.zeros((BATCH, N_M_HEADS), dt),
            jnp.zeros((BATCH, POOL_BUF, POOL_WIDTH), dt),
            jnp.zeros((BATCH, CONV_WIDTH - 1, 2 * D_FF), dt),
            0, *w)
        pC.append(C1); pn.append(n1); pm.append(m1); ppool.append(pb1); pconv.append(cb1)
        pmk.append(mk); pmv.append(mv)
        xs, C2, n2, m2, pb2, cb2 = trunk_layer(
            xs, cache_mem_k[l], cache_mem_v[l], state_C[l], state_n[l], state_m[l],
            state_pool[l], state_conv[l], PAST_LEN, *w)
        sC.append(C2); sn.append(n2); sm.append(m2); spool.append(pb2); sconv.append(cb2)
    return (xp, xs,
            jnp.stack(pC), jnp.stack(pn), jnp.stack(pm), jnp.stack(ppool), jnp.stack(pconv),
            jnp.stack(pmk), jnp.stack(pmv),
            jnp.stack(sC), jnp.stack(sn), jnp.stack(sm), jnp.stack(spool), jnp.stack(sconv))
```

```python
import functools

import jax
import jax.numpy as jnp
from jax import lax
from jax.experimental import pallas as pl
from jax.experimental.pallas import tpu as pltpu

F32 = jnp.float32
BF16 = jnp.bfloat16

N_HEADS = 4
HEAD_DIM = 256
POOL_WINDOWS = (2, 4, 8, 16)
POOL_GROUP_DIM = 256
POOL_BUF = 15
CONV_WIDTH = 3
DEPTH = 4
PAST_LEN = 16384
ALPHA = (2.0 * DEPTH) ** 0.25
LN_EPS = 1e-5
NEG = -1e30

SUBLANES = 8
LANES = 128
TIME_TILE = 256
FF_CHUNK = 256
SAMPLE_BLOCK = 8
VMEM_LIMIT = 56 * 1024 * 1024


def _params(n_axes):
    return pltpu.CompilerParams(
        dimension_semantics=("arbitrary",) * n_axes, vmem_limit_bytes=VMEM_LIMIT)


def _dot(a, b):
    return jnp.dot(a, b, preferred_element_type=F32)


def _dot_nt(a, b):
    return lax.dot_general(a, b, (((1,), (1,)), ((), ())), preferred_element_type=F32)


def _dot_tn(a, b):
    return lax.dot_general(a, b, (((0,), (0,)), ((), ())), preferred_element_type=F32)


def _layer_norm(x, g, b):
    mu = jnp.mean(x, axis=-1, keepdims=True)
    xc = x - mu
    var = jnp.mean(xc * xc, axis=-1, keepdims=True)
    return xc * lax.rsqrt(var + LN_EPS) * g + b


def _head_norm(x):
    mu = jnp.mean(x, axis=-1, keepdims=True)
    xc = x - mu
    var = jnp.mean(xc * xc, axis=-1, keepdims=True)
    return xc * lax.rsqrt(var + LN_EPS)


def _log_sigmoid(x):
    return jnp.minimum(x, 0.0) - jnp.log1p(jnp.exp(-jnp.abs(x)))


def _scan_rows(x, op):
    n = x.shape[0]
    rows = lax.broadcasted_iota(jnp.int32, x.shape, 0)
    shift = 1
    while shift < n:
        x = jnp.where(rows >= shift, op(x, pltpu.roll(x, shift, 0)), x)
        shift *= 2
    return x


def _hs(h):
    return slice(h * HEAD_DIM, (h + 1) * HEAD_DIM)


def _ln_kernel(x_ref, g_ref, b_ref, o_ref):
    o_ref[...] = _layer_norm(x_ref[...], g_ref[...], b_ref[...])


def _layer_norm_rows(x, g, b, tile):
    rows, d = x.shape
    return pl.pallas_call(
        _ln_kernel,
        grid=(rows // tile,),
        in_specs=[pl.BlockSpec((tile, d), lambda i: (i, 0)),
                  pl.BlockSpec((1, d), lambda i: (0, 0)),
                  pl.BlockSpec((1, d), lambda i: (0, 0))],
        out_specs=pl.BlockSpec((tile, d), lambda i: (i, 0)),
        out_shape=jax.ShapeDtypeStruct((rows, d), F32),
        compiler_params=_params(1),
        name="ln_rows",
    )(x, g, b)


def _mm_kernel(x_ref, w_ref, b_ref, o_ref):
    o_ref[...] = _dot(x_ref[...].astype(BF16), w_ref[...]) + b_ref[...]


def _matmul_bias(x, w_all, b_all, layer, tm, tn, name):
    rows, kdim = x.shape
    n = w_all.shape[-1]
    return pl.pallas_call(
        _mm_kernel,
        grid=(n // tn, rows // tm),
        in_specs=[pl.BlockSpec((tm, kdim), lambda j, i: (i, 0)),
                  pl.BlockSpec((None, kdim, tn), lambda j, i: (layer, 0, j)),
                  pl.BlockSpec((None, 1, tn), lambda j, i: (layer, 0, j))],
        out_specs=pl.BlockSpec((tm, tn), lambda j, i: (i, j)),
        out_shape=jax.ShapeDtypeStruct((rows, n), F32),
        compiler_params=_params(2),
        name=name,
    )(x, w_all, b_all)


def _mlstm_prompt_kernel(x_ref, wqkvo_ref, bqkvo_ref, wif_ref, bif_ref, wg0_ref, bg0_ref,
                         mhg_ref, wmo_ref, y_ref, c_ref, n_ref, m_ref):
    t = pl.program_id(1)
    d = N_HEADS * HEAD_DIM

    @pl.when(t == 0)
    def _():
        c_ref[...] = jnp.zeros(c_ref.shape, F32)
        n_ref[...] = jnp.zeros(n_ref.shape, F32)
        m_ref[...] = jnp.zeros(m_ref.shape, F32)

    xb = x_ref[...].astype(BF16)
    lc = xb.shape[0]
    gi = _dot(xb, wif_ref[:, 0:LANES]) + bif_ref[:, 0:LANES]
    gf = _dot(xb, wif_ref[:, LANES:2 * LANES]) + bif_ref[:, LANES:2 * LANES]
    lf = _log_sigmoid(gf)
    bcum = _scan_rows(lf, jnp.add)
    a = gi - bcum
    m0 = m_ref[0]
    mt = bcum + jnp.maximum(m0, _scan_rows(a, jnp.maximum))
    inter = jnp.exp(m0 + bcum - mt)
    bm = bcum - mt
    emt = jnp.exp(-mt)
    b_last = bcum[lc - 1:lc, :]
    m_last = mt[lc - 1:lc, :]
    ws = jnp.exp(a + b_last - m_last)
    decay = jnp.exp(m0 + b_last - m_last)
    a_t = a.T
    causal = (lax.broadcasted_iota(jnp.int32, (lc, lc), 1)
              <= lax.broadcasted_iota(jnp.int32, (lc, lc), 0))

    hn_heads = []
    for h in range(N_HEADS):
        def proj(part):
            cols = slice(part * d + h * HEAD_DIM, part * d + (h + 1) * HEAD_DIM)
            return _dot(xb, wqkvo_ref[:, cols]) + bqkvo_ref[:, cols]
        q = proj(0)
        k = proj(1) * (HEAD_DIM ** -0.5)
        v = proj(2)
        o = proj(3)
        qb = q.astype(BF16)
        vb = v.astype(BF16)
        col = slice(h, h + 1)
        logd = a_t[h:h + 1, :] + bm[:, col]
        dmat = jnp.exp(jnp.where(causal, logd, NEG))
        s = _dot_nt(qb, k.astype(BF16)) * dmat
        c_old = c_ref[0, h]
        n_old = n_ref[0, h:h + 1, :]
        inter_h = inter[:, col]
        num = _dot(s.astype(BF16), vb) + inter_h * _dot(qb, c_old.astype(BF16))
        den = (jnp.sum(s, axis=1, keepdims=True)
               + inter_h * jnp.sum(q * n_old, axis=1, keepdims=True))
        ht = num / jnp.maximum(jnp.abs(den), emt[:, col])
        kw = k * ws[:, col]
        c_ref[0, h] = decay[:, col] * c_old + _dot_tn(kw.astype(BF16), vb)
        n_ref[0, h:h + 1, :] = decay[:, col] * n_old + jnp.sum(kw, axis=0, keepdims=True)
        hn = _head_norm(jax.nn.sigmoid(o) * ht) * mhg_ref[:, _hs(h)]
        hn_heads.append(hn.astype(BF16))
    m_ref[0] = m_last

    ym = _dot(jnp.concatenate(hn_heads, axis=1), wmo_ref[...])
    g0 = jax.nn.sigmoid(_dot(xb, wg0_ref[...]) + bg0_ref[...])
    y_ref[...] = g0 * ym


def _mlstm_prompt(x2d, batch, w_main, b_main, w_if, b_if, mh_g, w_m_out, layer):
    rows, d = x2d.shape
    nt = rows // batch // TIME_TILE
    wide = N_HEADS * d

    def lw(shape, col):
        return pl.BlockSpec((None,) + shape, lambda b, t: (layer, 0, col))

    return pl.pallas_call(
        _mlstm_prompt_kernel,
        grid=(batch, nt),
        in_specs=[pl.BlockSpec((TIME_TILE, d), lambda b, t: (b * nt + t, 0)),
                  lw((d, wide), 0), lw((1, wide), 0),
                  lw((d, 2 * LANES), 0), lw((1, 2 * LANES), 0),
                  lw((d, d), 6), lw((1, d), 6),
                  lw((1, d), 0), lw((d, d), 0)],
        out_specs=[pl.BlockSpec((TIME_TILE, d), lambda b, t: (b * nt + t, 0)),
                   pl.BlockSpec((1, N_HEADS, HEAD_DIM, HEAD_DIM), lambda b, t: (b, 0, 0, 0)),
                   pl.BlockSpec((1, N_HEADS, HEAD_DIM), lambda b, t: (b, 0, 0)),
                   pl.BlockSpec((1, 1, LANES), lambda b, t: (b, 0, 0))],
        out_shape=[jax.ShapeDtypeStruct((rows, d), F32),
                   jax.ShapeDtypeStruct((batch, N_HEADS, HEAD_DIM, HEAD_DIM), F32),
                   jax.ShapeDtypeStruct((batch, N_HEADS, HEAD_DIM), F32),
                   jax.ShapeDtypeStruct((batch, 1, LANES), F32)],
        compiler_params=_params(2),
        name="mlstm_prompt",
    )(x2d, w_main, b_main, w_if, b_if, w_main, b_main, mh_g, w_m_out)


def _pool_window_sums(ext, window):
    s, span = ext, 1
    while span < window:
        s = s + pltpu.roll(s, span, 0)
        span *= 2
    return s


def _xattn_head(qb, kb, vb):
    s = _dot_nt(qb, kb) * (HEAD_DIM ** -0.5)
    s = s - jnp.max(s, axis=-1, keepdims=True)
    p = jnp.exp(s)
    p = p / jnp.sum(p, axis=-1, keepdims=True)
    return _dot(p.astype(BF16), vb)


def _mix_prompt_kernel(x_ref, ymg_ref, mk_ref, mv_ref, wu_ref, bu_ref, wxq_ref, bxq_ref,
                       wg1_ref, bg1_ref, wg2_ref, bg2_ref, wpool_ref, pscale_ref, wxo_ref,
                       wo_ref, lng_ref, lnb_ref, x1_ref, pool_ref, ext_scr, mkb_scr, mvb_scr):
    t = pl.program_id(1)
    tt = x_ref.shape[0]
    carry = 2 * SUBLANES

    @pl.when(t == 0)
    def _():
        ext_scr[0:carry, :] = jnp.zeros((carry, ext_scr.shape[1]), F32)
        mkb_scr[...] = mk_ref[...].astype(BF16)
        mvb_scr[...] = mv_ref[...].astype(BF16)

    x = x_ref[...]
    xb = x.astype(BF16)

    u = _dot(xb, wu_ref[...]) + bu_ref[...]
    ext_scr[carry:carry + tt, :] = u
    pos = t * tt + lax.broadcasted_iota(jnp.int32, (tt, 1), 0)
    yp_groups = []
    for g, window in enumerate(POOL_WINDOWS):
        cols = slice(g * POOL_GROUP_DIM, (g + 1) * POOL_GROUP_DIM)
        wsum = _pool_window_sums(ext_scr[:, cols], window)[carry:, :]
        cnt = jnp.minimum(window, pos + 1).astype(F32)
        dlt = wsum / cnt - u[:, cols]
        yp_groups.append(_dot(dlt.astype(BF16), wpool_ref[g]))
    yp = jnp.concatenate(yp_groups, axis=1) * pscale_ref[...]
    ext_scr[0:carry, :] = u[tt - carry:tt, :]
    pool_ref[0] = u[tt - carry:tt, :]

    xq = _dot(xb, wxq_ref[...]) + bxq_ref[...]
    heads = [_xattn_head(xq[:, _hs(h)].astype(BF16), mkb_scr[:, _hs(h)], mvb_scr[:, _hs(h)])
             for h in range(N_HEADS)]
    yx = _dot(jnp.concatenate(heads, axis=1).astype(BF16), wxo_ref[...])

    g1 = jax.nn.sigmoid(_dot(xb, wg1_ref[...]) + bg1_ref[...])
    g2 = jax.nn.sigmoid(_dot(xb, wg2_ref[...]) + bg2_ref[...])
    mix = ymg_ref[...] + g1 * yp + g2 * yx
    y = ALPHA * x + _dot(mix.astype(BF16), wo_ref[...])
    x1_ref[...] = _layer_norm(y, lng_ref[...], lnb_ref[...])


def _mix_prompt(x2d, ymg, mk, mv, batch, w_main, b_main, w_pool, pool_scale, w_x_out, w_o,
                ln_g, ln_b, layer):
    rows, d = x2d.shape
    nt = rows // batch // TIME_TILE
    n_mem = mk.shape[0] // batch
    carry = 2 * SUBLANES

    def lw(shape, col):
        return pl.BlockSpec((None,) + shape, lambda b, t: (layer, 0, col))

    row_spec = pl.BlockSpec((TIME_TILE, d), lambda b, t: (b * nt + t, 0))
    mem_spec = pl.BlockSpec((n_mem, d), lambda b, t: (b, 0))
    return pl.pallas_call(
        _mix_prompt_kernel,
        grid=(batch, nt),
        in_specs=[row_spec, row_spec, mem_spec, mem_spec,
                  lw((d, d), 4), lw((1, d), 4),
                  lw((d, d), 5), lw((1, d), 5),
                  lw((d, d), 7), lw((1, d), 7),
                  lw((d, d), 8), lw((1, d), 8),
                  pl.BlockSpec((None, len(POOL_WINDOWS), POOL_GROUP_DIM, POOL_GROUP_DIM),
                               lambda b, t: (layer, 0, 0, 0)),
                  lw((1, d), 0), lw((d, d), 0), lw((d, d), 0), lw((1, d), 0), lw((1, d), 0)],
        out_specs=[row_spec, pl.BlockSpec((1, carry, d), lambda b, t: (b, 0, 0))],
        out_shape=[jax.ShapeDtypeStruct((rows, d), F32),
                   jax.ShapeDtypeStruct((batch, carry, d), F32)],
        scratch_shapes=[pltpu.VMEM((carry + TIME_TILE, d), F32),
                        pltpu.VMEM((n_mem, d), BF16),
                        pltpu.VMEM((n_mem, d), BF16)],
        compiler_params=_params(2),
        name="mix_prompt",
    )(x2d, ymg, mk, mv, w_main, b_main, w_main, b_main, w_main, b_main, w_main, b_main,
      w_pool, pool_scale, w_x_out, w_o, ln_g, ln_b)


def _ffn_chunk(xb, j, d_ff, wup_ref, cw_ref, cb_ref, wdn_ref, conv_taps):
    halves = []
    conv = []
    for base in (0, d_ff):
        cols = slice(base + j * FF_CHUNK, base + (j + 1) * FF_CHUNK)
        hup = _dot(xb, wup_ref[:, cols])
        prev2, prev1 = conv_taps(hup, cols)
        conv.append(prev2 * cw_ref[0:1, cols] + prev1 * cw_ref[1:2, cols]
                    + hup * cw_ref[2:3, cols] + cb_ref[:, cols])
        halves.append(hup)
    act = jax.nn.gelu(conv[0]) * conv[1]
    part = _dot(act.astype(BF16), wdn_ref[j * FF_CHUNK:(j + 1) * FF_CHUNK, :])
    return part, halves


def _ffn_prompt_kernel(x_ref, wup_ref, cw_ref, cb_ref, wdn_ref, lng_ref, lnb_ref,
                       o_ref, conv_ref):
    t = pl.program_id(1)
    tt = x_ref.shape[0]
    d_ff = wdn_ref.shape[0]

    @pl.when(t == 0)
    def _():
        conv_ref[...] = jnp.zeros(conv_ref.shape, F32)

    x = x_ref[...]
    xb = x.astype(BF16)

    def conv_taps(hup, cols):
        ext = jnp.concatenate([conv_ref[0, :, cols], hup], axis=0)
        return (pltpu.roll(ext, 2, 0)[SUBLANES:, :], pltpu.roll(ext, 1, 0)[SUBLANES:, :])

    acc = None
    for j in range(d_ff // FF_CHUNK):
        part, halves = _ffn_chunk(xb, j, d_ff, wup_ref, cw_ref, cb_ref, wdn_ref, conv_taps)
        for base, hup in zip((0, d_ff), halves):
            conv_ref[0, :, base + j * FF_CHUNK:base + (j + 1) * FF_CHUNK] = hup[tt - SUBLANES:, :]
        acc = part if acc is None else acc + part
    o_ref[...] = _layer_norm(ALPHA * x + acc, lng_ref[...], lnb_ref[...])


def _ffn_prompt(x2d, batch, w_up, conv_w, conv_b, w_down, ln_g, ln_b, layer):
    rows, d = x2d.shape
    nt = rows // batch // TIME_TILE
    d_ff = w_down.shape[1]

    def lw(shape):
        return pl.BlockSpec((None,) + shape, lambda b, t: (layer, 0, 0))

    row_spec = pl.BlockSpec((TIME_TILE, d), lambda b, t: (b * nt + t, 0))
    return pl.pallas_call(
        _ffn_prompt_kernel,
        grid=(batch, nt),
        in_specs=[row_spec, lw((d, 2 * d_ff)), lw((CONV_WIDTH, 2 * d_ff)), lw((1, 2 * d_ff)),
                  lw((d_ff, d)), lw((1, d)), lw((1, d))],
        out_specs=[row_spec, pl.BlockSpec((1, SUBLANES, 2 * d_ff), lambda b, t: (b, 0, 0))],
        out_shape=[jax.ShapeDtypeStruct((rows, d), F32),
                   jax.ShapeDtypeStruct((batch, SUBLANES, 2 * d_ff), F32)],
        compiler_params=_params(2),
        name="ffn_prompt",
    )(x2d, w_up, conv_w, conv_b, w_down, ln_g, ln_b)


def _ffn_sample_kernel(x_ref, st_ref, wup_ref, cw_ref, cb_ref, wdn_ref, lng_ref, lnb_ref,
                       o_ref, stn_ref):
    d_ff = wdn_ref.shape[0]
    x = x_ref[...]
    xb = x.astype(BF16)

    def conv_taps(hup, cols):
        prev1 = slice(2 * d_ff + cols.start, 2 * d_ff + cols.stop)
        return st_ref[:, cols], st_ref[:, prev1]

    acc = None
    for j in range(d_ff // FF_CHUNK):
        part, halves = _ffn_chunk(xb, j, d_ff, wup_ref, cw_ref, cb_ref, wdn_ref, conv_taps)
        for base, hup in zip((0, d_ff), halves):
            cols = slice(base + j * FF_CHUNK, base + (j + 1) * FF_CHUNK)
            prev1 = slice(2 * d_ff + cols.start, 2 * d_ff + cols.stop)
            stn_ref[:, cols] = st_ref[:, prev1]
            stn_ref[:, prev1] = hup
        acc = part if acc is None else acc + part
    o_ref[...] = _layer_norm(ALPHA * x + acc, lng_ref[...], lnb_ref[...])


def _ffn_sample(x, conv_state, w_up, conv_w, conv_b, w_down, ln_g, ln_b, layer, tile):
    rows, d = x.shape
    d_ff = w_down.shape[1]
    width = conv_state.shape[2]

    def lw(shape):
        return pl.BlockSpec((None,) + shape, lambda i: (layer, 0, 0))

    return pl.pallas_call(
        _ffn_sample_kernel,
        grid=(rows // tile,),
        in_specs=[pl.BlockSpec((tile, d), lambda i: (i, 0)),
                  pl.BlockSpec((None, tile, width), lambda i: (layer, i, 0)),
                  lw((d, 2 * d_ff)), lw((CONV_WIDTH, 2 * d_ff)), lw((1, 2 * d_ff)),
                  lw((d_ff, d)), lw((1, d)), lw((1, d))],
        out_specs=[pl.BlockSpec((tile, d), lambda i: (i, 0)),
                   pl.BlockSpec((tile, width), lambda i: (i, 0))],
        out_shape=[jax.ShapeDtypeStruct((rows, d), F32),
                   jax.ShapeDtypeStruct((rows, width), F32)],
        compiler_params=_params(1),
        name="ffn_sample",
    )(x, conv_state, w_up, conv_w, conv_b, w_down, ln_g, ln_b)


def _mlstm_step_kernel(q_ref, k_ref, v_ref, o_ref, g_ref, c_ref, n_ref, m_ref, mhg_ref,
                       hn_ref, cn_ref, nn_ref, mn_ref, kw_scr, dec_scr, r_scr):
    bb = q_ref.shape[0]
    mxu_rows = 2 * SUBLANES
    gi = g_ref[:, 0:N_HEADS]
    lf = _log_sigmoid(g_ref[:, LANES:LANES + N_HEADS])
    m0 = m_ref[...]
    mt = jnp.maximum(m0 + lf, gi)
    inter_all = jnp.exp(m0 + lf - mt)
    dm_all = jnp.exp(gi - mt)
    emt_all = jnp.exp(-mt)
    mn_ref[...] = mt

    stats = []
    for h in range(N_HEADS):
        col = slice(h, h + 1)
        q = q_ref[:, _hs(h)]
        k = k_ref[:, _hs(h)] * (HEAD_DIM ** -0.5)
        n_old = n_ref[:, _hs(h)]
        inter, dm = inter_all[:, col], dm_all[:, col]
        s = jnp.sum(q * k, axis=1, keepdims=True) * dm
        den = s + inter * jnp.sum(q * n_old, axis=1, keepdims=True)
        kw = k * dm
        kw_scr[:, _hs(h)] = kw
        dec_scr[:, _hs(h)] = jnp.broadcast_to(inter, kw.shape)
        nn_ref[:, _hs(h)] = inter * n_old + kw
        stats.append((s, inter, den, emt_all[:, col]))

    row0 = lax.broadcasted_iota(jnp.int32, (mxu_rows, HEAD_DIM), 0) == 0

    def per_sequence(b, carry):
        row = pl.ds(b, 1)
        for h in range(N_HEADS):
            qrow = jnp.broadcast_to(q_ref[row, _hs(h)], (mxu_rows, HEAD_DIM)).astype(BF16)
            c_old = c_ref[b, h]
            r_scr[row, _hs(h)] = _dot(qrow, c_old.astype(BF16))[0:1, :]
            kw = jnp.where(row0, jnp.broadcast_to(kw_scr[row, _hs(h)], (mxu_rows, HEAD_DIM)), 0.0)
            vrow = jnp.broadcast_to(v_ref[row, _hs(h)], (mxu_rows, HEAD_DIM))
            cn_ref[b, h] = (dec_scr[row, _hs(h)] * c_old
                            + _dot_tn(kw.astype(BF16), vrow.astype(BF16)))
        return carry

    lax.fori_loop(0, bb, per_sequence, 0)

    for h in range(N_HEADS):
        s, inter, den, emt = stats[h]
        ht = (s * v_ref[:, _hs(h)] + inter * r_scr[:, _hs(h)]) / jnp.maximum(jnp.abs(den), emt)
        hc = jax.nn.sigmoid(o_ref[:, _hs(h)]) * ht
        hn_ref[:, _hs(h)] = _head_norm(hc) * mhg_ref[:, _hs(h)]


def _mlstm_step(proj, gates, state_c, state_n, state_m, mh_g, layer):
    nseq = proj.shape[0]
    d = N_HEADS * HEAD_DIM
    bb = SAMPLE_BLOCK

    def pcol(c):
        return pl.BlockSpec((bb, d), lambda i: (i, c))

    cspec = pl.BlockSpec((None, bb, N_HEADS, HEAD_DIM, HEAD_DIM), lambda i: (layer, i, 0, 0, 0))
    return pl.pallas_call(
        _mlstm_step_kernel,
        grid=(nseq // bb,),
        in_specs=[pcol(0), pcol(1), pcol(2), pcol(3),
                  pl.BlockSpec((bb, 2 * LANES), lambda i: (i, 0)),
                  cspec,
                  pl.BlockSpec((None, bb, d), lambda i: (layer, i, 0)),
                  pl.BlockSpec((None, bb, N_HEADS), lambda i: (layer, i, 0)),
                  pl.BlockSpec((None, 1, d), lambda i: (layer, 0, 0))],
        out_specs=[pl.BlockSpec((bb, d), lambda i: (i, 0)),
                   pl.BlockSpec((bb, N_HEADS, HEAD_DIM, HEAD_DIM), lambda i: (i, 0, 0, 0)),
                   pl.BlockSpec((bb, d), lambda i: (i, 0)),
                   pl.BlockSpec((bb, N_HEADS), lambda i: (i, 0))],
        out_shape=[jax.ShapeDtypeStruct((nseq, d), F32),
                   jax.ShapeDtypeStruct((nseq, N_HEADS, HEAD_DIM, HEAD_DIM), F32),
                   jax.ShapeDtypeStruct((nseq, d), F32),
                   jax.ShapeDtypeStruct((nseq, N_HEADS), F32)],
        scratch_shapes=[pltpu.VMEM((bb, d), F32)] * 3,
        compiler_params=_params(1),
        name="mlstm_step",
    )(proj, proj, proj, proj, gates, state_c, state_n, state_m, mh_g)


def _xattn_step_kernel(xq_ref, k_ref, v_ref, o_ref):
    bb = xq_ref.shape[0]
    mxu_rows = 2 * SUBLANES

    def per_sequence(b, carry):
        row = pl.ds(b, 1)
        for h in range(N_HEADS):
            qrow = jnp.broadcast_to(xq_ref[row, _hs(h)], (mxu_rows, HEAD_DIM)).astype(BF16)
            out = _xattn_head(qrow, k_ref[b, :, _hs(h)].astype(BF16),
                              v_ref[b, :, _hs(h)].astype(BF16))
            o_ref[row, _hs(h)] = out[0:1, :]
        return carry

    lax.fori_loop(0, bb, per_sequence, 0)


def _xattn_step(proj, mem_k, mem_v, layer):
    nseq = proj.shape[0]
    d = N_HEADS * HEAD_DIM
    n_mem = mem_k.shape[2]
    bb = SAMPLE_BLOCK
    kv_spec = pl.BlockSpec((None, bb, n_mem, d), lambda i: (layer, i, 0, 0))
    return pl.pallas_call(
        _xattn_step_kernel,
        grid=(nseq // bb,),
        in_specs=[pl.BlockSpec((bb, d), lambda i: (i, 5)), kv_spec, kv_spec],
        out_specs=pl.BlockSpec((bb, d), lambda i: (i, 0)),
        out_shape=jax.ShapeDtypeStruct((nseq, d), F32),
        compiler_params=_params(1),
        name="xattn_step",
    )(proj, mem_k, mem_v)


def _merge_step_kernel(x_ref, hn_ref, ox_ref, u_ref, g0_ref, g1_ref, g2_ref, pool_ref,
                       wmo_ref, wpool_ref, pscale_ref, wxo_ref, wo_ref, lng_ref, lnb_ref,
                       x1_ref, pooln_ref):
    d = x_ref.shape[1]
    x = x_ref[...]
    u = u_ref[...]
    pooln_ref[:, 0:(POOL_BUF - 1) * d] = pool_ref[:, d:POOL_BUF * d]
    pooln_ref[:, (POOL_BUF - 1) * d:POOL_BUF * d] = u
    yp_groups = []
    for g, window in enumerate(POOL_WINDOWS):
        cols = slice(g * POOL_GROUP_DIM, (g + 1) * POOL_GROUP_DIM)
        wsum = u[:, cols]
        for back in range(1, window):
            r = POOL_BUF - back
            wsum = wsum + pool_ref[:, r * d + cols.start:r * d + cols.stop]
        cnt = float(min(window, PAST_LEN + 1))
        dlt = wsum / cnt - u[:, cols]
        yp_groups.append(_dot(dlt.astype(BF16), wpool_ref[g]))
    yp = jnp.concatenate(yp_groups, axis=1) * pscale_ref[...]
    ym = _dot(hn_ref[...].astype(BF16), wmo_ref[...])
    yx = _dot(ox_ref[...].astype(BF16), wxo_ref[...])
    mix = (jax.nn.sigmoid(g0_ref[...]) * ym + jax.nn.sigmoid(g1_ref[...]) * yp
           + jax.nn.sigmoid(g2_ref[...]) * yx)
    y = ALPHA * x + _dot(mix.astype(BF16), wo_ref[...])
    x1_ref[...] = _layer_norm(y, lng_ref[...], lnb_ref[...])


def _merge_step(x, hn, ox, proj, pool_state, w_m_out, w_pool, pool_scale, w_x_out, w_o,
                ln_g, ln_b, layer, tile):
    rows, d = x.shape
    width = pool_state.shape[2]

    def lw(shape):
        return pl.BlockSpec((None,) + shape, lambda i: (layer,) + (0,) * len(shape))

    def rows_at(col):
        return pl.BlockSpec((tile, d), lambda i: (i, col))

    wide = pl.BlockSpec((tile, width), lambda i: (i, 0))
    return pl.pallas_call(
        _merge_step_kernel,
        grid=(rows // tile,),
        in_specs=[rows_at(0), rows_at(0), rows_at(0), rows_at(4), rows_at(6), rows_at(7),
                  rows_at(8), pl.BlockSpec((None, tile, width), lambda i: (layer, i, 0)),
                  lw((d, d)), lw((len(POOL_WINDOWS), POOL_GROUP_DIM, POOL_GROUP_DIM)),
                  lw((1, d)), lw((d, d)), lw((d, d)), lw((1, d)), lw((1, d))],
        out_specs=[rows_at(0), wide],
        out_shape=[jax.ShapeDtypeStruct((rows, d), F32),
                   jax.ShapeDtypeStruct((rows, width), F32)],
        compiler_params=_params(1),
        name="merge_step",
    )(x, hn, ox, proj, proj, proj, proj, pool_state, w_m_out, w_pool, pool_scale, w_x_out,
      w_o, ln_g, ln_b)


def kernel(x_prompt, mem_prompt, x_sample, cache_mem_k, cache_mem_v, state_C, state_n, state_m,
           state_pool, state_conv, ln_in_g, ln_in_b, w_in, b_in, mh_g, w_m_out, w_pool,
           pool_scale, w_mem_kv, w_x_out, w_o, ln1_g, ln1_b, w_up, conv_w, conv_b, w_down,
           ln2_g, ln2_b):
    batch, seq, d = x_prompt.shape
    nseq = x_sample.shape[0]
    n_mem = mem_prompt.shape[1]
    depth = w_in.shape[0]
    d_ff = w_down.shape[1]
    wide = N_HEADS * d
    gate_off = wide
    rest_off = wide + 2 * N_HEADS
    assert d == N_HEADS * HEAD_DIM and seq % TIME_TILE == 0 and d_ff % FF_CHUNK == 0
    assert depth == DEPTH and nseq % SAMPLE_BLOCK == 0 and x_sample.shape[1] == 1

    def row(v):
        return v.reshape(depth, 1, v.shape[-1])

    w_main = jnp.concatenate([w_in[:, :, :wide], w_in[:, :, rest_off:]], axis=2).astype(BF16)
    b_main = row(jnp.concatenate([b_in[:, :wide], b_in[:, rest_off:]], axis=1))
    lane_pad = ((0, 0), (0, 0), (0, LANES - N_HEADS))
    w_if = jnp.concatenate(
        [jnp.pad(w_in[:, :, gate_off:gate_off + N_HEADS], lane_pad),
         jnp.pad(w_in[:, :, gate_off + N_HEADS:rest_off], lane_pad)], axis=2).astype(BF16)
    b_if = row(jnp.concatenate(
        [jnp.pad(b_in[:, gate_off:gate_off + N_HEADS], lane_pad[1:]),
         jnp.pad(b_in[:, gate_off + N_HEADS:rest_off], lane_pad[1:])], axis=1))
    w_m_out_b, w_pool_b, w_x_out_b, w_o_b = (w.astype(BF16) for w in (w_m_out, w_pool, w_x_out, w_o))
    w_up_b, w_down_b = w_up.astype(BF16), w_down.astype(BF16)
    w_kv_b = w_mem_kv.astype(BF16)
    zero_bias = jnp.zeros((depth, 1, w_mem_kv.shape[2]), F32)
    mh_g_r, pool_scale_r, conv_b_r = row(mh_g), row(pool_scale), row(conv_b)
    ln1_g_r, ln1_b_r, ln2_g_r, ln2_b_r = row(ln1_g), row(ln1_b), row(ln2_g), row(ln2_b)

    xp = x_prompt.reshape(batch * seq, d)
    xs = x_sample.reshape(nseq, d)
    mem2d = mem_prompt.reshape(batch * n_mem, d)
    mem_k = cache_mem_k.reshape(depth, nseq, n_mem, d)
    mem_v = cache_mem_v.reshape(depth, nseq, n_mem, d)
    st_n = state_n.reshape(depth, nseq, d)
    st_pool = state_pool.reshape(depth, nseq, POOL_BUF * d)
    st_conv = state_conv.reshape(depth, nseq, (CONV_WIDTH - 1) * 2 * d_ff)

    xp = _layer_norm_rows(xp, ln_in_g.reshape(1, d), ln_in_b.reshape(1, d), 2 * TIME_TILE)
    xs = _layer_norm_rows(xs, ln_in_g.reshape(1, d), ln_in_b.reshape(1, d), nseq)

    outs = {name: [] for name in ("pC", "pn", "pm", "ppool", "pconv", "pmk", "pmv",
                                  "sC", "sn", "sm", "spool", "sconv")}
    for l in range(depth):
        kv = _matmul_bias(mem2d, w_kv_b, zero_bias, l, 2 * TIME_TILE, d, "mem_kv")
        mk, mv = kv[:, :d], kv[:, d:]
        ymg, p_c, p_n, p_m = _mlstm_prompt(xp, batch, w_main, b_main, w_if, b_if, mh_g_r,
                                           w_m_out_b, l)
        x1, p_pool = _mix_prompt(xp, ymg, mk, mv, batch, w_main, b_main, w_pool_b, pool_scale_r,
                                 w_x_out_b, w_o_b, ln1_g_r, ln1_b_r, l)
        xp, p_conv = _ffn_prompt(x1, batch, w_up_b, conv_w, conv_b_r, w_down_b, ln2_g_r,
                                 ln2_b_r, l)
        outs["pC"].append(p_c)
        outs["pn"].append(p_n)
        outs["pm"].append(p_m[:, 0, :N_HEADS])
        outs["ppool"].append(p_pool[:, 2 * SUBLANES - POOL_BUF:, :])
        outs["pconv"].append(p_conv[:, SUBLANES - (CONV_WIDTH - 1):, :])
        outs["pmk"].append(mk.reshape(batch, n_mem, N_HEADS, HEAD_DIM))
        outs["pmv"].append(mv.reshape(batch, n_mem, N_HEADS, HEAD_DIM))

        proj = _matmul_bias(xs, w_main, b_main, l, nseq, 3 * d // 2, "proj_sample")
        gates = _matmul_bias(xs, w_if, b_if, l, nseq, 2 * LANES, "gates_sample")
        hn, s_c, s_n, s_m = _mlstm_step(proj, gates, state_C, st_n, state_m, mh_g_r, l)
        ox = _xattn_step(proj, mem_k, mem_v, l)
        x1s, s_pool = _merge_step(xs, hn, ox, proj, st_pool, w_m_out_b, w_pool_b,
                                  pool_scale_r, w_x_out_b, w_o_b, ln1_g_r, ln1_b_r, l, nseq // 2)
        xs, s_conv = _ffn_sample(x1s, st_conv, w_up_b, conv_w, conv_b_r, w_down_b, ln2_g_r,
                                 ln2_b_r, l, nseq // 2)
        outs["sC"].append(s_c)
        outs["sn"].append(s_n.reshape(nseq, N_HEADS, HEAD_DIM))
        outs["sm"].append(s_m)
        outs["spool"].append(s_pool.reshape(nseq, POOL_BUF, d))
        outs["sconv"].append(s_conv.reshape(nseq, CONV_WIDTH - 1, 2 * d_ff))

    stacked = {k: jnp.stack(v) for k, v in outs.items()}
    return (xp.reshape(batch, seq, d), xs.reshape(nseq, 1, d),
            stacked["pC"], stacked["pn"], stacked["pm"], stacked["ppool"], stacked["pconv"],
            stacked["pmk"], stacked["pmv"],
            stacked["sC"], stacked["sn"], stacked["sm"], stacked["spool"], stacked["sconv"])
```

```python
import jax
import jax.numpy as jnp
from jax import lax
from jax.experimental import pallas as pl
from jax.experimental.pallas import tpu as pltpu

F32 = jnp.float32
BF16 = jnp.bfloat16

N_HEADS = 4
HEAD_DIM = 256
POOL_WINDOWS = (2, 4, 8, 16)
POOL_GROUP_DIM = 256
POOL_BUF = 15
CONV_WIDTH = 3
DEPTH = 4
PAST_LEN = 16384
ALPHA = (2.0 * DEPTH) ** 0.25
LN_EPS = 1e-5
NEG = -1e30

SUBLANES = 8
LANES = 128
TIME_TILE = 256
FF_CHUNK = 256
SAMPLE_BLOCK = 8
VMEM_LIMIT = 56 * 1024 * 1024

REST_U, REST_XQ, REST_G0, REST_G1, REST_G2 = range(5)


def _params(n_axes):
    return pltpu.CompilerParams(
        dimension_semantics=("arbitrary",) * n_axes, vmem_limit_bytes=VMEM_LIMIT)


def _dot(a, b):
    return jnp.dot(a, b, preferred_element_type=F32)


def _dot_nt(a, b):
    return lax.dot_general(a, b, (((1,), (1,)), ((), ())), preferred_element_type=F32)


def _dot_tn(a, b):
    return lax.dot_general(a, b, (((0,), (0,)), ((), ())), preferred_element_type=F32)


def _layer_norm(x, g, b):
    mu = jnp.mean(x, axis=-1, keepdims=True)
    xc = x - mu
    var = jnp.mean(xc * xc, axis=-1, keepdims=True)
    return xc * lax.rsqrt(var + LN_EPS) * g + b


def _head_norm(x):
    mu = jnp.mean(x, axis=-1, keepdims=True)
    xc = x - mu
    var = jnp.mean(xc * xc, axis=-1, keepdims=True)
    return xc * lax.rsqrt(var + LN_EPS)


def _log_sigmoid(x):
    return jnp.minimum(x, 0.0) - jnp.log1p(jnp.exp(-jnp.abs(x)))


def _scan_rows(x, op):
    n = x.shape[0]
    rows = lax.broadcasted_iota(jnp.int32, x.shape, 0)
    shift = 1
    while shift < n:
        x = jnp.where(rows >= shift, op(x, pltpu.roll(x, shift, 0)), x)
        shift *= 2
    return x


def _hs(h):
    return slice(h * HEAD_DIM, (h + 1) * HEAD_DIM)


def _ln_kernel(x_ref, g_ref, b_ref, o_ref):
    o_ref[...] = _layer_norm(x_ref[...], g_ref[...], b_ref[...])


def _layer_norm_rows(x, g, b, tile):
    rows, d = x.shape
    return pl.pallas_call(
        _ln_kernel,
        grid=(rows // tile,),
        in_specs=[pl.BlockSpec((tile, d), lambda i: (i, 0)),
                  pl.BlockSpec((1, d), lambda i: (0, 0)),
                  pl.BlockSpec((1, d), lambda i: (0, 0))],
        out_specs=pl.BlockSpec((tile, d), lambda i: (i, 0)),
        out_shape=jax.ShapeDtypeStruct((rows, d), F32),
        compiler_params=_params(1),
        name="ln_rows",
    )(x, g, b)


def _mm_kernel(x_ref, w_ref, b_ref, o_ref):
    o_ref[...] = _dot(x_ref[...].astype(BF16), w_ref[...]) + b_ref[...]


def _matmul_bias(x, w_all, b_all, layer, tm, tn, name):
    rows, kdim = x.shape
    n = w_all.shape[-1]
    return pl.pallas_call(
        _mm_kernel,
        grid=(n // tn, rows // tm),
        in_specs=[pl.BlockSpec((tm, kdim), lambda j, i: (i, 0)),
                  pl.BlockSpec((None, kdim, tn), lambda j, i: (layer, 0, j)),
                  pl.BlockSpec((None, 1, tn), lambda j, i: (layer, 0, j))],
        out_specs=pl.BlockSpec((tm, tn), lambda j, i: (i, j)),
        out_shape=jax.ShapeDtypeStruct((rows, n), F32),
        compiler_params=_params(2),
        name=name,
    )(x, w_all, b_all)


def _chain_inputs(prev):
    if prev is None:
        return [], [], 0
    specs = [pl.BlockSpec(memory_space=pl.ANY)] * len(prev)
    return specs, list(prev), len(prev)


def _mem_kv_kernel(x_ref, w_ref, *refs):
    k_ref, v_ref, kb_ref, vb_ref = refs[-4:]
    d = N_HEADS * HEAD_DIM
    res = _dot(x_ref[...].astype(BF16), w_ref[...])
    for h in range(N_HEADS):
        k_ref[:, h, :] = res[:, _hs(h)]
        v_ref[:, h, :] = res[:, d + h * HEAD_DIM:d + (h + 1) * HEAD_DIM]
    kb_ref[...] = res[:, :d].astype(BF16)
    vb_ref[...] = res[:, d:].astype(BF16)


def _mem_kv(mem2d, w_kv, prev, layer, batch):
    rows, d = mem2d.shape
    n_mem = rows // batch
    depth = w_kv.shape[0]
    chain_specs, chain_args, n_chain = _chain_inputs(prev)
    out_spec = pl.BlockSpec((None, None, n_mem, N_HEADS, HEAD_DIM),
                            lambda b: (layer, b, 0, 0, 0))
    out_sds = jax.ShapeDtypeStruct((depth, batch, n_mem, N_HEADS, HEAD_DIM), F32)
    dense_spec = pl.BlockSpec((n_mem, d), lambda b: (b, 0))
    dense_sds = jax.ShapeDtypeStruct((rows, d), BF16)
    k_all, v_all, kb, vb = pl.pallas_call(
        _mem_kv_kernel,
        grid=(batch,),
        in_specs=[pl.BlockSpec((n_mem, d), lambda b: (b, 0)),
                  pl.BlockSpec((None, d, 2 * d), lambda b: (layer, 0, 0))] + chain_specs,
        out_specs=[out_spec, out_spec, dense_spec, dense_spec],
        out_shape=[out_sds, out_sds, dense_sds, dense_sds],
        input_output_aliases={2 + i: i for i in range(n_chain)},
        compiler_params=_params(1),
        name="mem_kv",
    )(mem2d, w_kv, *chain_args)
    return (k_all, v_all), (kb, vb)


def _mlstm_prompt_kernel(x_ref, wqkvo_ref, bqkvo_ref, wif_ref, bif_ref, wg0_ref, bg0_ref,
                         mhg_ref, wmo_ref, y_ref, c_ref, n_ref, m_ref):
    t = pl.program_id(1)
    d = N_HEADS * HEAD_DIM

    @pl.when(t == 0)
    def _():
        c_ref[...] = jnp.zeros(c_ref.shape, F32)
        n_ref[...] = jnp.zeros(n_ref.shape, F32)
        m_ref[...] = jnp.zeros(m_ref.shape, F32)

    xb = x_ref[...].astype(BF16)
    lc = xb.shape[0]
    gi = _dot(xb, wif_ref[:, 0:LANES]) + bif_ref[:, 0:LANES]
    gf = _dot(xb, wif_ref[:, LANES:2 * LANES]) + bif_ref[:, LANES:2 * LANES]
    lf = _log_sigmoid(gf)
    bcum = _scan_rows(lf, jnp.add)
    a = gi - bcum
    m0 = m_ref[0]
    mt = bcum + jnp.maximum(m0, _scan_rows(a, jnp.maximum))
    inter = jnp.exp(m0 + bcum - mt)
    bm = bcum - mt
    emt = jnp.exp(-mt)
    b_last = bcum[lc - 1:lc, :]
    m_last = mt[lc - 1:lc, :]
    ws = jnp.exp(a + b_last - m_last)
    decay = jnp.exp(m0 + b_last - m_last)
    a_t = a.T
    causal = (lax.broadcasted_iota(jnp.int32, (lc, lc), 1)
              <= lax.broadcasted_iota(jnp.int32, (lc, lc), 0))

    hn_heads = []
    for h in range(N_HEADS):
        def proj(part):
            cols = slice(part * d + h * HEAD_DIM, part * d + (h + 1) * HEAD_DIM)
            return _dot(xb, wqkvo_ref[:, cols]) + bqkvo_ref[:, cols]
        q = proj(0)
        k = proj(1) * (HEAD_DIM ** -0.5)
        v = proj(2)
        o = proj(3)
        qb = q.astype(BF16)
        vb = v.astype(BF16)
        col = slice(h, h + 1)
        logd = a_t[h:h + 1, :] + bm[:, col]
        dmat = jnp.exp(jnp.where(causal, logd, NEG))
        s = _dot_nt(qb, k.astype(BF16)) * dmat
        c_old = c_ref[0, h]
        n_old = n_ref[0, h:h + 1, :]
        inter_h = inter[:, col]
        num = _dot(s.astype(BF16), vb) + inter_h * _dot(qb, c_old.astype(BF16))
        den = (jnp.sum(s, axis=1, keepdims=True)
               + inter_h * jnp.sum(q * n_old, axis=1, keepdims=True))
        ht = num / jnp.maximum(jnp.abs(den), emt[:, col])
        kw = k * ws[:, col]
        c_ref[0, h] = decay[:, col] * c_old + _dot_tn(kw.astype(BF16), vb)
        n_ref[0, h:h + 1, :] = decay[:, col] * n_old + jnp.sum(kw, axis=0, keepdims=True)
        hn = _head_norm(jax.nn.sigmoid(o) * ht) * mhg_ref[:, _hs(h)]
        hn_heads.append(hn.astype(BF16))
    m_ref[0] = m_last

    ym = _dot(jnp.concatenate(hn_heads, axis=1), wmo_ref[...])
    g0 = jax.nn.sigmoid(_dot(xb, wg0_ref[...]) + bg0_ref[...])
    y_ref[...] = g0 * ym


def _mlstm_prompt(x2d, batch, w_qkvo, b_qkvo, w_if, b_if, w_rest, b_rest, mh_g, w_m_out, layer):
    rows, d = x2d.shape
    nt = rows // batch // TIME_TILE
    wide = N_HEADS * d

    def lw(shape, col):
        return pl.BlockSpec((None,) + shape, lambda b, t: (layer, 0, col))

    return pl.pallas_call(
        _mlstm_prompt_kernel,
        grid=(batch, nt),
        in_specs=[pl.BlockSpec((TIME_TILE, d), lambda b, t: (b * nt + t, 0)),
                  lw((d, wide), 0), lw((1, wide), 0),
                  lw((d, 2 * LANES), 0), lw((1, 2 * LANES), 0),
                  lw((d, d), REST_G0), lw((1, d), REST_G0),
                  lw((1, d), 0), lw((d, d), 0)],
        out_specs=[pl.BlockSpec((TIME_TILE, d), lambda b, t: (b * nt + t, 0)),
                   pl.BlockSpec((1, N_HEADS, HEAD_DIM, HEAD_DIM), lambda b, t: (b, 0, 0, 0)),
                   pl.BlockSpec((1, N_HEADS, HEAD_DIM), lambda b, t: (b, 0, 0)),
                   pl.BlockSpec((1, 1, LANES), lambda b, t: (b, 0, 0))],
        out_shape=[jax.ShapeDtypeStruct((rows, d), F32),
                   jax.ShapeDtypeStruct((batch, N_HEADS, HEAD_DIM, HEAD_DIM), F32),
                   jax.ShapeDtypeStruct((batch, N_HEADS, HEAD_DIM), F32),
                   jax.ShapeDtypeStruct((batch, 1, LANES), F32)],
        compiler_params=_params(2),
        name="mlstm_prompt",
    )(x2d, w_qkvo, b_qkvo, w_if, b_if, w_rest, b_rest, mh_g, w_m_out)


def _pool_window_sums(ext, window):
    s, span = ext, 1
    while span < window:
        s = s + pltpu.roll(s, span, 0)
        span *= 2
    return s


def _xattn_head(qb, kb, vb):
    s = _dot_nt(qb, kb) * (HEAD_DIM ** -0.5)
    s = s - jnp.max(s, axis=-1, keepdims=True)
    p = jnp.exp(s)
    p = p / jnp.sum(p, axis=-1, keepdims=True)
    return _dot(p.astype(BF16), vb)


def _mix_prompt_kernel(x_ref, ymg_ref, mk_ref, mv_ref, wu_ref, bu_ref, wxq_ref, bxq_ref,
                       wg1_ref, bg1_ref, wg2_ref, bg2_ref, wpool_ref, pscale_ref, wxo_ref,
                       wo_ref, lng_ref, lnb_ref, x1_ref, pool_ref, ext_scr):
    t = pl.program_id(1)
    tt = x_ref.shape[0]
    carry = 2 * SUBLANES

    @pl.when(t == 0)
    def _():
        ext_scr[0:carry, :] = jnp.zeros((carry, ext_scr.shape[1]), F32)

    x = x_ref[...]
    xb = x.astype(BF16)

    u = _dot(xb, wu_ref[...]) + bu_ref[...]
    ext_scr[carry:carry + tt, :] = u
    pos = t * tt + lax.broadcasted_iota(jnp.int32, (tt, 1), 0)
    yp_groups = []
    for g, window in enumerate(POOL_WINDOWS):
        cols = slice(g * POOL_GROUP_DIM, (g + 1) * POOL_GROUP_DIM)
        wsum = _pool_window_sums(ext_scr[:, cols], window)[carry:, :]
        cnt = jnp.minimum(window, pos + 1).astype(F32)
        dlt = wsum / cnt - u[:, cols]
        yp_groups.append(_dot(dlt.astype(BF16), wpool_ref[g]))
    yp = jnp.concatenate(yp_groups, axis=1) * pscale_ref[...]
    ext_scr[0:carry, :] = u[tt - carry:tt, :]
    pool_ref[0] = u[tt - carry:tt, :]

    xq = _dot(xb, wxq_ref[...]) + bxq_ref[...]
    heads = [_xattn_head(xq[:, _hs(h)].astype(BF16), mk_ref[:, _hs(h)], mv_ref[:, _hs(h)])
             for h in range(N_HEADS)]
    yx = _dot(jnp.concatenate(heads, axis=1).astype(BF16), wxo_ref[...])

    g1 = jax.nn.sigmoid(_dot(xb, wg1_ref[...]) + bg1_ref[...])
    g2 = jax.nn.sigmoid(_dot(xb, wg2_ref[...]) + bg2_ref[...])
    mix = ymg_ref[...] + g1 * yp + g2 * yx
    y = ALPHA * x + _dot(mix.astype(BF16), wo_ref[...])
    x1_ref[...] = _layer_norm(y, lng_ref[...], lnb_ref[...])


def _mix_prompt(x2d, ymg, mk, mv, batch, w_rest, b_rest, w_pool, pool_scale, w_x_out, w_o,
                ln_g, ln_b, layer):
    rows, d = x2d.shape
    nt = rows // batch // TIME_TILE
    n_mem = mk.shape[0] // batch
    carry = 2 * SUBLANES

    def lw(shape, col):
        return pl.BlockSpec((None,) + shape, lambda b, t: (layer, 0, col))

    row_spec = pl.BlockSpec((TIME_TILE, d), lambda b, t: (b * nt + t, 0))
    mem_spec = pl.BlockSpec((n_mem, d), lambda b, t: (b, 0))
    return pl.pallas_call(
        _mix_prompt_kernel,
        grid=(batch, nt),
        in_specs=[row_spec, row_spec, mem_spec, mem_spec,
                  lw((d, d), REST_U), lw((1, d), REST_U),
                  lw((d, d), REST_XQ), lw((1, d), REST_XQ),
                  lw((d, d), REST_G1), lw((1, d), REST_G1),
                  lw((d, d), REST_G2), lw((1, d), REST_G2),
                  pl.BlockSpec((None, len(POOL_WINDOWS), POOL_GROUP_DIM, POOL_GROUP_DIM),
                               lambda b, t: (layer, 0, 0, 0)),
                  lw((1, d), 0), lw((d, d), 0), lw((d, d), 0), lw((1, d), 0), lw((1, d), 0)],
        out_specs=[row_spec, pl.BlockSpec((1, carry, d), lambda b, t: (b, 0, 0))],
        out_shape=[jax.ShapeDtypeStruct((rows, d), F32),
                   jax.ShapeDtypeStruct((batch, carry, d), F32)],
        scratch_shapes=[pltpu.VMEM((carry + TIME_TILE, d), F32)],
        compiler_params=_params(2),
        name="mix_prompt",
    )(x2d, ymg, mk, mv, w_rest, b_rest, w_rest, b_rest, w_rest, b_rest, w_rest, b_rest,
      w_pool, pool_scale, w_x_out, w_o, ln_g, ln_b)


def _ffn_chunk(xb, j, d_ff, wup_ref, cw_ref, cb_ref, wdn_ref, conv_taps):
    halves = []
    conv = []
    for base in (0, d_ff):
        cols = slice(base + j * FF_CHUNK, base + (j + 1) * FF_CHUNK)
        hup = _dot(xb, wup_ref[:, cols])
        prev2, prev1 = conv_taps(hup, cols)
        conv.append(prev2 * cw_ref[0:1, cols] + prev1 * cw_ref[1:2, cols]
                    + hup * cw_ref[2:3, cols] + cb_ref[:, cols])
        halves.append(hup)
    act = jax.nn.gelu(conv[0]) * conv[1]
    part = _dot(act.astype(BF16), wdn_ref[j * FF_CHUNK:(j + 1) * FF_CHUNK, :])
    return part, halves


def _ffn_prompt_kernel(x_ref, wup_ref, cw_ref, cb_ref, wdn_ref, lng_ref, lnb_ref,
                       o_ref, conv_ref):
    t = pl.program_id(1)
    tt = x_ref.shape[0]
    d_ff = wdn_ref.shape[0]

    @pl.when(t == 0)
    def _():
        conv_ref[...] = jnp.zeros(conv_ref.shape, F32)

    x = x_ref[...]
    xb = x.astype(BF16)

    def conv_taps(hup, cols):
        ext = jnp.concatenate([conv_ref[0, :, cols], hup], axis=0)
        return (pltpu.roll(ext, 2, 0)[SUBLANES:, :], pltpu.roll(ext, 1, 0)[SUBLANES:, :])

    acc = None
    for j in range(d_ff // FF_CHUNK):
        part, halves = _ffn_chunk(xb, j, d_ff, wup_ref, cw_ref, cb_ref, wdn_ref, conv_taps)
        for base, hup in zip((0, d_ff), halves):
            conv_ref[0, :, base + j * FF_CHUNK:base + (j + 1) * FF_CHUNK] = hup[tt - SUBLANES:, :]
        acc = part if acc is None else acc + part
    o_ref[...] = _layer_norm(ALPHA * x + acc, lng_ref[...], lnb_ref[...])


def _ffn_prompt(x2d, batch, w_up, conv_w, conv_b, w_down, ln_g, ln_b, layer):
    rows, d = x2d.shape
    nt = rows // batch // TIME_TILE
    d_ff = w_down.shape[1]

    def lw(shape):
        return pl.BlockSpec((None,) + shape, lambda b, t: (layer, 0, 0))

    row_spec = pl.BlockSpec((TIME_TILE, d), lambda b, t: (b * nt + t, 0))
    return pl.pallas_call(
        _ffn_prompt_kernel,
        grid=(batch, nt),
        in_specs=[row_spec, lw((d, 2 * d_ff)), lw((CONV_WIDTH, 2 * d_ff)), lw((1, 2 * d_ff)),
                  lw((d_ff, d)), lw((1, d)), lw((1, d))],
        out_specs=[row_spec, pl.BlockSpec((1, SUBLANES, 2 * d_ff), lambda b, t: (b, 0, 0))],
        out_shape=[jax.ShapeDtypeStruct((rows, d), F32),
                   jax.ShapeDtypeStruct((batch, SUBLANES, 2 * d_ff), F32)],
        compiler_params=_params(2),
        name="ffn_prompt",
    )(x2d, w_up, conv_w, conv_b, w_down, ln_g, ln_b)


def _ffn_sample_kernel(x_ref, st_ref, wup_ref, cw_ref, cb_ref, wdn_ref, lng_ref, lnb_ref,
                       o_ref, stn_ref):
    d_ff = wdn_ref.shape[0]
    x = x_ref[...]
    xb = x.astype(BF16)

    def conv_taps(hup, cols):
        return st_ref[:, 0, cols], st_ref[:, 1, cols]

    acc = None
    for j in range(d_ff // FF_CHUNK):
        part, halves = _ffn_chunk(xb, j, d_ff, wup_ref, cw_ref, cb_ref, wdn_ref, conv_taps)
        for base, hup in zip((0, d_ff), halves):
            cols = slice(base + j * FF_CHUNK, base + (j + 1) * FF_CHUNK)
            stn_ref[:, 0, cols] = st_ref[:, 1, cols]
            stn_ref[:, 1, cols] = hup
        acc = part if acc is None else acc + part
    o_ref[...] = _layer_norm(ALPHA * x + acc, lng_ref[...], lnb_ref[...])


def _ffn_sample(x, conv_state, w_up, conv_w, conv_b, w_down, ln_g, ln_b, layer, tile):
    rows, d = x.shape
    d_ff = w_down.shape[1]
    st_shape = conv_state.shape[2:]

    def lw(shape):
        return pl.BlockSpec((None,) + shape, lambda i: (layer, 0, 0))

    return pl.pallas_call(
        _ffn_sample_kernel,
        grid=(rows // tile,),
        in_specs=[pl.BlockSpec((tile, d), lambda i: (i, 0)),
                  pl.BlockSpec((None, tile) + st_shape, lambda i: (layer, i, 0, 0)),
                  lw((d, 2 * d_ff)), lw((CONV_WIDTH, 2 * d_ff)), lw((1, 2 * d_ff)),
                  lw((d_ff, d)), lw((1, d)), lw((1, d))],
        out_specs=[pl.BlockSpec((tile, d), lambda i: (i, 0)),
                   pl.BlockSpec((tile,) + st_shape, lambda i: (i, 0, 0))],
        out_shape=[jax.ShapeDtypeStruct((rows, d), F32),
                   jax.ShapeDtypeStruct((rows,) + st_shape, F32)],
        compiler_params=_params(1),
        name="ffn_sample",
    )(x, conv_state, w_up, conv_w, conv_b, w_down, ln_g, ln_b)


def _mlstm_step_kernel(q_ref, k_ref, v_ref, o_ref, g_ref, c_ref, n_ref, m_ref, mhg_ref, *refs):
    hn_ref, cn_ref, nn_ref, mn_ref, kw_scr, dec_scr, r_scr = refs[-7:]
    bb = q_ref.shape[0]
    mxu_rows = 2 * SUBLANES
    gi = g_ref[:, 0:N_HEADS]
    lf = _log_sigmoid(g_ref[:, LANES:LANES + N_HEADS])
    m0 = m_ref[...]
    mt = jnp.maximum(m0 + lf, gi)
    inter_all = jnp.exp(m0 + lf - mt)
    dm_all = jnp.exp(gi - mt)
    emt_all = jnp.exp(-mt)
    mn_ref[...] = mt

    stats = []
    for h in range(N_HEADS):
        col = slice(h, h + 1)
        q = q_ref[:, _hs(h)]
        k = k_ref[:, _hs(h)] * (HEAD_DIM ** -0.5)
        n_old = n_ref[:, h, :]
        inter, dm = inter_all[:, col], dm_all[:, col]
        s = jnp.sum(q * k, axis=1, keepdims=True) * dm
        den = s + inter * jnp.sum(q * n_old, axis=1, keepdims=True)
        kw = k * dm
        kw_scr[:, _hs(h)] = kw
        dec_scr[:, _hs(h)] = jnp.broadcast_to(inter, kw.shape)
        nn_ref[:, h, :] = inter * n_old + kw
        stats.append((s, inter, den, emt_all[:, col]))

    row0 = lax.broadcasted_iota(jnp.int32, (mxu_rows, HEAD_DIM), 0) == 0

    def per_sequence(b, carry):
        row = pl.ds(b, 1)
        for h in range(N_HEADS):
            qrow = jnp.broadcast_to(q_ref[row, _hs(h)], (mxu_rows, HEAD_DIM)).astype(BF16)
            c_old = c_ref[b, h]
            r_scr[row, _hs(h)] = _dot(qrow, c_old.astype(BF16))[0:1, :]
            kw = jnp.where(row0, jnp.broadcast_to(kw_scr[row, _hs(h)], (mxu_rows, HEAD_DIM)), 0.0)
            vrow = jnp.broadcast_to(v_ref[row, _hs(h)], (mxu_rows, HEAD_DIM))
            cn_ref[b, h] = (dec_scr[row, _hs(h)] * c_old
                            + _dot_tn(kw.astype(BF16), vrow.astype(BF16)))
        return carry

    lax.fori_loop(0, bb, per_sequence, 0)

    for h in range(N_HEADS):
        s, inter, den, emt = stats[h]
        ht = (s * v_ref[:, _hs(h)] + inter * r_scr[:, _hs(h)]) / jnp.maximum(jnp.abs(den), emt)
        hc = jax.nn.sigmoid(o_ref[:, _hs(h)]) * ht
        hn_ref[:, _hs(h)] = _head_norm(hc) * mhg_ref[:, _hs(h)]


def _mlstm_step(proj, gates, state_c, state_n, state_m, mh_g, prev_c, layer):
    nseq = proj.shape[0]
    d = N_HEADS * HEAD_DIM
    bb = SAMPLE_BLOCK
    chain_specs, chain_args, n_chain = _chain_inputs(None if prev_c is None else (prev_c,))

    def pcol(c):
        return pl.BlockSpec((bb, d), lambda i: (i, c))

    cspec = pl.BlockSpec((None, bb, N_HEADS, HEAD_DIM, HEAD_DIM), lambda i: (layer, i, 0, 0, 0))
    nspec_in = pl.BlockSpec((None, bb, N_HEADS, HEAD_DIM), lambda i: (layer, i, 0, 0))
    return pl.pallas_call(
        _mlstm_step_kernel,
        grid=(nseq // bb,),
        in_specs=[pcol(0), pcol(1), pcol(2), pcol(3),
                  pl.BlockSpec((bb, 2 * LANES), lambda i: (i, 0)),
                  cspec, nspec_in,
                  pl.BlockSpec((None, bb, N_HEADS), lambda i: (layer, i, 0)),
                  pl.BlockSpec((None, 1, d), lambda i: (layer, 0, 0))] + chain_specs,
        out_specs=[pl.BlockSpec((bb, d), lambda i: (i, 0)),
                   cspec,
                   pl.BlockSpec((bb, N_HEADS, HEAD_DIM), lambda i: (i, 0, 0)),
                   pl.BlockSpec((bb, N_HEADS), lambda i: (i, 0))],
        out_shape=[jax.ShapeDtypeStruct((nseq, d), F32),
                   jax.ShapeDtypeStruct(state_c.shape, F32),
                   jax.ShapeDtypeStruct((nseq, N_HEADS, HEAD_DIM), F32),
                   jax.ShapeDtypeStruct((nseq, N_HEADS), F32)],
        input_output_aliases={9 + i: 1 for i in range(n_chain)},
        scratch_shapes=[pltpu.VMEM((bb, d), F32)] * 3,
        compiler_params=_params(1),
        name="mlstm_step",
    )(proj, proj, proj, proj, gates, state_c, state_n, state_m, mh_g, *chain_args)


def _xattn_step_kernel(xq_ref, k_ref, v_ref, o_ref, s_scr):
    bb = xq_ref.shape[0]
    n_mem = k_ref.shape[1]
    rows = SUBLANES
    sub = lax.broadcasted_iota(jnp.int32, (rows, HEAD_DIM), 0)

    def scores(b, carry):
        q = jnp.zeros((rows, HEAD_DIM), F32)
        for h in range(N_HEADS):
            qrow = jnp.broadcast_to(xq_ref[pl.ds(b, 1), _hs(h)], (rows, HEAD_DIM))
            q = jnp.where(sub == h, qrow, q)
        keys = k_ref[b].reshape(n_mem * N_HEADS, HEAD_DIM).astype(BF16)
        s_scr[pl.ds(pl.multiple_of(b * rows, rows), rows), :] = _dot_nt(q.astype(BF16), keys)
        return carry

    lax.fori_loop(0, bb, scores, 0)

    s = s_scr[...] * (HEAD_DIM ** -0.5)
    head = lax.broadcasted_iota(jnp.int32, s.shape, 0) % rows
    col_head = lax.broadcasted_iota(jnp.int32, s.shape, 1) % N_HEADS
    s = jnp.where((col_head == head) | (head >= N_HEADS), s, -jnp.inf)
    p = jnp.exp(s - jnp.max(s, axis=-1, keepdims=True))
    s_scr[...] = p / jnp.sum(p, axis=-1, keepdims=True)

    def outputs(b, carry):
        p = s_scr[pl.ds(pl.multiple_of(b * rows, rows), rows), :]
        vals = v_ref[b].reshape(n_mem * N_HEADS, HEAD_DIM).astype(BF16)
        out = _dot(p.astype(BF16), vals)
        for h in range(N_HEADS):
            o_ref[pl.ds(b, 1), _hs(h)] = out[h:h + 1, :]
        return carry

    lax.fori_loop(0, bb, outputs, 0)


def _xattn_step(proj_rest, mem_k, mem_v, layer):
    nseq = proj_rest.shape[0]
    d = N_HEADS * HEAD_DIM
    n_mem = mem_k.shape[2]
    bb = SAMPLE_BLOCK
    kv_spec = pl.BlockSpec((None, bb, n_mem, N_HEADS, HEAD_DIM), lambda i: (layer, i, 0, 0, 0))
    return pl.pallas_call(
        _xattn_step_kernel,
        grid=(nseq // bb,),
        in_specs=[pl.BlockSpec((bb, d), lambda i: (i, REST_XQ)), kv_spec, kv_spec],
        out_specs=pl.BlockSpec((bb, d), lambda i: (i, 0)),
        out_shape=jax.ShapeDtypeStruct((nseq, d), F32),
        scratch_shapes=[pltpu.VMEM((SUBLANES * bb, N_HEADS * n_mem), F32)],
        compiler_params=_params(1),
        name="xattn_step",
    )(proj_rest, mem_k, mem_v)


def _merge_step_kernel(x_ref, hn_ref, ox_ref, u_ref, g0_ref, g1_ref, g2_ref, pool_ref,
                       wmo_ref, wpool_ref, pscale_ref, wxo_ref, wo_ref, lng_ref, lnb_ref,
                       x1_ref, pooln_ref):
    x = x_ref[...]
    u = u_ref[...]
    for r in range(1, POOL_BUF):
        pooln_ref[:, r - 1, :] = pool_ref[:, r, :]
    pooln_ref[:, POOL_BUF - 1, :] = u
    yp_groups = []
    for g, window in enumerate(POOL_WINDOWS):
        cols = slice(g * POOL_GROUP_DIM, (g + 1) * POOL_GROUP_DIM)
        wsum = u[:, cols]
        for back in range(1, window):
            wsum = wsum + pool_ref[:, POOL_BUF - back, cols]
        cnt = float(min(window, PAST_LEN + 1))
        dlt = wsum / cnt - u[:, cols]
        yp_groups.append(_dot(dlt.astype(BF16), wpool_ref[g]))
    yp = jnp.concatenate(yp_groups, axis=1) * pscale_ref[...]
    ym = _dot(hn_ref[...].astype(BF16), wmo_ref[...])
    yx = _dot(ox_ref[...].astype(BF16), wxo_ref[...])
    mix = (jax.nn.sigmoid(g0_ref[...]) * ym + jax.nn.sigmoid(g1_ref[...]) * yp
           + jax.nn.sigmoid(g2_ref[...]) * yx)
    y = ALPHA * x + _dot(mix.astype(BF16), wo_ref[...])
    x1_ref[...] = _layer_norm(y, lng_ref[...], lnb_ref[...])


def _merge_step(x, hn, ox, proj_rest, pool_state, w_m_out, w_pool, pool_scale, w_x_out, w_o,
                ln_g, ln_b, layer, tile):
    rows, d = x.shape
    st_shape = pool_state.shape[2:]

    def lw(shape):
        return pl.BlockSpec((None,) + shape, lambda i: (layer,) + (0,) * len(shape))

    def rows_at(col):
        return pl.BlockSpec((tile, d), lambda i: (i, col))

    return pl.pallas_call(
        _merge_step_kernel,
        grid=(rows // tile,),
        in_specs=[rows_at(0), rows_at(0), rows_at(0), rows_at(REST_U), rows_at(REST_G0),
                  rows_at(REST_G1), rows_at(REST_G2),
                  pl.BlockSpec((None, tile) + st_shape, lambda i: (layer, i, 0, 0)),
                  lw((d, d)), lw((len(POOL_WINDOWS), POOL_GROUP_DIM, POOL_GROUP_DIM)),
                  lw((1, d)), lw((d, d)), lw((d, d)), lw((1, d)), lw((1, d))],
        out_specs=[rows_at(0), pl.BlockSpec((tile,) + st_shape, lambda i: (i, 0, 0))],
        out_shape=[jax.ShapeDtypeStruct((rows, d), F32),
                   jax.ShapeDtypeStruct((rows,) + st_shape, F32)],
        compiler_params=_params(1),
        name="merge_step",
    )(x, hn, ox, proj_rest, proj_rest, proj_rest, proj_rest, pool_state, w_m_out, w_pool,
      pool_scale, w_x_out, w_o, ln_g, ln_b)


def kernel(x_prompt, mem_prompt, x_sample, cache_mem_k, cache_mem_v, state_C, state_n, state_m,
           state_pool, state_conv, ln_in_g, ln_in_b, w_in, b_in, mh_g, w_m_out, w_pool,
           pool_scale, w_mem_kv, w_x_out, w_o, ln1_g, ln1_b, w_up, conv_w, conv_b, w_down,
           ln2_g, ln2_b):
    batch, seq, d = x_prompt.shape
    nseq = x_sample.shape[0]
    n_mem = mem_prompt.shape[1]
    depth = w_in.shape[0]
    d_ff = w_down.shape[1]
    wide = N_HEADS * d
    gate_off = wide
    rest_off = wide + 2 * N_HEADS
    assert d == N_HEADS * HEAD_DIM and seq % TIME_TILE == 0 and d_ff % FF_CHUNK == 0
    assert depth == DEPTH and nseq % SAMPLE_BLOCK == 0 and x_sample.shape[1] == 1

    def row(v):
        return v.reshape(depth, 1, v.shape[-1])

    w_qkvo, w_rest = w_in[:, :, :wide].astype(BF16), w_in[:, :, rest_off:].astype(BF16)
    b_qkvo, b_rest = row(b_in[:, :wide]), row(b_in[:, rest_off:])
    lane_pad = ((0, 0), (0, 0), (0, LANES - N_HEADS))
    w_if = jnp.concatenate(
        [jnp.pad(w_in[:, :, gate_off:gate_off + N_HEADS], lane_pad),
         jnp.pad(w_in[:, :, gate_off + N_HEADS:rest_off], lane_pad)], axis=2).astype(BF16)
    b_if = row(jnp.concatenate(
        [jnp.pad(b_in[:, gate_off:gate_off + N_HEADS], lane_pad[1:]),
         jnp.pad(b_in[:, gate_off + N_HEADS:rest_off], lane_pad[1:])], axis=1))
    w_m_out_b, w_pool_b, w_x_out_b, w_o_b = (w.astype(BF16) for w in (w_m_out, w_pool, w_x_out, w_o))
    w_up_b, w_down_b = w_up.astype(BF16), w_down.astype(BF16)
    w_kv_b = w_mem_kv.astype(BF16)
    mh_g_r, pool_scale_r, conv_b_r = row(mh_g), row(pool_scale), row(conv_b)
    ln1_g_r, ln1_b_r, ln2_g_r, ln2_b_r = row(ln1_g), row(ln1_b), row(ln2_g), row(ln2_b)

    xp = x_prompt.reshape(batch * seq, d)
    xs = x_sample.reshape(nseq, d)
    mem2d = mem_prompt.reshape(batch * n_mem, d)

    xp = _layer_norm_rows(xp, ln_in_g.reshape(1, d), ln_in_b.reshape(1, d), 2 * TIME_TILE)
    xs = _layer_norm_rows(xs, ln_in_g.reshape(1, d), ln_in_b.reshape(1, d), nseq)

    outs = {name: [] for name in ("pC", "pn", "pm", "ppool", "pconv", "sn", "sm", "spool", "sconv")}
    p_mem = None
    s_c = None
    for l in range(depth):
        p_mem, (mk_b, mv_b) = _mem_kv(mem2d, w_kv_b, p_mem, l, batch)
        ymg, p_c, p_n, p_m = _mlstm_prompt(xp, batch, w_qkvo, b_qkvo, w_if, b_if, w_rest, b_rest,
                                           mh_g_r, w_m_out_b, l)
        x1, p_pool = _mix_prompt(xp, ymg, mk_b, mv_b, batch, w_rest, b_rest, w_pool_b,
                                 pool_scale_r, w_x_out_b, w_o_b, ln1_g_r, ln1_b_r, l)
        xp, p_conv = _ffn_prompt(x1, batch, w_up_b, conv_w, conv_b_r, w_down_b, ln2_g_r,
                                 ln2_b_r, l)
        outs["pC"].append(p_c)
        outs["pn"].append(p_n)
        outs["pm"].append(p_m[:, 0, :N_HEADS])
        outs["ppool"].append(p_pool[:, 2 * SUBLANES - POOL_BUF:, :])
        outs["pconv"].append(p_conv[:, SUBLANES - (CONV_WIDTH - 1):, :])

        proj_qkvo = _matmul_bias(xs, w_qkvo, b_qkvo, l, nseq, d, "proj_sample")
        proj_rest = _matmul_bias(xs, w_rest, b_rest, l, nseq, d, "proj_sample")
        gates = _matmul_bias(xs, w_if, b_if, l, nseq, 2 * LANES, "gates_sample")
        hn, s_c, s_n, s_m = _mlstm_step(proj_qkvo, gates, state_C, state_n, state_m, mh_g_r,
                                        s_c, l)
        ox = _xattn_step(proj_rest, cache_mem_k, cache_mem_v, l)
        x1s, s_pool = _merge_step(xs, hn, ox, proj_rest, state_pool, w_m_out_b, w_pool_b,
                                  pool_scale_r, w_x_out_b, w_o_b, ln1_g_r, ln1_b_r, l, nseq // 2)
        xs, s_conv = _ffn_sample(x1s, state_conv, w_up_b, conv_w, conv_b_r, w_down_b, ln2_g_r,
                                 ln2_b_r, l, nseq // 2)
        outs["sn"].append(s_n)
        outs["sm"].append(s_m)
        outs["spool"].append(s_pool)
        outs["sconv"].append(s_conv)

    stacked = {k: jnp.stack(v) for k, v in outs.items()}
    return (xp.reshape(batch, seq, d), xs.reshape(nseq, 1, d),
            stacked["pC"], stacked["pn"], stacked["pm"], stacked["ppool"], stacked["pconv"],
            p_mem[0], p_mem[1],
            s_c, stacked["sn"], stacked["sm"], stacked["spool"], stacked["sconv"])
```

```python
import jax
import jax.numpy as jnp
from jax import lax
from jax.experimental import pallas as pl
from jax.experimental.pallas import tpu as pltpu

F32 = jnp.float32
BF16 = jnp.bfloat16

N_HEADS = 4
HEAD_DIM = 256
POOL_WINDOWS = (2, 4, 8, 16)
POOL_GROUP_DIM = 256
POOL_BUF = 15
CONV_WIDTH = 3
DEPTH = 4
PAST_LEN = 16384
ALPHA = (2.0 * DEPTH) ** 0.25
LN_EPS = 1e-5
NEG = -1e30

SUBLANES = 8
LANES = 128
TIME_TILE = 256
FF_CHUNK = 256
FF_TILE = 512
FF_SUB_ROWS = 256
FF_LOOKAHEAD = 3
SAMPLE_BLOCK = 8
VMEM_LIMIT = 56 * 1024 * 1024

REST_U, REST_XQ, REST_G0, REST_G1, REST_G2 = range(5)


def _params(n_axes):
    return pltpu.CompilerParams(
        dimension_semantics=("arbitrary",) * n_axes, vmem_limit_bytes=VMEM_LIMIT)


def _dot(a, b):
    return jnp.dot(a, b, preferred_element_type=F32)


def _dot_nt(a, b):
    return lax.dot_general(a, b, (((1,), (1,)), ((), ())), preferred_element_type=F32)


def _dot_tn(a, b):
    return lax.dot_general(a, b, (((0,), (0,)), ((), ())), preferred_element_type=F32)


def _layer_norm(x, g, b):
    mu = jnp.mean(x, axis=-1, keepdims=True)
    xc = x - mu
    var = jnp.mean(xc * xc, axis=-1, keepdims=True)
    return xc * lax.rsqrt(var + LN_EPS) * g + b


def _head_norm(x):
    mu = jnp.mean(x, axis=-1, keepdims=True)
    xc = x - mu
    var = jnp.mean(xc * xc, axis=-1, keepdims=True)
    return xc * lax.rsqrt(var + LN_EPS)


def _log_sigmoid(x):
    return jnp.minimum(x, 0.0) - jnp.log1p(jnp.exp(-jnp.abs(x)))


def _scan_rows(x, op):
    n = x.shape[0]
    rows = lax.broadcasted_iota(jnp.int32, x.shape, 0)
    shift = 1
    while shift < n:
        x = jnp.where(rows >= shift, op(x, pltpu.roll(x, shift, 0)), x)
        shift *= 2
    return x


def _hs(h):
    return slice(h * HEAD_DIM, (h + 1) * HEAD_DIM)


def _ln_kernel(x_ref, g_ref, b_ref, o_ref):
    o_ref[...] = _layer_norm(x_ref[...], g_ref[...], b_ref[...])


def _layer_norm_rows(x, g, b, tile):
    rows, d = x.shape
    return pl.pallas_call(
        _ln_kernel,
        grid=(rows // tile,),
        in_specs=[pl.BlockSpec((tile, d), lambda i: (i, 0)),
                  pl.BlockSpec((1, d), lambda i: (0, 0)),
                  pl.BlockSpec((1, d), lambda i: (0, 0))],
        out_specs=pl.BlockSpec((tile, d), lambda i: (i, 0)),
        out_shape=jax.ShapeDtypeStruct((rows, d), F32),
        compiler_params=_params(1),
        name="ln_rows",
    )(x, g, b)


def _mm_kernel(x_ref, w_ref, b_ref, o_ref):
    o_ref[...] = _dot(x_ref[...].astype(BF16), w_ref[...]) + b_ref[...]


def _matmul_bias(x, w_all, b_all, layer, tm, tn, name):
    rows, kdim = x.shape
    n = w_all.shape[-1]
    return pl.pallas_call(
        _mm_kernel,
        grid=(n // tn, rows // tm),
        in_specs=[pl.BlockSpec((tm, kdim), lambda j, i: (i, 0)),
                  pl.BlockSpec((None, kdim, tn), lambda j, i: (layer, 0, j)),
                  pl.BlockSpec((None, 1, tn), lambda j, i: (layer, 0, j))],
        out_specs=pl.BlockSpec((tm, tn), lambda j, i: (i, j)),
        out_shape=jax.ShapeDtypeStruct((rows, n), F32),
        compiler_params=_params(2),
        name=name,
    )(x, w_all, b_all)


def _chain_inputs(prev):
    specs = [pl.BlockSpec(memory_space=pl.ANY)] * len(prev)
    return specs, list(prev), len(prev)


def _mem_kv_kernel(x_ref, w_ref, *refs):
    k_ref, v_ref, kb_ref, vb_ref = refs[-4:]
    d = N_HEADS * HEAD_DIM
    res = _dot(x_ref[...].astype(BF16), w_ref[...])
    for h in range(N_HEADS):
        k_ref[:, h, :] = res[:, _hs(h)]
        v_ref[:, h, :] = res[:, d + h * HEAD_DIM:d + (h + 1) * HEAD_DIM]
    kb_ref[...] = res[:, :d].astype(BF16)
    vb_ref[...] = res[:, d:].astype(BF16)


def _mem_kv(mem2d, w_kv, prev, layer, batch):
    rows, d = mem2d.shape
    n_mem = rows // batch
    depth = w_kv.shape[0]
    chain_specs, chain_args, n_chain = _chain_inputs(prev)
    out_spec = pl.BlockSpec((None, None, n_mem, N_HEADS, HEAD_DIM),
                            lambda b: (layer, b, 0, 0, 0))
    out_sds = jax.ShapeDtypeStruct((depth, batch, n_mem, N_HEADS, HEAD_DIM), F32)
    dense_spec = pl.BlockSpec((n_mem, d), lambda b: (b, 0))
    dense_sds = jax.ShapeDtypeStruct((rows, d), BF16)
    k_all, v_all, kb, vb = pl.pallas_call(
        _mem_kv_kernel,
        grid=(batch,),
        in_specs=[pl.BlockSpec((n_mem, d), lambda b: (b, 0)),
                  pl.BlockSpec((None, d, 2 * d), lambda b: (layer, 0, 0))] + chain_specs,
        out_specs=[out_spec, out_spec, dense_spec, dense_spec],
        out_shape=[out_sds, out_sds, dense_sds, dense_sds],
        input_output_aliases={2 + i: i for i in range(n_chain)},
        compiler_params=_params(1),
        name="mem_kv",
    )(mem2d, w_kv, *chain_args)
    return (k_all, v_all), (kb, vb)


def _mlstm_prompt_kernel(x_ref, wqkvo_ref, bqkvo_ref, wif_ref, bif_ref, wg0_ref, bg0_ref,
                         mhg_ref, wmo_ref, y_ref, c_ref, n_ref, m_ref):
    t = pl.program_id(1)
    d = N_HEADS * HEAD_DIM

    @pl.when(t == 0)
    def _():
        c_ref[...] = jnp.zeros(c_ref.shape, F32)
        n_ref[...] = jnp.zeros(n_ref.shape, F32)
        m_ref[...] = jnp.zeros(m_ref.shape, F32)

    xb = x_ref[...].astype(BF16)
    lc = xb.shape[0]
    gi = _dot(xb, wif_ref[:, 0:LANES]) + bif_ref[:, 0:LANES]
    gf = _dot(xb, wif_ref[:, LANES:2 * LANES]) + bif_ref[:, LANES:2 * LANES]
    lf = _log_sigmoid(gf)
    bcum = _scan_rows(lf, jnp.add)
    a = gi - bcum
    m0 = m_ref[0]
    mt = bcum + jnp.maximum(m0, _scan_rows(a, jnp.maximum))
    inter = jnp.exp(m0 + bcum - mt)
    bm = bcum - mt
    emt = jnp.exp(-mt)
    b_last = bcum[lc - 1:lc, :]
    m_last = mt[lc - 1:lc, :]
    ws = jnp.exp(a + b_last - m_last)
    decay = jnp.exp(m0 + b_last - m_last)
    a_t = a.T
    causal = (lax.broadcasted_iota(jnp.int32, (lc, lc), 1)
              <= lax.broadcasted_iota(jnp.int32, (lc, lc), 0))

    def head_proj(h):
        def proj(part):
            cols = slice(part * d + h * HEAD_DIM, part * d + (h + 1) * HEAD_DIM)
            return _dot(xb, wqkvo_ref[:, cols]) + bqkvo_ref[:, cols]
        return proj(0), proj(1) * (HEAD_DIM ** -0.5), proj(2), proj(3)

    hn_heads = []
    ahead = head_proj(0)
    for h in range(N_HEADS):
        q, k, v, o = ahead
        if h + 1 < N_HEADS:
            ahead = head_proj(h + 1)
        else:
            g0 = _dot(xb, wg0_ref[...]) + bg0_ref[...]
        qb = q.astype(BF16)
        vb = v.astype(BF16)
        col = slice(h, h + 1)
        logd = a_t[h:h + 1, :] + bm[:, col]
        dmat = jnp.exp(jnp.where(causal, logd, NEG))
        s = _dot_nt(qb, k.astype(BF16)) * dmat
        c_old = c_ref[0, h]
        n_old = n_ref[0, h:h + 1, :]
        inter_h = inter[:, col]
        num = _dot(s.astype(BF16), vb) + inter_h * _dot(qb, c_old.astype(BF16))
        den = (jnp.sum(s, axis=1, keepdims=True)
               + inter_h * jnp.sum(q * n_old, axis=1, keepdims=True))
        ht = num / jnp.maximum(jnp.abs(den), emt[:, col])
        kw = k * ws[:, col]
        c_ref[0, h] = decay[:, col] * c_old + _dot_tn(kw.astype(BF16), vb)
        n_ref[0, h:h + 1, :] = decay[:, col] * n_old + jnp.sum(kw, axis=0, keepdims=True)
        hn = _head_norm(jax.nn.sigmoid(o) * ht) * mhg_ref[:, _hs(h)]
        hn_heads.append(hn.astype(BF16))
    m_ref[0] = m_last

    ym = _dot(jnp.concatenate(hn_heads, axis=1), wmo_ref[...])
    y_ref[...] = jax.nn.sigmoid(g0) * ym


def _mlstm_prompt(x2d, batch, w_qkvo, b_qkvo, w_if, b_if, w_rest, b_rest, mh_g, w_m_out, layer):
    rows, d = x2d.shape
    nt = rows // batch // TIME_TILE
    wide = N_HEADS * d

    def lw(shape, col):
        return pl.BlockSpec((None,) + shape, lambda b, t: (layer, 0, col))

    return pl.pallas_call(
        _mlstm_prompt_kernel,
        grid=(batch, nt),
        in_specs=[pl.BlockSpec((TIME_TILE, d), lambda b, t: (b * nt + t, 0)),
                  lw((d, wide), 0), lw((1, wide), 0),
                  lw((d, 2 * LANES), 0), lw((1, 2 * LANES), 0),
                  lw((d, d), REST_G0), lw((1, d), REST_G0),
                  lw((1, d), 0), lw((d, d), 0)],
        out_specs=[pl.BlockSpec((TIME_TILE, d), lambda b, t: (b * nt + t, 0)),
                   pl.BlockSpec((1, N_HEADS, HEAD_DIM, HEAD_DIM), lambda b, t: (b, 0, 0, 0)),
                   pl.BlockSpec((1, N_HEADS, HEAD_DIM), lambda b, t: (b, 0, 0)),
                   pl.BlockSpec((1, 1, LANES), lambda b, t: (b, 0, 0))],
        out_shape=[jax.ShapeDtypeStruct((rows, d), F32),
                   jax.ShapeDtypeStruct((batch, N_HEADS, HEAD_DIM, HEAD_DIM), F32),
                   jax.ShapeDtypeStruct((batch, N_HEADS, HEAD_DIM), F32),
                   jax.ShapeDtypeStruct((batch, 1, LANES), F32)],
        compiler_params=_params(2),
        name="mlstm_prompt",
    )(x2d, w_qkvo, b_qkvo, w_if, b_if, w_rest, b_rest, mh_g, w_m_out)


def _pool_window_sums(ext, window):
    s, span = ext, 1
    while span < window:
        s = s + pltpu.roll(s, span, 0)
        span *= 2
    return s


def _xattn_head(qb, kb, vb):
    s = _dot_nt(qb, kb) * (HEAD_DIM ** -0.5)
    s = s - jnp.max(s, axis=-1, keepdims=True)
    p = jnp.exp(s)
    p = p / jnp.sum(p, axis=-1, keepdims=True)
    return _dot(p.astype(BF16), vb)


def _mix_prompt_kernel(x_ref, ymg_ref, mk_ref, mv_ref, wu_ref, bu_ref, wxq_ref, bxq_ref,
                       wg1_ref, bg1_ref, wg2_ref, bg2_ref, wpool_ref, pscale_ref, wxo_ref,
                       wo_ref, lng_ref, lnb_ref, x1_ref, pool_ref, ext_scr):
    t = pl.program_id(1)
    tt = x_ref.shape[0]
    carry = 2 * SUBLANES

    @pl.when(t == 0)
    def _():
        ext_scr[0:carry, :] = jnp.zeros((carry, ext_scr.shape[1]), F32)

    x = x_ref[...]
    xb = x.astype(BF16)

    u = _dot(xb, wu_ref[...]) + bu_ref[...]
    xq = (_dot(xb, wxq_ref[...]) + bxq_ref[...]).astype(BF16)

    ext_scr[carry:carry + tt, :] = u
    pos = t * tt + lax.broadcasted_iota(jnp.int32, (tt, 1), 0)
    scores = [_dot_nt(xq[:, _hs(h)], mk_ref[:, _hs(h)]) for h in range(N_HEADS)]
    yp_groups = []
    for g, window in enumerate(POOL_WINDOWS):
        cols = slice(g * POOL_GROUP_DIM, (g + 1) * POOL_GROUP_DIM)
        wsum = _pool_window_sums(ext_scr[:, cols], window)[carry:, :]
        cnt = jnp.minimum(window, pos + 1).astype(F32)
        dlt = wsum / cnt - u[:, cols]
        yp_groups.append(_dot(dlt.astype(BF16), wpool_ref[g]))
    ext_scr[0:carry, :] = u[tt - carry:tt, :]
    pool_ref[0] = u[tt - carry:tt, :]
    g1 = _dot(xb, wg1_ref[...]) + bg1_ref[...]

    heads = []
    for h in range(N_HEADS):
        s = scores[h] * (HEAD_DIM ** -0.5)
        p = jnp.exp(s - jnp.max(s, axis=-1, keepdims=True))
        p = p / jnp.sum(p, axis=-1, keepdims=True)
        heads.append(_dot(p.astype(BF16), mv_ref[:, _hs(h)]).astype(BF16))
    g2 = _dot(xb, wg2_ref[...]) + bg2_ref[...]
    yp = jnp.concatenate(yp_groups, axis=1) * pscale_ref[...]
    yx = _dot(jnp.concatenate(heads, axis=1), wxo_ref[...])
    mix = ymg_ref[...] + jax.nn.sigmoid(g1) * yp + jax.nn.sigmoid(g2) * yx
    y = ALPHA * x + _dot(mix.astype(BF16), wo_ref[...])
    x1_ref[...] = _layer_norm(y, lng_ref[...], lnb_ref[...])


def _mix_prompt(x2d, ymg, mk, mv, batch, w_rest, b_rest, w_pool, pool_scale, w_x_out, w_o,
                ln_g, ln_b, layer):
    rows, d = x2d.shape
    nt = rows // batch // TIME_TILE
    n_mem = mk.shape[0] // batch
    carry = 2 * SUBLANES

    def lw(shape, col):
        return pl.BlockSpec((None,) + shape, lambda b, t: (layer, 0, col))

    row_spec = pl.BlockSpec((TIME_TILE, d), lambda b, t: (b * nt + t, 0))
    mem_spec = pl.BlockSpec((n_mem, d), lambda b, t: (b, 0))
    return pl.pallas_call(
        _mix_prompt_kernel,
        grid=(batch, nt),
        in_specs=[row_spec, row_spec, mem_spec, mem_spec,
                  lw((d, d), REST_U), lw((1, d), REST_U),
                  lw((d, d), REST_XQ), lw((1, d), REST_XQ),
                  lw((d, d), REST_G1), lw((1, d), REST_G1),
                  lw((d, d), REST_G2), lw((1, d), REST_G2),
                  pl.BlockSpec((None, len(POOL_WINDOWS), POOL_GROUP_DIM, POOL_GROUP_DIM),
                               lambda b, t: (layer, 0, 0, 0)),
                  lw((1, d), 0), lw((d, d), 0), lw((d, d), 0), lw((1, d), 0), lw((1, d), 0)],
        out_specs=[row_spec, pl.BlockSpec((1, carry, d), lambda b, t: (b, 0, 0))],
        out_shape=[jax.ShapeDtypeStruct((rows, d), F32),
                   jax.ShapeDtypeStruct((batch, carry, d), F32)],
        scratch_shapes=[pltpu.VMEM((carry + TIME_TILE, d), F32)],
        compiler_params=_params(2),
        name="mix_prompt",
    )(x2d, ymg, mk, mv, w_rest, b_rest, w_rest, b_rest, w_rest, b_rest, w_rest, b_rest,
      w_pool, pool_scale, w_x_out, w_o, ln_g, ln_b)


def _ffn_chunk(xb, j, d_ff, wup_ref, cw_ref, cb_ref, wdn_ref, conv_taps):
    halves = _ffn_up(xb, j, d_ff, wup_ref)
    return _ffn_gate_down(halves, j, d_ff, cw_ref, cb_ref, wdn_ref, conv_taps), halves


def _ffn_up(xb, j, d_ff, wup_ref):
    return [_dot(xb, wup_ref[:, base + j * FF_CHUNK:base + (j + 1) * FF_CHUNK])
            for base in (0, d_ff)]


def _ffn_gate_down(halves, j, d_ff, cw_ref, cb_ref, wdn_ref, conv_taps):
    conv = []
    for base, hup in zip((0, d_ff), halves):
        cols = slice(base + j * FF_CHUNK, base + (j + 1) * FF_CHUNK)
        prev2, prev1 = conv_taps(hup, cols)
        conv.append(prev2 * cw_ref[0:1, cols] + prev1 * cw_ref[1:2, cols]
                    + hup * cw_ref[2:3, cols] + cb_ref[:, cols])
    act = jax.nn.gelu(conv[0]) * conv[1]
    return _dot(act.astype(BF16), wdn_ref[j * FF_CHUNK:(j + 1) * FF_CHUNK, :])


def _ffn_prompt_kernel(x_ref, wup_ref, cw_ref, cb_ref, wdn_ref, lng_ref, lnb_ref,
                       o_ref, conv_ref, ext_scr):
    t = pl.program_id(1)
    tt = x_ref.shape[0]
    d_ff = wdn_ref.shape[0]
    n_chunks = d_ff // FF_CHUNK

    @pl.when(t == 0)
    def _():
        ext_scr[0:SUBLANES, :] = jnp.zeros((SUBLANES, ext_scr.shape[1]), F32)

    items = [(r0, j) for r0 in range(0, tt, FF_SUB_ROWS) for j in range(n_chunks)]
    xs = {r0: x_ref[r0:r0 + FF_SUB_ROWS, :] for r0 in range(0, tt, FF_SUB_ROWS)}
    xbs = {r0: xs[r0].astype(BF16) for r0 in xs}

    def up(item):
        r0, j = item
        return _ffn_up(xbs[r0], j, d_ff, wup_ref)

    ups = [up(item) for item in items[:FF_LOOKAHEAD]]
    acc = None
    for i, (r0, j) in enumerate(items):
        if i + FF_LOOKAHEAD < len(items):
            ups.append(up(items[i + FF_LOOKAHEAD]))
        halves = ups[i]
        ups[i] = None
        lo = SUBLANES + r0

        def conv_taps(hup, cols):
            ext_scr[lo:lo + FF_SUB_ROWS, cols] = hup
            return (ext_scr[lo - 2:lo - 2 + FF_SUB_ROWS, cols],
                    ext_scr[lo - 1:lo - 1 + FF_SUB_ROWS, cols])

        part = _ffn_gate_down(halves, j, d_ff, cw_ref, cb_ref, wdn_ref, conv_taps)
        acc = part if j == 0 else part + acc
        if r0 + FF_SUB_ROWS == tt:
            for base, hup in zip((0, d_ff), halves):
                cols = slice(base + j * FF_CHUNK, base + (j + 1) * FF_CHUNK)
                ext_scr[0:SUBLANES, cols] = hup[FF_SUB_ROWS - SUBLANES:, :]
                conv_ref[0, :, cols] = hup[FF_SUB_ROWS - SUBLANES:, :]
        if j == n_chunks - 1:
            o_ref[r0:r0 + FF_SUB_ROWS, :] = _layer_norm(ALPHA * xs[r0] + acc, lng_ref[...],
                                                       lnb_ref[...])


def _ffn_prompt(x2d, batch, w_up, conv_w, conv_b, w_down, ln_g, ln_b, layer):
    rows, d = x2d.shape
    nt = rows // batch // FF_TILE
    d_ff = w_down.shape[1]

    def lw(shape):
        return pl.BlockSpec((None,) + shape, lambda b, t: (layer, 0, 0),
                            pipeline_mode=pl.Buffered(1))

    row_spec = pl.BlockSpec((FF_TILE, d), lambda b, t: (b * nt + t, 0))
    return pl.pallas_call(
        _ffn_prompt_kernel,
        grid=(batch, nt),
        in_specs=[row_spec, lw((d, 2 * d_ff)), lw((CONV_WIDTH, 2 * d_ff)), lw((1, 2 * d_ff)),
                  lw((d_ff, d)), lw((1, d)), lw((1, d))],
        out_specs=[row_spec, pl.BlockSpec((1, SUBLANES, 2 * d_ff), lambda b, t: (b, 0, 0))],
        out_shape=[jax.ShapeDtypeStruct((rows, d), F32),
                   jax.ShapeDtypeStruct((batch, SUBLANES, 2 * d_ff), F32)],
        scratch_shapes=[pltpu.VMEM((SUBLANES + FF_TILE, 2 * d_ff), F32)],
        compiler_params=_params(2),
        name="ffn_prompt",
    )(x2d, w_up, conv_w, conv_b, w_down, ln_g, ln_b)


def _ffn_sample_kernel(x_ref, st_ref, wup_ref, cw_ref, cb_ref, wdn_ref, lng_ref, lnb_ref,
                       o_ref, stn_ref):
    d_ff = wdn_ref.shape[0]
    x = x_ref[...]
    xb = x.astype(BF16)

    def conv_taps(hup, cols):
        return st_ref[:, 0, cols], st_ref[:, 1, cols]

    acc = None
    for j in range(d_ff // FF_CHUNK):
        part, halves = _ffn_chunk(xb, j, d_ff, wup_ref, cw_ref, cb_ref, wdn_ref, conv_taps)
        for base, hup in zip((0, d_ff), halves):
            cols = slice(base + j * FF_CHUNK, base + (j + 1) * FF_CHUNK)
            stn_ref[:, 0, cols] = st_ref[:, 1, cols]
            stn_ref[:, 1, cols] = hup
        acc = part if acc is None else acc + part
    o_ref[...] = _layer_norm(ALPHA * x + acc, lng_ref[...], lnb_ref[...])


def _ffn_sample(x, conv_state, w_up, conv_w, conv_b, w_down, ln_g, ln_b, layer, tile):
    rows, d = x.shape
    d_ff = w_down.shape[1]
    st_shape = conv_state.shape[2:]

    def lw(shape):
        return pl.BlockSpec((None,) + shape, lambda i: (layer, 0, 0))

    return pl.pallas_call(
        _ffn_sample_kernel,
        grid=(rows // tile,),
        in_specs=[pl.BlockSpec((tile, d), lambda i: (i, 0)),
                  pl.BlockSpec((None, tile) + st_shape, lambda i: (layer, i, 0, 0)),
                  lw((d, 2 * d_ff)), lw((CONV_WIDTH, 2 * d_ff)), lw((1, 2 * d_ff)),
                  lw((d_ff, d)), lw((1, d)), lw((1, d))],
        out_specs=[pl.BlockSpec((tile, d), lambda i: (i, 0)),
                   pl.BlockSpec((tile,) + st_shape, lambda i: (i, 0, 0))],
        out_shape=[jax.ShapeDtypeStruct((rows, d), F32),
                   jax.ShapeDtypeStruct((rows,) + st_shape, F32)],
        compiler_params=_params(1),
        name="ffn_sample",
    )(x, conv_state, w_up, conv_w, conv_b, w_down, ln_g, ln_b)


def _mlstm_step_kernel(q_ref, k_ref, v_ref, o_ref, g_ref, c_ref, n_ref, m_ref, mhg_ref, *refs):
    hn_ref, cn_ref, nn_ref, mn_ref, kw_scr, dec_scr, r_scr = refs[-7:]
    bb = q_ref.shape[0]
    mxu_rows = 2 * SUBLANES
    gi = g_ref[:, 0:N_HEADS]
    lf = _log_sigmoid(g_ref[:, LANES:LANES + N_HEADS])
    m0 = m_ref[...]
    mt = jnp.maximum(m0 + lf, gi)
    inter_all = jnp.exp(m0 + lf - mt)
    dm_all = jnp.exp(gi - mt)
    emt_all = jnp.exp(-mt)
    mn_ref[...] = mt

    stats = []
    for h in range(N_HEADS):
        col = slice(h, h + 1)
        q = q_ref[:, _hs(h)]
        k = k_ref[:, _hs(h)] * (HEAD_DIM ** -0.5)
        n_old = n_ref[:, h, :]
        inter, dm = inter_all[:, col], dm_all[:, col]
        s = jnp.sum(q * k, axis=1, keepdims=True) * dm
        den = s + inter * jnp.sum(q * n_old, axis=1, keepdims=True)
        kw = k * dm
        kw_scr[:, _hs(h)] = kw
        dec_scr[:, _hs(h)] = jnp.broadcast_to(inter, kw.shape)
        nn_ref[:, h, :] = inter * n_old + kw
        stats.append((s, inter, den, emt_all[:, col]))

    row0 = lax.broadcasted_iota(jnp.int32, (mxu_rows, HEAD_DIM), 0) == 0

    def per_sequence(b, carry):
        row = pl.ds(b, 1)
        for h in range(N_HEADS):
            qrow = jnp.broadcast_to(q_ref[row, _hs(h)], (mxu_rows, HEAD_DIM)).astype(BF16)
            c_old = c_ref[b, h]
            r_scr[row, _hs(h)] = _dot(qrow, c_old.astype(BF16))[0:1, :]
            kw = jnp.where(row0, jnp.broadcast_to(kw_scr[row, _hs(h)], (mxu_rows, HEAD_DIM)), 0.0)
            vrow = jnp.broadcast_to(v_ref[row, _hs(h)], (mxu_rows, HEAD_DIM))
            cn_ref[b, h] = (dec_scr[row, _hs(h)] * c_old
                            + _dot_tn(kw.astype(BF16), vrow.astype(BF16)))
        return carry

    lax.fori_loop(0, bb, per_sequence, 0)

    for h in range(N_HEADS):
        s, inter, den, emt = stats[h]
        ht = (s * v_ref[:, _hs(h)] + inter * r_scr[:, _hs(h)]) / jnp.maximum(jnp.abs(den), emt)
        hc = jax.nn.sigmoid(o_ref[:, _hs(h)]) * ht
        hn_ref[:, _hs(h)] = _head_norm(hc) * mhg_ref[:, _hs(h)]


def _mlstm_step(proj, gates, state_c, state_n, state_m, mh_g, prev_c, layer):
    nseq = proj.shape[0]
    d = N_HEADS * HEAD_DIM
    bb = SAMPLE_BLOCK
    chain_specs, chain_args, n_chain = _chain_inputs((prev_c,))

    def pcol(c):
        return pl.BlockSpec((bb, d), lambda i: (i, c))

    cspec = pl.BlockSpec((None, bb, N_HEADS, HEAD_DIM, HEAD_DIM), lambda i: (layer, i, 0, 0, 0))
    nspec_in = pl.BlockSpec((None, bb, N_HEADS, HEAD_DIM), lambda i: (layer, i, 0, 0))
    return pl.pallas_call(
        _mlstm_step_kernel,
        grid=(nseq // bb,),
        in_specs=[pcol(0), pcol(1), pcol(2), pcol(3),
                  pl.BlockSpec((bb, 2 * LANES), lambda i: (i, 0)),
                  cspec, nspec_in,
                  pl.BlockSpec((None, bb, N_HEADS), lambda i: (layer, i, 0)),
                  pl.BlockSpec((None, 1, d), lambda i: (layer, 0, 0))] + chain_specs,
        out_specs=[pl.BlockSpec((bb, d), lambda i: (i, 0)),
                   cspec,
                   pl.BlockSpec((bb, N_HEADS, HEAD_DIM), lambda i: (i, 0, 0)),
                   pl.BlockSpec((bb, N_HEADS), lambda i: (i, 0))],
        out_shape=[jax.ShapeDtypeStruct((nseq, d), F32),
                   jax.ShapeDtypeStruct(state_c.shape, F32),
                   jax.ShapeDtypeStruct((nseq, N_HEADS, HEAD_DIM), F32),
                   jax.ShapeDtypeStruct((nseq, N_HEADS), F32)],
        input_output_aliases={9 + i: 1 for i in range(n_chain)},
        scratch_shapes=[pltpu.VMEM((bb, d), F32)] * 3,
        compiler_params=_params(1),
        name="mlstm_step",
    )(proj, proj, proj, proj, gates, state_c, state_n, state_m, mh_g, *chain_args)


def _xattn_step_kernel(xq_ref, k_ref, v_ref, o_ref, s_scr):
    bb = xq_ref.shape[0]
    n_mem = k_ref.shape[1]
    rows = SUBLANES
    sub = lax.broadcasted_iota(jnp.int32, (rows, HEAD_DIM), 0)

    def scores(b, carry):
        q = jnp.zeros((rows, HEAD_DIM), F32)
        for h in range(N_HEADS):
            qrow = jnp.broadcast_to(xq_ref[pl.ds(b, 1), _hs(h)], (rows, HEAD_DIM))
            q = jnp.where(sub == h, qrow, q)
        keys = k_ref[b].reshape(n_mem * N_HEADS, HEAD_DIM).astype(BF16)
        s_scr[pl.ds(pl.multiple_of(b * rows, rows), rows), :] = _dot_nt(q.astype(BF16), keys)
        return carry

    lax.fori_loop(0, bb, scores, 0)

    s = s_scr[...] * (HEAD_DIM ** -0.5)
    head = lax.broadcasted_iota(jnp.int32, s.shape, 0) % rows
    col_head = lax.broadcasted_iota(jnp.int32, s.shape, 1) % N_HEADS
    s = jnp.where((col_head == head) | (head >= N_HEADS), s, -jnp.inf)
    p = jnp.exp(s - jnp.max(s, axis=-1, keepdims=True))
    s_scr[...] = p / jnp.sum(p, axis=-1, keepdims=True)

    def outputs(b, carry):
        p = s_scr[pl.ds(pl.multiple_of(b * rows, rows), rows), :]
        vals = v_ref[b].reshape(n_mem * N_HEADS, HEAD_DIM).astype(BF16)
        out = _dot(p.astype(BF16), vals)
        for h in range(N_HEADS):
            o_ref[pl.ds(b, 1), _hs(h)] = out[h:h + 1, :]
        return carry

    lax.fori_loop(0, bb, outputs, 0)


def _xattn_step(proj_rest, mem_k, mem_v, layer):
    nseq = proj_rest.shape[0]
    d = N_HEADS * HEAD_DIM
    n_mem = mem_k.shape[2]
    bb = SAMPLE_BLOCK
    kv_spec = pl.BlockSpec((None, bb, n_mem, N_HEADS, HEAD_DIM), lambda i: (layer, i, 0, 0, 0))
    return pl.pallas_call(
        _xattn_step_kernel,
        grid=(nseq // bb,),
        in_specs=[pl.BlockSpec((bb, d), lambda i: (i, REST_XQ)), kv_spec, kv_spec],
        out_specs=pl.BlockSpec((bb, d), lambda i: (i, 0)),
        out_shape=jax.ShapeDtypeStruct((nseq, d), F32),
        scratch_shapes=[pltpu.VMEM((SUBLANES * bb, N_HEADS * n_mem), F32)],
        compiler_params=_params(1),
        name="xattn_step",
    )(proj_rest, mem_k, mem_v)


def _merge_step_kernel(x_ref, hn_ref, ox_ref, u_ref, g0_ref, g1_ref, g2_ref, pool_ref,
                       wmo_ref, wpool_ref, pscale_ref, wxo_ref, wo_ref, lng_ref, lnb_ref,
                       x1_ref, pooln_ref):
    x = x_ref[...]
    u = u_ref[...]
    for r in range(1, POOL_BUF):
        pooln_ref[:, r - 1, :] = pool_ref[:, r, :]
    pooln_ref[:, POOL_BUF - 1, :] = u
    yp_groups = []
    for g, window in enumerate(POOL_WINDOWS):
        cols = slice(g * POOL_GROUP_DIM, (g + 1) * POOL_GROUP_DIM)
        wsum = u[:, cols]
        for back in range(1, window):
            wsum = wsum + pool_ref[:, POOL_BUF - back, cols]
        cnt = float(min(window, PAST_LEN + 1))
        dlt = wsum / cnt - u[:, cols]
        yp_groups.append(_dot(dlt.astype(BF16), wpool_ref[g]))
    yp = jnp.concatenate(yp_groups, axis=1) * pscale_ref[...]
    ym = _dot(hn_ref[...].astype(BF16), wmo_ref[...])
    yx = _dot(ox_ref[...].astype(BF16), wxo_ref[...])
    mix = (jax.nn.sigmoid(g0_ref[...]) * ym + jax.nn.sigmoid(g1_ref[...]) * yp
           + jax.nn.sigmoid(g2_ref[...]) * yx)
    y = ALPHA * x + _dot(mix.astype(BF16), wo_ref[...])
    x1_ref[...] = _layer_norm(y, lng_ref[...], lnb_ref[...])


def _merge_step(x, hn, ox, proj_rest, pool_state, w_m_out, w_pool, pool_scale, w_x_out, w_o,
                ln_g, ln_b, layer, tile):
    rows, d = x.shape
    st_shape = pool_state.shape[2:]

    def lw(shape):
        return pl.BlockSpec((None,) + shape, lambda i: (layer,) + (0,) * len(shape))

    def rows_at(col):
        return pl.BlockSpec((tile, d), lambda i: (i, col))

    return pl.pallas_call(
        _merge_step_kernel,
        grid=(rows // tile,),
        in_specs=[rows_at(0), rows_at(0), rows_at(0), rows_at(REST_U), rows_at(REST_G0),
                  rows_at(REST_G1), rows_at(REST_G2),
                  pl.BlockSpec((None, tile) + st_shape, lambda i: (layer, i, 0, 0)),
                  lw((d, d)), lw((len(POOL_WINDOWS), POOL_GROUP_DIM, POOL_GROUP_DIM)),
                  lw((1, d)), lw((d, d)), lw((d, d)), lw((1, d)), lw((1, d))],
        out_specs=[rows_at(0), pl.BlockSpec((tile,) + st_shape, lambda i: (i, 0, 0))],
        out_shape=[jax.ShapeDtypeStruct((rows, d), F32),
                   jax.ShapeDtypeStruct((rows,) + st_shape, F32)],
        compiler_params=_params(1),
        name="merge_step",
    )(x, hn, ox, proj_rest, proj_rest, proj_rest, proj_rest, pool_state, w_m_out, w_pool,
      pool_scale, w_x_out, w_o, ln_g, ln_b)


def kernel(x_prompt, mem_prompt, x_sample, cache_mem_k, cache_mem_v, state_C, state_n, state_m,
           state_pool, state_conv, ln_in_g, ln_in_b, w_in, b_in, mh_g, w_m_out, w_pool,
           pool_scale, w_mem_kv, w_x_out, w_o, ln1_g, ln1_b, w_up, conv_w, conv_b, w_down,
           ln2_g, ln2_b):
    batch, seq, d = x_prompt.shape
    nseq = x_sample.shape[0]
    n_mem = mem_prompt.shape[1]
    depth = w_in.shape[0]
    d_ff = w_down.shape[1]
    wide = N_HEADS * d
    gate_off = wide
    rest_off = wide + 2 * N_HEADS
    assert d == N_HEADS * HEAD_DIM and seq % TIME_TILE == 0 and d_ff % FF_CHUNK == 0
    assert depth == DEPTH and nseq % SAMPLE_BLOCK == 0 and x_sample.shape[1] == 1

    def row(v):
        return v.reshape(depth, 1, v.shape[-1])

    w_qkvo, w_rest = w_in[:, :, :wide].astype(BF16), w_in[:, :, rest_off:].astype(BF16)
    b_qkvo, b_rest = row(b_in[:, :wide]), row(b_in[:, rest_off:])
    lane_pad = ((0, 0), (0, 0), (0, LANES - N_HEADS))
    w_if = jnp.concatenate(
        [jnp.pad(w_in[:, :, gate_off:gate_off + N_HEADS], lane_pad),
         jnp.pad(w_in[:, :, gate_off + N_HEADS:rest_off], lane_pad)], axis=2).astype(BF16)
    b_if = row(jnp.concatenate(
        [jnp.pad(b_in[:, gate_off:gate_off + N_HEADS], lane_pad[1:]),
         jnp.pad(b_in[:, gate_off + N_HEADS:rest_off], lane_pad[1:])], axis=1))
    w_m_out_b, w_pool_b, w_x_out_b, w_o_b = (w.astype(BF16) for w in (w_m_out, w_pool, w_x_out, w_o))
    w_up_b, w_down_b = w_up.astype(BF16), w_down.astype(BF16)
    w_kv_b = w_mem_kv.astype(BF16)
    mh_g_r, pool_scale_r, conv_b_r = row(mh_g), row(pool_scale), row(conv_b)
    ln1_g_r, ln1_b_r, ln2_g_r, ln2_b_r = row(ln1_g), row(ln1_b), row(ln2_g), row(ln2_b)

    xp = x_prompt.reshape(batch * seq, d)
    xs = x_sample.reshape(nseq, d)
    mem2d = mem_prompt.reshape(batch * n_mem, d)

    xp = _layer_norm_rows(xp, ln_in_g.reshape(1, d), ln_in_b.reshape(1, d), 2 * TIME_TILE)
    xs = _layer_norm_rows(xs, ln_in_g.reshape(1, d), ln_in_b.reshape(1, d), nseq)

    outs = {name: [] for name in ("pC", "pn", "pm", "ppool", "pconv", "sn", "sm", "spool", "sconv")}
    p_mem = (jnp.zeros((depth, batch, n_mem, N_HEADS, HEAD_DIM), F32),) * 2
    s_c = jnp.zeros(state_C.shape, F32)
    for l in range(depth):
        p_mem, (mk_b, mv_b) = _mem_kv(mem2d, w_kv_b, p_mem, l, batch)
        ymg, p_c, p_n, p_m = _mlstm_prompt(xp, batch, w_qkvo, b_qkvo, w_if, b_if, w_rest, b_rest,
                                           mh_g_r, w_m_out_b, l)
        x1, p_pool = _mix_prompt(xp, ymg, mk_b, mv_b, batch, w_rest, b_rest, w_pool_b,
                                 pool_scale_r, w_x_out_b, w_o_b, ln1_g_r, ln1_b_r, l)
        xp, p_conv = _ffn_prompt(x1, batch, w_up_b, conv_w, conv_b_r, w_down_b, ln2_g_r,
                                 ln2_b_r, l)
        outs["pC"].append(p_c)
        outs["pn"].append(p_n)
        outs["pm"].append(p_m[:, 0, :N_HEADS])
        outs["ppool"].append(p_pool[:, 2 * SUBLANES - POOL_BUF:, :])
        outs["pconv"].append(p_conv[:, SUBLANES - (CONV_WIDTH - 1):, :])

        proj_qkvo = _matmul_bias(xs, w_qkvo, b_qkvo, l, nseq, d, "proj_sample")
        proj_rest = _matmul_bias(xs, w_rest, b_rest, l, nseq, d, "proj_sample")
        gates = _matmul_bias(xs, w_if, b_if, l, nseq, 2 * LANES, "gates_sample")
        hn, s_c, s_n, s_m = _mlstm_step(proj_qkvo, gates, state_C, state_n, state_m, mh_g_r,
                                        s_c, l)
        ox = _xattn_step(proj_rest, cache_mem_k, cache_mem_v, l)
        x1s, s_pool = _merge_step(xs, hn, ox, proj_rest, state_pool, w_m_out_b, w_pool_b,
                                  pool_scale_r, w_x_out_b, w_o_b, ln1_g_r, ln1_b_r, l, nseq // 2)
        xs, s_conv = _ffn_sample(x1s, state_conv, w_up_b, conv_w, conv_b_r, w_down_b, ln2_g_r,
                                 ln2_b_r, l, nseq // 2)
        outs["sn"].append(s_n)
        outs["sm"].append(s_m)
        outs["spool"].append(s_pool)
        outs["sconv"].append(s_conv)

    stacked = {k: jnp.stack(v) for k, v in outs.items()}
    return (xp.reshape(batch, seq, d), xs.reshape(nseq, 1, d),
            stacked["pC"], stacked["pn"], stacked["pm"], stacked["ppool"], stacked["pconv"],
            p_mem[0], p_mem[1],
            s_c, stacked["sn"], stacked["sm"], stacked["spool"], stacked["sconv"])
```

```python
import jax
import jax.numpy as jnp
from jax import lax
from jax.experimental import pallas as pl
from jax.experimental.pallas import tpu as pltpu

F32 = jnp.float32
BF16 = jnp.bfloat16

N_HEADS = 4
HEAD_DIM = 256
POOL_WINDOWS = (2, 4, 8, 16)
POOL_GROUP_DIM = 256
POOL_BUF = 15
CONV_WIDTH = 3
DEPTH = 4
PAST_LEN = 16384
ALPHA = (2.0 * DEPTH) ** 0.25
LN_EPS = 1e-5
NEG = -1e30

SUBLANES = 8
LANES = 128
TIME_TILE = 256
FF_CHUNK = 256
FF_TILE = 512
FF_SUB_ROWS = 256
FF_LOOKAHEAD = 3
VMEM_LIMIT = 56 * 1024 * 1024

REST_U, REST_XQ, REST_G0, REST_G1, REST_G2 = range(5)


def _params(n_axes):
    return pltpu.CompilerParams(
        dimension_semantics=("arbitrary",) * n_axes, vmem_limit_bytes=VMEM_LIMIT)


def _dot(a, b):
    return jnp.dot(a, b, preferred_element_type=F32)


def _dot_nt(a, b):
    return lax.dot_general(a, b, (((1,), (1,)), ((), ())), preferred_element_type=F32)


def _dot_tn(a, b):
    return lax.dot_general(a, b, (((0,), (0,)), ((), ())), preferred_element_type=F32)


def _layer_norm(x, g, b):
    mu = jnp.mean(x, axis=-1, keepdims=True)
    xc = x - mu
    var = jnp.mean(xc * xc, axis=-1, keepdims=True)
    return xc * lax.rsqrt(var + LN_EPS) * g + b


def _head_norm(x):
    mu = jnp.mean(x, axis=-1, keepdims=True)
    xc = x - mu
    var = jnp.mean(xc * xc, axis=-1, keepdims=True)
    return xc * lax.rsqrt(var + LN_EPS)


def _log_sigmoid(x):
    return jnp.minimum(x, 0.0) - jnp.log1p(jnp.exp(-jnp.abs(x)))


def _scan_rows(x, op):
    n = x.shape[0]
    rows = lax.broadcasted_iota(jnp.int32, x.shape, 0)
    shift = 1
    while shift < n:
        x = jnp.where(rows >= shift, op(x, pltpu.roll(x, shift, 0)), x)
        shift *= 2
    return x


def _hs(h):
    return slice(h * HEAD_DIM, (h + 1) * HEAD_DIM)


def _ln_kernel(x_ref, g_ref, b_ref, o_ref):
    o_ref[...] = _layer_norm(x_ref[...], g_ref[...], b_ref[...])


def _layer_norm_rows(x, g, b, tile):
    rows, d = x.shape
    return pl.pallas_call(
        _ln_kernel,
        grid=(rows // tile,),
        in_specs=[pl.BlockSpec((tile, d), lambda i: (i, 0)),
                  pl.BlockSpec((1, d), lambda i: (0, 0)),
                  pl.BlockSpec((1, d), lambda i: (0, 0))],
        out_specs=pl.BlockSpec((tile, d), lambda i: (i, 0)),
        out_shape=jax.ShapeDtypeStruct((rows, d), F32),
        compiler_params=_params(1),
        name="ln_rows",
    )(x, g, b)


def _mm_kernel(x_ref, w_ref, b_ref, o_ref):
    o_ref[...] = _dot(x_ref[...].astype(BF16), w_ref[...]) + b_ref[...]


def _matmul_bias(x, w_all, b_all, layer, tm, tn, name):
    rows, kdim = x.shape
    n = w_all.shape[-1]
    return pl.pallas_call(
        _mm_kernel,
        grid=(n // tn, rows // tm),
        in_specs=[pl.BlockSpec((tm, kdim), lambda j, i: (i, 0)),
                  pl.BlockSpec((None, kdim, tn), lambda j, i: (layer, 0, j)),
                  pl.BlockSpec((None, 1, tn), lambda j, i: (layer, 0, j))],
        out_specs=pl.BlockSpec((tm, tn), lambda j, i: (i, j)),
        out_shape=jax.ShapeDtypeStruct((rows, n), F32),
        compiler_params=_params(2),
        name=name,
    )(x, w_all, b_all)


def _chain_inputs(prev):
    specs = [pl.BlockSpec(memory_space=pl.ANY)] * len(prev)
    return specs, list(prev), len(prev)


def _mem_kv_kernel(x_ref, w_ref, *refs):
    k_ref, v_ref, kb_ref, vb_ref = refs[-4:]
    d = N_HEADS * HEAD_DIM
    res = _dot(x_ref[...].astype(BF16), w_ref[...])
    for h in range(N_HEADS):
        k_ref[:, h, :] = res[:, _hs(h)]
        v_ref[:, h, :] = res[:, d + h * HEAD_DIM:d + (h + 1) * HEAD_DIM]
    kb_ref[...] = res[:, :d].astype(BF16)
    vb_ref[...] = res[:, d:].astype(BF16)


def _mem_kv(mem2d, w_kv, prev, layer, batch):
    rows, d = mem2d.shape
    n_mem = rows // batch
    depth = w_kv.shape[0]
    chain_specs, chain_args, n_chain = _chain_inputs(prev)
    out_spec = pl.BlockSpec((None, None, n_mem, N_HEADS, HEAD_DIM),
                            lambda b: (layer, b, 0, 0, 0))
    out_sds = jax.ShapeDtypeStruct((depth, batch, n_mem, N_HEADS, HEAD_DIM), F32)
    dense_spec = pl.BlockSpec((n_mem, d), lambda b: (b, 0))
    dense_sds = jax.ShapeDtypeStruct((rows, d), BF16)
    k_all, v_all, kb, vb = pl.pallas_call(
        _mem_kv_kernel,
        grid=(batch,),
        in_specs=[pl.BlockSpec((n_mem, d), lambda b: (b, 0)),
                  pl.BlockSpec((None, d, 2 * d), lambda b: (layer, 0, 0))] + chain_specs,
        out_specs=[out_spec, out_spec, dense_spec, dense_spec],
        out_shape=[out_sds, out_sds, dense_sds, dense_sds],
        input_output_aliases={2 + i: i for i in range(n_chain)},
        compiler_params=_params(1),
        name="mem_kv",
    )(mem2d, w_kv, *chain_args)
    return (k_all, v_all), (kb, vb)


def _sample_attn_scores(q_row, keys):
    sub = lax.broadcasted_iota(jnp.int32, (SUBLANES, HEAD_DIM), 0)
    q = jnp.zeros((SUBLANES, HEAD_DIM), F32)
    for h in range(N_HEADS):
        q = jnp.where(sub == h, jnp.broadcast_to(q_row(h), (SUBLANES, HEAD_DIM)), q)
    keys2d = keys.reshape(keys.shape[0] * N_HEADS, HEAD_DIM).astype(BF16)
    return _dot_nt(q.astype(BF16), keys2d)


def _sample_attn_softmax(s):
    s = s * (HEAD_DIM ** -0.5)
    head = lax.broadcasted_iota(jnp.int32, s.shape, 0) % SUBLANES
    col_head = lax.broadcasted_iota(jnp.int32, s.shape, 1) % N_HEADS
    s = jnp.where((col_head == head) | (head >= N_HEADS), s, -jnp.inf)
    p = jnp.exp(s - jnp.max(s, axis=-1, keepdims=True))
    return p / jnp.sum(p, axis=-1, keepdims=True)


def _sample_attn_output(p, vals):
    vals2d = vals.reshape(vals.shape[0] * N_HEADS, HEAD_DIM).astype(BF16)
    return _dot(p.astype(BF16), vals2d)


def _mlstm_prompt_kernel(x_ref, wqkvo_ref, bqkvo_ref, wif_ref, bif_ref, wg0_ref, bg0_ref,
                         mhg_ref, wmo_ref, xq_ref, sk_ref, sv_ref,
                         y_ref, c_ref, n_ref, m_ref, ox_ref, s_scr):
    t = pl.program_id(1)
    d = N_HEADS * HEAD_DIM
    ride = sk_ref.shape[0]
    seq0 = (pl.program_id(0) * pl.num_programs(1) + t) * ride

    @pl.when(t == 0)
    def _():
        c_ref[...] = jnp.zeros(c_ref.shape, F32)
        n_ref[...] = jnp.zeros(n_ref.shape, F32)
        m_ref[...] = jnp.zeros(m_ref.shape, F32)

    for i in range(ride):
        s_scr[i * SUBLANES:(i + 1) * SUBLANES, :] = _sample_attn_scores(
            lambda h, i=i: xq_ref[pl.ds(seq0 + i, 1), _hs(h)], sk_ref[i])

    xb = x_ref[...].astype(BF16)
    lc = xb.shape[0]
    gi = _dot(xb, wif_ref[:, 0:LANES]) + bif_ref[:, 0:LANES]
    gf = _dot(xb, wif_ref[:, LANES:2 * LANES]) + bif_ref[:, LANES:2 * LANES]
    lf = _log_sigmoid(gf)
    bcum = _scan_rows(lf, jnp.add)
    a = gi - bcum
    m0 = m_ref[0]
    mt = bcum + jnp.maximum(m0, _scan_rows(a, jnp.maximum))
    inter = jnp.exp(m0 + bcum - mt)
    bm = bcum - mt
    emt = jnp.exp(-mt)
    b_last = bcum[lc - 1:lc, :]
    m_last = mt[lc - 1:lc, :]
    ws = jnp.exp(a + b_last - m_last)
    decay = jnp.exp(m0 + b_last - m_last)
    a_t = a.T
    causal = (lax.broadcasted_iota(jnp.int32, (lc, lc), 1)
              <= lax.broadcasted_iota(jnp.int32, (lc, lc), 0))

    def head_proj(h):
        def proj(part):
            cols = slice(part * d + h * HEAD_DIM, part * d + (h + 1) * HEAD_DIM)
            return _dot(xb, wqkvo_ref[:, cols]) + bqkvo_ref[:, cols]
        return proj(0), proj(1) * (HEAD_DIM ** -0.5), proj(2), proj(3)

    hn_heads = []
    ahead = head_proj(0)
    for h in range(N_HEADS):
        q, k, v, o = ahead
        if h + 1 < N_HEADS:
            ahead = head_proj(h + 1)
        else:
            g0 = _dot(xb, wg0_ref[...]) + bg0_ref[...]
        qb = q.astype(BF16)
        vb = v.astype(BF16)
        col = slice(h, h + 1)
        logd = a_t[h:h + 1, :] + bm[:, col]
        dmat = jnp.exp(jnp.where(causal, logd, NEG))
        s = _dot_nt(qb, k.astype(BF16)) * dmat
        c_old = c_ref[0, h]
        n_old = n_ref[0, h:h + 1, :]
        inter_h = inter[:, col]
        num = _dot(s.astype(BF16), vb) + inter_h * _dot(qb, c_old.astype(BF16))
        den = (jnp.sum(s, axis=1, keepdims=True)
               + inter_h * jnp.sum(q * n_old, axis=1, keepdims=True))
        ht = num / jnp.maximum(jnp.abs(den), emt[:, col])
        kw = k * ws[:, col]
        c_ref[0, h] = decay[:, col] * c_old + _dot_tn(kw.astype(BF16), vb)
        n_ref[0, h:h + 1, :] = decay[:, col] * n_old + jnp.sum(kw, axis=0, keepdims=True)
        hn = _head_norm(jax.nn.sigmoid(o) * ht) * mhg_ref[:, _hs(h)]
        hn_heads.append(hn.astype(BF16))
        if h == 0:
            s_scr[...] = _sample_attn_softmax(s_scr[...])
    m_ref[0] = m_last

    for i in range(ride):
        out = _sample_attn_output(s_scr[i * SUBLANES:(i + 1) * SUBLANES, :], sv_ref[i])
        for h in range(N_HEADS):
            ox_ref[pl.ds(seq0 + i, 1), _hs(h)] = out[h:h + 1, :]

    ym = _dot(jnp.concatenate(hn_heads, axis=1), wmo_ref[...])
    y_ref[...] = jax.nn.sigmoid(g0) * ym


def _mlstm_prompt(x2d, batch, w_qkvo, b_qkvo, w_if, b_if, w_rest, b_rest, mh_g, w_m_out,
                  proj_rest_s, mem_k, mem_v, layer):
    rows, d = x2d.shape
    nt = rows // batch // TIME_TILE
    wide = N_HEADS * d
    nseq, n_mem = mem_k.shape[1], mem_k.shape[2]
    ride = nseq // (batch * nt)
    assert ride * batch * nt == nseq

    def lw(shape, col):
        return pl.BlockSpec((None,) + shape, lambda b, t: (layer, 0, col))

    kv_spec = pl.BlockSpec((None, ride, n_mem, N_HEADS, HEAD_DIM),
                           lambda b, t: (layer, b * nt + t, 0, 0, 0))
    return pl.pallas_call(
        _mlstm_prompt_kernel,
        grid=(batch, nt),
        in_specs=[pl.BlockSpec((TIME_TILE, d), lambda b, t: (b * nt + t, 0)),
                  lw((d, wide), 0), lw((1, wide), 0),
                  lw((d, 2 * LANES), 0), lw((1, 2 * LANES), 0),
                  lw((d, d), REST_G0), lw((1, d), REST_G0),
                  lw((1, d), 0), lw((d, d), 0),
                  pl.BlockSpec((nseq, d), lambda b, t: (0, REST_XQ)), kv_spec, kv_spec],
        out_specs=[pl.BlockSpec((TIME_TILE, d), lambda b, t: (b * nt + t, 0)),
                   pl.BlockSpec((1, N_HEADS, HEAD_DIM, HEAD_DIM), lambda b, t: (b, 0, 0, 0)),
                   pl.BlockSpec((1, N_HEADS, HEAD_DIM), lambda b, t: (b, 0, 0)),
                   pl.BlockSpec((1, 1, LANES), lambda b, t: (b, 0, 0)),
                   pl.BlockSpec((nseq, d), lambda b, t: (0, 0))],
        out_shape=[jax.ShapeDtypeStruct((rows, d), F32),
                   jax.ShapeDtypeStruct((batch, N_HEADS, HEAD_DIM, HEAD_DIM), F32),
                   jax.ShapeDtypeStruct((batch, N_HEADS, HEAD_DIM), F32),
                   jax.ShapeDtypeStruct((batch, 1, LANES), F32),
                   jax.ShapeDtypeStruct((nseq, d), F32)],
        scratch_shapes=[pltpu.VMEM((ride * SUBLANES, N_HEADS * n_mem), F32)],
        compiler_params=_params(2),
        name="mlstm_prompt",
    )(x2d, w_qkvo, b_qkvo, w_if, b_if, w_rest, b_rest, mh_g, w_m_out, proj_rest_s, mem_k, mem_v)


def _pool_window_sums(ext, window):
    s, span = ext, 1
    while span < window:
        s = s + pltpu.roll(s, span, 0)
        span *= 2
    return s


def _mix_prompt_kernel(x_ref, ymg_ref, mk_ref, mv_ref, wu_ref, bu_ref, wxq_ref, bxq_ref,
                       wg1_ref, bg1_ref, wg2_ref, bg2_ref, wpool_ref, pscale_ref, wxo_ref,
                       wo_ref, lng_ref, lnb_ref, sq_ref, sv_ref, skw_ref, sdec_ref, sc_ref,
                       prev_c_ref, x1_ref, pool_ref, sr_ref, scn_ref, ext_scr):
    del prev_c_ref
    t = pl.program_id(1)
    tt = x_ref.shape[0]
    carry = 2 * SUBLANES
    ride = sc_ref.shape[0]
    seq0 = (pl.program_id(0) * pl.num_programs(1) + t) * ride

    @pl.when(t == 0)
    def _():
        ext_scr[0:carry, :] = jnp.zeros((carry, ext_scr.shape[1]), F32)

    for i in range(ride):
        _sample_memory_update(seq0 + i, i, sq_ref, sv_ref, skw_ref, sdec_ref, sc_ref, scn_ref,
                              sr_ref)

    x = x_ref[...]
    xb = x.astype(BF16)

    u = _dot(xb, wu_ref[...]) + bu_ref[...]
    xq = (_dot(xb, wxq_ref[...]) + bxq_ref[...]).astype(BF16)

    ext_scr[carry:carry + tt, :] = u
    pos = t * tt + lax.broadcasted_iota(jnp.int32, (tt, 1), 0)
    scores = [_dot_nt(xq[:, _hs(h)], mk_ref[:, _hs(h)]) for h in range(N_HEADS)]
    yp_groups = []
    for g, window in enumerate(POOL_WINDOWS):
        cols = slice(g * POOL_GROUP_DIM, (g + 1) * POOL_GROUP_DIM)
        wsum = _pool_window_sums(ext_scr[:, cols], window)[carry:, :]
        cnt = jnp.minimum(window, pos + 1).astype(F32)
        dlt = wsum / cnt - u[:, cols]
        yp_groups.append(_dot(dlt.astype(BF16), wpool_ref[g]))
    ext_scr[0:carry, :] = u[tt - carry:tt, :]
    pool_ref[0] = u[tt - carry:tt, :]
    g1 = _dot(xb, wg1_ref[...]) + bg1_ref[...]

    heads = []
    for h in range(N_HEADS):
        s = scores[h] * (HEAD_DIM ** -0.5)
        p = jnp.exp(s - jnp.max(s, axis=-1, keepdims=True))
        p = p / jnp.sum(p, axis=-1, keepdims=True)
        heads.append(_dot(p.astype(BF16), mv_ref[:, _hs(h)]).astype(BF16))
    g2 = _dot(xb, wg2_ref[...]) + bg2_ref[...]
    yp = jnp.concatenate(yp_groups, axis=1) * pscale_ref[...]
    yx = _dot(jnp.concatenate(heads, axis=1), wxo_ref[...])
    mix = ymg_ref[...] + jax.nn.sigmoid(g1) * yp + jax.nn.sigmoid(g2) * yx
    y = ALPHA * x + _dot(mix.astype(BF16), wo_ref[...])
    x1_ref[...] = _layer_norm(y, lng_ref[...], lnb_ref[...])


def _mix_prompt(x2d, ymg, mk, mv, batch, w_rest, b_rest, w_pool, pool_scale, w_x_out, w_o,
                ln_g, ln_b, proj_qkvo_s, kw_s, dec_s, state_c, prev_c, layer):
    rows, d = x2d.shape
    nt = rows // batch // TIME_TILE
    n_mem = mk.shape[0] // batch
    carry = 2 * SUBLANES
    nseq = state_c.shape[1]
    ride = nseq // (batch * nt)
    assert ride * batch * nt == nseq

    def lw(shape, col):
        return pl.BlockSpec((None,) + shape, lambda b, t: (layer, 0, col))

    def sample_rows(col):
        return pl.BlockSpec((nseq, d), lambda b, t: (0, col))

    row_spec = pl.BlockSpec((TIME_TILE, d), lambda b, t: (b * nt + t, 0))
    mem_spec = pl.BlockSpec((n_mem, d), lambda b, t: (b, 0))
    c_spec = pl.BlockSpec((None, ride, N_HEADS, HEAD_DIM, HEAD_DIM),
                          lambda b, t: (layer, b * nt + t, 0, 0, 0))
    return pl.pallas_call(
        _mix_prompt_kernel,
        grid=(batch, nt),
        in_specs=[row_spec, row_spec, mem_spec, mem_spec,
                  lw((d, d), REST_U), lw((1, d), REST_U),
                  lw((d, d), REST_XQ), lw((1, d), REST_XQ),
                  lw((d, d), REST_G1), lw((1, d), REST_G1),
                  lw((d, d), REST_G2), lw((1, d), REST_G2),
                  pl.BlockSpec((None, len(POOL_WINDOWS), POOL_GROUP_DIM, POOL_GROUP_DIM),
                               lambda b, t: (layer, 0, 0, 0)),
                  lw((1, d), 0), lw((d, d), 0), lw((d, d), 0), lw((1, d), 0), lw((1, d), 0),
                  sample_rows(0), sample_rows(2), sample_rows(0), sample_rows(0), c_spec,
                  pl.BlockSpec(memory_space=pl.ANY)],
        out_specs=[row_spec, pl.BlockSpec((1, carry, d), lambda b, t: (b, 0, 0)),
                   sample_rows(0), c_spec],
        out_shape=[jax.ShapeDtypeStruct((rows, d), F32),
                   jax.ShapeDtypeStruct((batch, carry, d), F32),
                   jax.ShapeDtypeStruct((nseq, d), F32),
                   jax.ShapeDtypeStruct(state_c.shape, F32)],
        input_output_aliases={23: 3},
        scratch_shapes=[pltpu.VMEM((carry + TIME_TILE, d), F32)],
        compiler_params=_params(2),
        name="mix_prompt",
    )(x2d, ymg, mk, mv, w_rest, b_rest, w_rest, b_rest, w_rest, b_rest, w_rest, b_rest,
      w_pool, pool_scale, w_x_out, w_o, ln_g, ln_b,
      proj_qkvo_s, proj_qkvo_s, kw_s, dec_s, state_c, prev_c)


def _ffn_chunk(xb, j, d_ff, wup_ref, cw_ref, cb_ref, wdn_ref, conv_taps):
    halves = _ffn_up(xb, j, d_ff, wup_ref)
    return _ffn_gate_down(halves, j, d_ff, cw_ref, cb_ref, wdn_ref, conv_taps), halves


def _ffn_up(xb, j, d_ff, wup_ref):
    return [_dot(xb, wup_ref[:, base + j * FF_CHUNK:base + (j + 1) * FF_CHUNK])
            for base in (0, d_ff)]


def _ffn_gate_down(halves, j, d_ff, cw_ref, cb_ref, wdn_ref, conv_taps):
    conv = []
    for base, hup in zip((0, d_ff), halves):
        cols = slice(base + j * FF_CHUNK, base + (j + 1) * FF_CHUNK)
        prev2, prev1 = conv_taps(hup, cols)
        conv.append(prev2 * cw_ref[0:1, cols] + prev1 * cw_ref[1:2, cols]
                    + hup * cw_ref[2:3, cols] + cb_ref[:, cols])
    act = jax.nn.gelu(conv[0]) * conv[1]
    return _dot(act.astype(BF16), wdn_ref[j * FF_CHUNK:(j + 1) * FF_CHUNK, :])


def _ffn_prompt_kernel(x_ref, wup_ref, cw_ref, cb_ref, wdn_ref, lng_ref, lnb_ref,
                       o_ref, conv_ref, ext_scr):
    t = pl.program_id(1)
    tt = x_ref.shape[0]
    d_ff = wdn_ref.shape[0]
    n_chunks = d_ff // FF_CHUNK

    @pl.when(t == 0)
    def _():
        ext_scr[0:SUBLANES, :] = jnp.zeros((SUBLANES, ext_scr.shape[1]), F32)

    items = [(r0, j) for r0 in range(0, tt, FF_SUB_ROWS) for j in range(n_chunks)]
    xs = {r0: x_ref[r0:r0 + FF_SUB_ROWS, :] for r0 in range(0, tt, FF_SUB_ROWS)}
    xbs = {r0: xs[r0].astype(BF16) for r0 in xs}

    def up(item):
        r0, j = item
        return _ffn_up(xbs[r0], j, d_ff, wup_ref)

    ups = [up(item) for item in items[:FF_LOOKAHEAD]]
    acc = None
    for i, (r0, j) in enumerate(items):
        if i + FF_LOOKAHEAD < len(items):
            ups.append(up(items[i + FF_LOOKAHEAD]))
        halves = ups[i]
        ups[i] = None
        lo = SUBLANES + r0

        def conv_taps(hup, cols):
            ext_scr[lo:lo + FF_SUB_ROWS, cols] = hup
            return (ext_scr[lo - 2:lo - 2 + FF_SUB_ROWS, cols],
                    ext_scr[lo - 1:lo - 1 + FF_SUB_ROWS, cols])

        part = _ffn_gate_down(halves, j, d_ff, cw_ref, cb_ref, wdn_ref, conv_taps)
        acc = part if j == 0 else part + acc
        if r0 + FF_SUB_ROWS == tt:
            for base, hup in zip((0, d_ff), halves):
                cols = slice(base + j * FF_CHUNK, base + (j + 1) * FF_CHUNK)
                ext_scr[0:SUBLANES, cols] = hup[FF_SUB_ROWS - SUBLANES:, :]
                conv_ref[0, :, cols] = hup[FF_SUB_ROWS - SUBLANES:, :]
        if j == n_chunks - 1:
            o_ref[r0:r0 + FF_SUB_ROWS, :] = _layer_norm(ALPHA * xs[r0] + acc, lng_ref[...],
                                                       lnb_ref[...])


def _ffn_prompt(x2d, batch, w_up, conv_w, conv_b, w_down, ln_g, ln_b, layer):
    rows, d = x2d.shape
    nt = rows // batch // FF_TILE
    d_ff = w_down.shape[1]

    def lw(shape):
        return pl.BlockSpec((None,) + shape, lambda b, t: (layer, 0, 0),
                            pipeline_mode=pl.Buffered(1))

    row_spec = pl.BlockSpec((FF_TILE, d), lambda b, t: (b * nt + t, 0))
    return pl.pallas_call(
        _ffn_prompt_kernel,
        grid=(batch, nt),
        in_specs=[row_spec, lw((d, 2 * d_ff)), lw((CONV_WIDTH, 2 * d_ff)), lw((1, 2 * d_ff)),
                  lw((d_ff, d)), lw((1, d)), lw((1, d))],
        out_specs=[row_spec, pl.BlockSpec((1, SUBLANES, 2 * d_ff), lambda b, t: (b, 0, 0))],
        out_shape=[jax.ShapeDtypeStruct((rows, d), F32),
                   jax.ShapeDtypeStruct((batch, SUBLANES, 2 * d_ff), F32)],
        scratch_shapes=[pltpu.VMEM((SUBLANES + FF_TILE, 2 * d_ff), F32)],
        compiler_params=_params(2),
        name="ffn_prompt",
    )(x2d, w_up, conv_w, conv_b, w_down, ln_g, ln_b)


def _ffn_sample_kernel(x_ref, st_ref, wup_ref, cw_ref, cb_ref, wdn_ref, lng_ref, lnb_ref,
                       o_ref, stn_ref):
    d_ff = wdn_ref.shape[0]
    x = x_ref[...]
    xb = x.astype(BF16)

    def conv_taps(hup, cols):
        return st_ref[:, 0, cols], st_ref[:, 1, cols]

    acc = None
    for j in range(d_ff // FF_CHUNK):
        part, halves = _ffn_chunk(xb, j, d_ff, wup_ref, cw_ref, cb_ref, wdn_ref, conv_taps)
        for base, hup in zip((0, d_ff), halves):
            cols = slice(base + j * FF_CHUNK, base + (j + 1) * FF_CHUNK)
            stn_ref[:, 0, cols] = st_ref[:, 1, cols]
            stn_ref[:, 1, cols] = hup
        acc = part if acc is None else acc + part
    o_ref[...] = _layer_norm(ALPHA * x + acc, lng_ref[...], lnb_ref[...])


def _ffn_sample(x, conv_state, w_up, conv_w, conv_b, w_down, ln_g, ln_b, layer, tile):
    rows, d = x.shape
    d_ff = w_down.shape[1]
    st_shape = conv_state.shape[2:]

    def lw(shape):
        return pl.BlockSpec((None,) + shape, lambda i: (layer, 0, 0))

    return pl.pallas_call(
        _ffn_sample_kernel,
        grid=(rows // tile,),
        in_specs=[pl.BlockSpec((tile, d), lambda i: (i, 0)),
                  pl.BlockSpec((None, tile) + st_shape, lambda i: (layer, i, 0, 0)),
                  lw((d, 2 * d_ff)), lw((CONV_WIDTH, 2 * d_ff)), lw((1, 2 * d_ff)),
                  lw((d_ff, d)), lw((1, d)), lw((1, d))],
        out_specs=[pl.BlockSpec((tile, d), lambda i: (i, 0)),
                   pl.BlockSpec((tile,) + st_shape, lambda i: (i, 0, 0))],
        out_shape=[jax.ShapeDtypeStruct((rows, d), F32),
                   jax.ShapeDtypeStruct((rows,) + st_shape, F32)],
        compiler_params=_params(1),
        name="ffn_sample",
    )(x, conv_state, w_up, conv_w, conv_b, w_down, ln_g, ln_b)


def _mlstm_pre_kernel(q_ref, k_ref, g_ref, n_ref, m_ref,
                      kw_ref, dec_ref, s_ref, den_ref, nn_ref, mn_ref):
    gi = g_ref[:, 0:N_HEADS]
    lf = _log_sigmoid(g_ref[:, LANES:LANES + N_HEADS])
    m0 = m_ref[...]
    mt = jnp.maximum(m0 + lf, gi)
    inter_all = jnp.exp(m0 + lf - mt)
    dm_all = jnp.exp(gi - mt)
    emt_all = jnp.exp(-mt)
    mn_ref[...] = mt
    for h in range(N_HEADS):
        col = slice(h, h + 1)
        q = q_ref[:, _hs(h)]
        k = k_ref[:, _hs(h)] * (HEAD_DIM ** -0.5)
        n_old = n_ref[:, h, :]
        inter, dm = inter_all[:, col], dm_all[:, col]
        s = jnp.sum(q * k, axis=1, keepdims=True) * dm
        den = s + inter * jnp.sum(q * n_old, axis=1, keepdims=True)
        kw = k * dm
        kw_ref[:, _hs(h)] = kw
        dec_ref[:, _hs(h)] = jnp.broadcast_to(inter, kw.shape)
        s_ref[:, _hs(h)] = jnp.broadcast_to(s, kw.shape)
        den_ref[:, _hs(h)] = jnp.broadcast_to(jnp.maximum(jnp.abs(den), emt_all[:, col]), kw.shape)
        nn_ref[:, h, :] = inter * n_old + kw


def _mlstm_pre(proj_qkvo, gates, state_n, state_m, layer, tile):
    nseq = proj_qkvo.shape[0]
    d = N_HEADS * HEAD_DIM

    def rows_at(col):
        return pl.BlockSpec((tile, d), lambda i: (i, col))

    wide_sds = jax.ShapeDtypeStruct((nseq, d), F32)
    return pl.pallas_call(
        _mlstm_pre_kernel,
        grid=(nseq // tile,),
        in_specs=[rows_at(0), rows_at(1),
                  pl.BlockSpec((tile, 2 * LANES), lambda i: (i, 0)),
                  pl.BlockSpec((None, tile, N_HEADS, HEAD_DIM), lambda i: (layer, i, 0, 0)),
                  pl.BlockSpec((None, tile, N_HEADS), lambda i: (layer, i, 0))],
        out_specs=[rows_at(0)] * 4 + [pl.BlockSpec((tile, N_HEADS, HEAD_DIM), lambda i: (i, 0, 0)),
                                      pl.BlockSpec((tile, N_HEADS), lambda i: (i, 0))],
        out_shape=[wide_sds] * 4 + [jax.ShapeDtypeStruct((nseq, N_HEADS, HEAD_DIM), F32),
                                    jax.ShapeDtypeStruct((nseq, N_HEADS), F32)],
        compiler_params=_params(1),
        name="mlstm_pre",
    )(proj_qkvo, proj_qkvo, gates, state_n, state_m)


def _sample_memory_update(seq, i, q_ref, v_ref, kw_ref, dec_ref, c_ref, cn_ref, r_ref):
    mxu_rows = 2 * SUBLANES
    row0 = lax.broadcasted_iota(jnp.int32, (mxu_rows, HEAD_DIM), 0) == 0
    row = pl.ds(seq, 1)
    for h in range(N_HEADS):
        qrow = jnp.broadcast_to(q_ref[row, _hs(h)], (mxu_rows, HEAD_DIM)).astype(BF16)
        c_old = c_ref[i, h]
        r_ref[row, _hs(h)] = _dot(qrow, c_old.astype(BF16))[0:1, :]
        kw = jnp.where(row0, jnp.broadcast_to(kw_ref[row, _hs(h)], (mxu_rows, HEAD_DIM)), 0.0)
        vrow = jnp.broadcast_to(v_ref[row, _hs(h)], (mxu_rows, HEAD_DIM))
        cn_ref[i, h] = (dec_ref[row, _hs(h)] * c_old
                        + _dot_tn(kw.astype(BF16), vrow.astype(BF16)))


def _merge_step_kernel(x_ref, r_ref, s_ref, den_ref, dec_ref, v_ref, o_ref, mhg_ref, ox_ref,
                       u_ref, g0_ref, g1_ref, g2_ref, pool_ref,
                       wmo_ref, wpool_ref, pscale_ref, wxo_ref, wo_ref, lng_ref, lnb_ref,
                       x1_ref, pooln_ref):
    x = x_ref[...]
    u = u_ref[...]
    ht = (s_ref[...] * v_ref[...] + dec_ref[...] * r_ref[...]) / den_ref[...]
    hc = jax.nn.sigmoid(o_ref[...]) * ht
    hn = jnp.concatenate([_head_norm(hc[:, _hs(h)]) for h in range(N_HEADS)], axis=1)
    hn = hn * mhg_ref[...]
    for r in range(1, POOL_BUF):
        pooln_ref[:, r - 1, :] = pool_ref[:, r, :]
    pooln_ref[:, POOL_BUF - 1, :] = u
    yp_groups = []
    for g, window in enumerate(POOL_WINDOWS):
        cols = slice(g * POOL_GROUP_DIM, (g + 1) * POOL_GROUP_DIM)
        wsum = u[:, cols]
        for back in range(1, window):
            wsum = wsum + pool_ref[:, POOL_BUF - back, cols]
        cnt = float(min(window, PAST_LEN + 1))
        dlt = wsum / cnt - u[:, cols]
        yp_groups.append(_dot(dlt.astype(BF16), wpool_ref[g]))
    yp = jnp.concatenate(yp_groups, axis=1) * pscale_ref[...]
    ym = _dot(hn.astype(BF16), wmo_ref[...])
    yx = _dot(ox_ref[...].astype(BF16), wxo_ref[...])
    mix = (jax.nn.sigmoid(g0_ref[...]) * ym + jax.nn.sigmoid(g1_ref[...]) * yp
           + jax.nn.sigmoid(g2_ref[...]) * yx)
    y = ALPHA * x + _dot(mix.astype(BF16), wo_ref[...])
    x1_ref[...] = _layer_norm(y, lng_ref[...], lnb_ref[...])


def _merge_step(x, readout, s_rep, den_rep, dec_rep, proj_qkvo, mh_g, ox, proj_rest, pool_state,
                w_m_out, w_pool, pool_scale, w_x_out, w_o, ln_g, ln_b, layer, tile):
    rows, d = x.shape
    st_shape = pool_state.shape[2:]

    def lw(shape):
        return pl.BlockSpec((None,) + shape, lambda i: (layer,) + (0,) * len(shape))

    def rows_at(col):
        return pl.BlockSpec((tile, d), lambda i: (i, col))

    return pl.pallas_call(
        _merge_step_kernel,
        grid=(rows // tile,),
        in_specs=[rows_at(0), rows_at(0), rows_at(0), rows_at(0), rows_at(0),
                  rows_at(2), rows_at(3), lw((1, d)), rows_at(0),
                  rows_at(REST_U), rows_at(REST_G0), rows_at(REST_G1), rows_at(REST_G2),
                  pl.BlockSpec((None, tile) + st_shape, lambda i: (layer, i, 0, 0)),
                  lw((d, d)), lw((len(POOL_WINDOWS), POOL_GROUP_DIM, POOL_GROUP_DIM)),
                  lw((1, d)), lw((d, d)), lw((d, d)), lw((1, d)), lw((1, d))],
        out_specs=[rows_at(0), pl.BlockSpec((tile,) + st_shape, lambda i: (i, 0, 0))],
        out_shape=[jax.ShapeDtypeStruct((rows, d), F32),
                   jax.ShapeDtypeStruct((rows,) + st_shape, F32)],
        compiler_params=_params(1),
        name="merge_step",
    )(x, readout, s_rep, den_rep, dec_rep, proj_qkvo, proj_qkvo, mh_g, ox,
      proj_rest, proj_rest, proj_rest, proj_rest, pool_state, w_m_out, w_pool,
      pool_scale, w_x_out, w_o, ln_g, ln_b)


def kernel(x_prompt, mem_prompt, x_sample, cache_mem_k, cache_mem_v, state_C, state_n, state_m,
           state_pool, state_conv, ln_in_g, ln_in_b, w_in, b_in, mh_g, w_m_out, w_pool,
           pool_scale, w_mem_kv, w_x_out, w_o, ln1_g, ln1_b, w_up, conv_w, conv_b, w_down,
           ln2_g, ln2_b):
    batch, seq, d = x_prompt.shape
    nseq = x_sample.shape[0]
    n_mem = mem_prompt.shape[1]
    depth = w_in.shape[0]
    d_ff = w_down.shape[1]
    wide = N_HEADS * d
    gate_off = wide
    rest_off = wide + 2 * N_HEADS
    assert d == N_HEADS * HEAD_DIM and seq % TIME_TILE == 0 and d_ff % FF_CHUNK == 0
    assert depth == DEPTH and x_sample.shape[1] == 1

    def row(v):
        return v.reshape(depth, 1, v.shape[-1])

    w_qkvo, w_rest = w_in[:, :, :wide].astype(BF16), w_in[:, :, rest_off:].astype(BF16)
    b_qkvo, b_rest = row(b_in[:, :wide]), row(b_in[:, rest_off:])
    lane_pad = ((0, 0), (0, 0), (0, LANES - N_HEADS))
    w_if = jnp.concatenate(
        [jnp.pad(w_in[:, :, gate_off:gate_off + N_HEADS], lane_pad),
         jnp.pad(w_in[:, :, gate_off + N_HEADS:rest_off], lane_pad)], axis=2).astype(BF16)
    b_if = row(jnp.concatenate(
        [jnp.pad(b_in[:, gate_off:gate_off + N_HEADS], lane_pad[1:]),
         jnp.pad(b_in[:, gate_off + N_HEADS:rest_off], lane_pad[1:])], axis=1))
    w_m_out_b, w_pool_b, w_x_out_b, w_o_b = (w.astype(BF16) for w in (w_m_out, w_pool, w_x_out, w_o))
    w_up_b, w_down_b = w_up.astype(BF16), w_down.astype(BF16)
    w_kv_b = w_mem_kv.astype(BF16)
    mh_g_r, pool_scale_r, conv_b_r = row(mh_g), row(pool_scale), row(conv_b)
    ln1_g_r, ln1_b_r, ln2_g_r, ln2_b_r = row(ln1_g), row(ln1_b), row(ln2_g), row(ln2_b)

    xp = x_prompt.reshape(batch * seq, d)
    xs = x_sample.reshape(nseq, d)
    mem2d = mem_prompt.reshape(batch * n_mem, d)

    xp = _layer_norm_rows(xp, ln_in_g.reshape(1, d), ln_in_b.reshape(1, d), 2 * TIME_TILE)
    xs = _layer_norm_rows(xs, ln_in_g.reshape(1, d), ln_in_b.reshape(1, d), nseq)

    outs = {name: [] for name in ("pC", "pn", "pm", "ppool", "pconv", "sn", "sm", "spool", "sconv")}
    p_mem = (jnp.zeros((depth, batch, n_mem, N_HEADS, HEAD_DIM), F32),) * 2
    s_c = jnp.zeros(state_C.shape, F32)
    for l in range(depth):
        proj_qkvo = _matmul_bias(xs, w_qkvo, b_qkvo, l, nseq, d, "proj_sample")
        proj_rest = _matmul_bias(xs, w_rest, b_rest, l, nseq, d, "proj_sample")
        gates = _matmul_bias(xs, w_if, b_if, l, nseq, 2 * LANES, "gates_sample")
        kw_s, dec_s, s_rep, den_rep, s_n, s_m = _mlstm_pre(proj_qkvo, gates, state_n, state_m,
                                                          l, nseq // 2)

        p_mem, (mk_b, mv_b) = _mem_kv(mem2d, w_kv_b, p_mem, l, batch)
        ymg, p_c, p_n, p_m, ox = _mlstm_prompt(xp, batch, w_qkvo, b_qkvo, w_if, b_if, w_rest,
                                               b_rest, mh_g_r, w_m_out_b, proj_rest,
                                               cache_mem_k, cache_mem_v, l)
        x1, p_pool, readout, s_c = _mix_prompt(xp, ymg, mk_b, mv_b, batch, w_rest, b_rest,
                                               w_pool_b, pool_scale_r, w_x_out_b, w_o_b, ln1_g_r,
                                               ln1_b_r, proj_qkvo, kw_s, dec_s, state_C, s_c, l)
        xp, p_conv = _ffn_prompt(x1, batch, w_up_b, conv_w, conv_b_r, w_down_b, ln2_g_r,
                                 ln2_b_r, l)
        outs["pC"].append(p_c)
        outs["pn"].append(p_n)
        outs["pm"].append(p_m[:, 0, :N_HEADS])
        outs["ppool"].append(p_pool[:, 2 * SUBLANES - POOL_BUF:, :])
        outs["pconv"].append(p_conv[:, SUBLANES - (CONV_WIDTH - 1):, :])

        x1s, s_pool = _merge_step(xs, readout, s_rep, den_rep, dec_s, proj_qkvo, mh_g_r, ox,
                                  proj_rest, state_pool, w_m_out_b, w_pool_b, pool_scale_r,
                                  w_x_out_b, w_o_b, ln1_g_r, ln1_b_r, l, nseq // 2)
        xs, s_conv = _ffn_sample(x1s, state_conv, w_up_b, conv_w, conv_b_r, w_down_b, ln2_g_r,
                                 ln2_b_r, l, nseq // 2)
        outs["sn"].append(s_n)
        outs["sm"].append(s_m)
        outs["spool"].append(s_pool)
        outs["sconv"].append(s_conv)

    stacked = {k: jnp.stack(v) for k, v in outs.items()}
    return (xp.reshape(batch, seq, d), xs.reshape(nseq, 1, d),
            stacked["pC"], stacked["pn"], stacked["pm"], stacked["ppool"], stacked["pconv"],
            p_mem[0], p_mem[1],
            s_c, stacked["sn"], stacked["sm"], stacked["spool"], stacked["sconv"])
```

```python
import jax
import jax.numpy as jnp
from jax import lax
from jax.experimental import pallas as pl
from jax.experimental.pallas import tpu as pltpu

F32 = jnp.float32
BF16 = jnp.bfloat16

N_HEADS = 4
HEAD_DIM = 256
POOL_WINDOWS = (2, 4, 8, 16)
POOL_GROUP_DIM = 256
POOL_BUF = 15
CONV_WIDTH = 3
DEPTH = 4
PAST_LEN = 16384
ALPHA = (2.0 * DEPTH) ** 0.25
LN_EPS = 1e-5
NEG = -1e30

SUBLANES = 8
LANES = 128
TIME_TILE = 256
MLSTM_TILE = 512
FF_CHUNK = 256
MIX_TILE = 512
FF_TILE = 512
FF_SUB_ROWS = 256
FF_LOOKAHEAD = 3
VMEM_LIMIT = 56 * 1024 * 1024

REST_U, REST_XQ, REST_G0, REST_G1, REST_G2 = range(5)


def _params(n_axes):
    return pltpu.CompilerParams(
        dimension_semantics=("arbitrary",) * n_axes, vmem_limit_bytes=VMEM_LIMIT)


def _dot(a, b):
    return jnp.dot(a, b, preferred_element_type=F32)


def _dot_nt(a, b):
    return lax.dot_general(a, b, (((1,), (1,)), ((), ())), preferred_element_type=F32)


def _dot_tn(a, b):
    return lax.dot_general(a, b, (((0,), (0,)), ((), ())), preferred_element_type=F32)


def _layer_norm(x, g, b):
    mu = jnp.mean(x, axis=-1, keepdims=True)
    xc = x - mu
    var = jnp.mean(xc * xc, axis=-1, keepdims=True)
    return xc * lax.rsqrt(var + LN_EPS) * g + b


def _head_norm(x):
    mu = jnp.mean(x, axis=-1, keepdims=True)
    xc = x - mu
    var = jnp.mean(xc * xc, axis=-1, keepdims=True)
    return xc * lax.rsqrt(var + LN_EPS)


def _log_sigmoid(x):
    return jnp.minimum(x, 0.0) - jnp.log1p(jnp.exp(-jnp.abs(x)))


def _scan_rows(x, op):
    n = x.shape[0]
    rows = lax.broadcasted_iota(jnp.int32, x.shape, 0)
    shift = 1
    while shift < n:
        x = jnp.where(rows >= shift, op(x, pltpu.roll(x, shift, 0)), x)
        shift *= 2
    return x


def _hs(h):
    return slice(h * HEAD_DIM, (h + 1) * HEAD_DIM)


def _ln_kernel(x_ref, g_ref, b_ref, o_ref):
    o_ref[...] = _layer_norm(x_ref[...], g_ref[...], b_ref[...])


def _layer_norm_rows(x, g, b, tile):
    rows, d = x.shape
    return pl.pallas_call(
        _ln_kernel,
        grid=(rows // tile,),
        in_specs=[pl.BlockSpec((tile, d), lambda i: (i, 0)),
                  pl.BlockSpec((1, d), lambda i: (0, 0)),
                  pl.BlockSpec((1, d), lambda i: (0, 0))],
        out_specs=pl.BlockSpec((tile, d), lambda i: (i, 0)),
        out_shape=jax.ShapeDtypeStruct((rows, d), F32),
        compiler_params=_params(1),
        name="ln_rows",
    )(x, g, b)


def _mm_kernel(x_ref, w_ref, b_ref, o_ref):
    o_ref[...] = _dot(x_ref[...].astype(BF16), w_ref[...]) + b_ref[...]


def _matmul_bias(x, w_all, b_all, layer, tm, tn, name):
    rows, kdim = x.shape
    n = w_all.shape[-1]
    return pl.pallas_call(
        _mm_kernel,
        grid=(n // tn, rows // tm),
        in_specs=[pl.BlockSpec((tm, kdim), lambda j, i: (i, 0)),
                  pl.BlockSpec((None, kdim, tn), lambda j, i: (layer, 0, j)),
                  pl.BlockSpec((None, 1, tn), lambda j, i: (layer, 0, j))],
        out_specs=pl.BlockSpec((tm, tn), lambda j, i: (i, j)),
        out_shape=jax.ShapeDtypeStruct((rows, n), F32),
        compiler_params=_params(2),
        name=name,
    )(x, w_all, b_all)


def _split_in_proj_kernel(w_ref, qkvo_ref, rest_ref, gates_ref):
    wide = qkvo_ref.shape[1]
    rest_off = wide + 2 * N_HEADS
    qkvo_ref[...] = w_ref[:, 0:wide].astype(BF16)
    rest_ref[...] = w_ref[:, rest_off:rest_off + rest_ref.shape[1]].astype(BF16)
    lane = lax.broadcasted_iota(jnp.int32, (w_ref.shape[0], LANES), 1)
    for part in range(2):
        cols = w_ref[:, wide + part * N_HEADS:wide + (part + 1) * N_HEADS]
        padded = jnp.pad(cols, ((0, 0), (0, LANES - N_HEADS)))
        gates_ref[:, part * LANES:(part + 1) * LANES] = jnp.where(lane < N_HEADS, padded,
                                                                  0.0).astype(BF16)


def _split_in_proj(w_in, wide, rest_off):
    depth, d, in_width = w_in.shape
    rest = in_width - rest_off
    tile = 2 * LANES
    out_dims = (wide, rest, 2 * LANES)
    return pl.pallas_call(
        _split_in_proj_kernel,
        grid=(depth, d // tile),
        in_specs=[pl.BlockSpec((None, tile, in_width), lambda l, i: (l, i, 0))],
        out_specs=[pl.BlockSpec((None, tile, n), lambda l, i: (l, i, 0)) for n in out_dims],
        out_shape=[jax.ShapeDtypeStruct((depth, d, n), BF16) for n in out_dims],
        compiler_params=_params(2),
        name="split_in_proj",
    )(w_in)


def _chain_inputs(prev):
    specs = [pl.BlockSpec(memory_space=pl.ANY)] * len(prev)
    return specs, list(prev), len(prev)


def _mem_kv_kernel(x_ref, w_ref, *refs):
    k_ref, v_ref, kb_ref, vb_ref = refs[-4:]
    d = N_HEADS * HEAD_DIM
    res = _dot(x_ref[...].astype(BF16), w_ref[...])
    for h in range(N_HEADS):
        k_ref[:, h, :] = res[:, _hs(h)]
        v_ref[:, h, :] = res[:, d + h * HEAD_DIM:d + (h + 1) * HEAD_DIM]
    kb_ref[...] = res[:, :d].astype(BF16)
    vb_ref[...] = res[:, d:].astype(BF16)


def _mem_kv(mem2d, w_kv, prev, layer, batch):
    rows, d = mem2d.shape
    n_mem = rows // batch
    depth = w_kv.shape[0]
    chain_specs, chain_args, n_chain = _chain_inputs(prev)
    out_spec = pl.BlockSpec((None, None, n_mem, N_HEADS, HEAD_DIM),
                            lambda b: (layer, b, 0, 0, 0))
    out_sds = jax.ShapeDtypeStruct((depth, batch, n_mem, N_HEADS, HEAD_DIM), F32)
    dense_spec = pl.BlockSpec((n_mem, d), lambda b: (b, 0))
    dense_sds = jax.ShapeDtypeStruct((rows, d), BF16)
    k_all, v_all, kb, vb = pl.pallas_call(
        _mem_kv_kernel,
        grid=(batch,),
        in_specs=[pl.BlockSpec((n_mem, d), lambda b: (b, 0)),
                  pl.BlockSpec((None, d, 2 * d), lambda b: (layer, 0, 0))] + chain_specs,
        out_specs=[out_spec, out_spec, dense_spec, dense_spec],
        out_shape=[out_sds, out_sds, dense_sds, dense_sds],
        input_output_aliases={2 + i: i for i in range(n_chain)},
        compiler_params=_params(1),
        name="mem_kv",
    )(mem2d, w_kv, *chain_args)
    return (k_all, v_all), (kb, vb)


def _sample_attn_scores(q_row, keys):
    sub = lax.broadcasted_iota(jnp.int32, (SUBLANES, HEAD_DIM), 0)
    q = jnp.zeros((SUBLANES, HEAD_DIM), F32)
    for h in range(N_HEADS):
        q = jnp.where(sub == h, jnp.broadcast_to(q_row(h), (SUBLANES, HEAD_DIM)), q)
    keys2d = keys.reshape(keys.shape[0] * N_HEADS, HEAD_DIM).astype(BF16)
    return _dot_nt(q.astype(BF16), keys2d)


def _sample_attn_softmax(s):
    s = s * (HEAD_DIM ** -0.5)
    head = lax.broadcasted_iota(jnp.int32, s.shape, 0) % SUBLANES
    col_head = lax.broadcasted_iota(jnp.int32, s.shape, 1) % N_HEADS
    s = jnp.where((col_head == head) | (head >= N_HEADS), s, -jnp.inf)
    p = jnp.exp(s - jnp.max(s, axis=-1, keepdims=True))
    return p / jnp.sum(p, axis=-1, keepdims=True)


def _sample_attn_output(p, vals):
    vals2d = vals.reshape(vals.shape[0] * N_HEADS, HEAD_DIM).astype(BF16)
    return _dot(p.astype(BF16), vals2d)


def _mlstm_prompt_kernel(x_ref, wqkvo_ref, bqkvo_ref, wif_ref, bif_ref, wg0_ref, bg0_ref,
                         mhg_ref, wmo_ref, xq_ref, sk_ref, sv_ref,
                         y_ref, c_ref, n_ref, m_ref, ox_ref, s_scr):
    t = pl.program_id(1)
    d = N_HEADS * HEAD_DIM
    ride = sk_ref.shape[0]
    seq0 = (pl.program_id(0) * pl.num_programs(1) + t) * ride

    @pl.when(t == 0)
    def _():
        c_ref[...] = jnp.zeros(c_ref.shape, F32)
        n_ref[...] = jnp.zeros(n_ref.shape, F32)
        m_ref[...] = jnp.zeros(m_ref.shape, F32)

    for i in range(ride):
        s_scr[i * SUBLANES:(i + 1) * SUBLANES, :] = _sample_attn_scores(
            lambda h, i=i: xq_ref[pl.ds(seq0 + i, 1), _hs(h)], sk_ref[i])

    lc = TIME_TILE
    causal = (lax.broadcasted_iota(jnp.int32, (lc, lc), 1)
              <= lax.broadcasted_iota(jnp.int32, (lc, lc), 0))

    for chunk, r0 in enumerate(range(0, x_ref.shape[0], lc)):
        rows = slice(r0, r0 + lc)
        xb = x_ref[rows, :].astype(BF16)
        gi = _dot(xb, wif_ref[:, 0:LANES]) + bif_ref[:, 0:LANES]
        gf = _dot(xb, wif_ref[:, LANES:2 * LANES]) + bif_ref[:, LANES:2 * LANES]
        lf = _log_sigmoid(gf)
        bcum = _scan_rows(lf, jnp.add)
        a = gi - bcum
        m0 = m_ref[0]
        mt = bcum + jnp.maximum(m0, _scan_rows(a, jnp.maximum))
        inter = jnp.exp(m0 + bcum - mt)
        bm = bcum - mt
        emt = jnp.exp(-mt)
        b_last = bcum[lc - 1:lc, :]
        m_last = mt[lc - 1:lc, :]
        ws = jnp.exp(a + b_last - m_last)
        decay = jnp.exp(m0 + b_last - m_last)
        a_t = a.T

        def head_proj(h, xb=xb):
            def proj(part):
                cols = slice(part * d + h * HEAD_DIM, part * d + (h + 1) * HEAD_DIM)
                return _dot(xb, wqkvo_ref[:, cols]) + bqkvo_ref[:, cols]
            return proj(0), proj(1) * (HEAD_DIM ** -0.5), proj(2), proj(3)

        hn_heads = []
        ahead = head_proj(0)
        for h in range(N_HEADS):
            q, k, v, o = ahead
            if h + 1 < N_HEADS:
                ahead = head_proj(h + 1)
            else:
                g0 = _dot(xb, wg0_ref[...]) + bg0_ref[...]
            qb = q.astype(BF16)
            vb = v.astype(BF16)
            col = slice(h, h + 1)
            logd = a_t[h:h + 1, :] + bm[:, col]
            dmat = jnp.exp(jnp.where(causal, logd, NEG))
            s = _dot_nt(qb, k.astype(BF16)) * dmat
            c_old = c_ref[0, h]
            n_old = n_ref[0, h:h + 1, :]
            inter_h = inter[:, col]
            num = _dot(s.astype(BF16), vb) + inter_h * _dot(qb, c_old.astype(BF16))
            den = (jnp.sum(s, axis=1, keepdims=True)
                   + inter_h * jnp.sum(q * n_old, axis=1, keepdims=True))
            ht = num / jnp.maximum(jnp.abs(den), emt[:, col])
            kw = k * ws[:, col]
            c_ref[0, h] = decay[:, col] * c_old + _dot_tn(kw.astype(BF16), vb)
            n_ref[0, h:h + 1, :] = decay[:, col] * n_old + jnp.sum(kw, axis=0, keepdims=True)
            hn = _head_norm(jax.nn.sigmoid(o) * ht) * mhg_ref[:, _hs(h)]
            hn_heads.append(hn.astype(BF16))
            if chunk == 0 and h == 0:
                s_scr[...] = _sample_attn_softmax(s_scr[...])
        m_ref[0] = m_last

        if chunk == 0:
            for i in range(ride):
                out = _sample_attn_output(s_scr[i * SUBLANES:(i + 1) * SUBLANES, :], sv_ref[i])
                for h in range(N_HEADS):
                    ox_ref[pl.ds(seq0 + i, 1), _hs(h)] = out[h:h + 1, :]

        ym = _dot(jnp.concatenate(hn_heads, axis=1), wmo_ref[...])
        y_ref[rows, :] = jax.nn.sigmoid(g0) * ym


def _mlstm_prompt(x2d, batch, w_qkvo, b_qkvo, w_if, b_if, w_rest, b_rest, mh_g, w_m_out,
                  proj_rest_s, mem_k, mem_v, layer):
    rows, d = x2d.shape
    nt = rows // batch // MLSTM_TILE
    wide = N_HEADS * d
    nseq, n_mem = mem_k.shape[1], mem_k.shape[2]
    ride = nseq // (batch * nt)
    assert ride * batch * nt == nseq

    def lw(shape, col):
        return pl.BlockSpec((None,) + shape, lambda b, t: (layer, 0, col),
                            pipeline_mode=pl.Buffered(1))

    kv_spec = pl.BlockSpec((None, ride, n_mem, N_HEADS, HEAD_DIM),
                           lambda b, t: (layer, b * nt + t, 0, 0, 0))
    return pl.pallas_call(
        _mlstm_prompt_kernel,
        grid=(batch, nt),
        in_specs=[pl.BlockSpec((MLSTM_TILE, d), lambda b, t: (b * nt + t, 0)),
                  lw((d, wide), 0), lw((1, wide), 0),
                  lw((d, 2 * LANES), 0), lw((1, 2 * LANES), 0),
                  lw((d, d), REST_G0), lw((1, d), REST_G0),
                  lw((1, d), 0), lw((d, d), 0),
                  pl.BlockSpec((nseq, d), lambda b, t: (0, REST_XQ)), kv_spec, kv_spec],
        out_specs=[pl.BlockSpec((MLSTM_TILE, d), lambda b, t: (b * nt + t, 0)),
                   pl.BlockSpec((1, N_HEADS, HEAD_DIM, HEAD_DIM), lambda b, t: (b, 0, 0, 0)),
                   pl.BlockSpec((1, N_HEADS, HEAD_DIM), lambda b, t: (b, 0, 0)),
                   pl.BlockSpec((1, 1, LANES), lambda b, t: (b, 0, 0)),
                   pl.BlockSpec((nseq, d), lambda b, t: (0, 0))],
        out_shape=[jax.ShapeDtypeStruct((rows, d), F32),
                   jax.ShapeDtypeStruct((batch, N_HEADS, HEAD_DIM, HEAD_DIM), F32),
                   jax.ShapeDtypeStruct((batch, N_HEADS, HEAD_DIM), F32),
                   jax.ShapeDtypeStruct((batch, 1, LANES), F32),
                   jax.ShapeDtypeStruct((nseq, d), F32)],
        scratch_shapes=[pltpu.VMEM((ride * SUBLANES, N_HEADS * n_mem), F32)],
        compiler_params=_params(2),
        name="mlstm_prompt",
    )(x2d, w_qkvo, b_qkvo, w_if, b_if, w_rest, b_rest, mh_g, w_m_out, proj_rest_s, mem_k, mem_v)


def _pool_window_sums(ext, window):
    s, span = ext, 1
    while span < window:
        s = s + pltpu.roll(s, span, 0)
        span *= 2
    return s


def _mix_prompt_kernel(x_ref, ymg_ref, mk_ref, mv_ref, wu_ref, bu_ref, wxq_ref, bxq_ref,
                       wg1_ref, bg1_ref, wg2_ref, bg2_ref, wpool_ref, pscale_ref, wxo_ref,
                       wo_ref, lng_ref, lnb_ref, sq_ref, sv_ref, skw_ref, sdec_ref, sc_ref,
                       prev_c_ref, x1_ref, pool_ref, sr_ref, scn_ref, ext_scr):
    del prev_c_ref
    t = pl.program_id(1)
    tt = x_ref.shape[0]
    carry = 2 * SUBLANES
    ride = sc_ref.shape[0]
    seq0 = (pl.program_id(0) * pl.num_programs(1) + t) * ride

    @pl.when(t == 0)
    def _():
        ext_scr[0:carry, :] = jnp.zeros((carry, ext_scr.shape[1]), F32)

    for sub, r0 in enumerate(range(0, tt, TIME_TILE)):
        rows = slice(r0, r0 + TIME_TILE)
        for i in range(sub * ride // (tt // TIME_TILE), (sub + 1) * ride // (tt // TIME_TILE)):
            _sample_memory_update(seq0 + i, i, sq_ref, sv_ref, skw_ref, sdec_ref, sc_ref,
                                  scn_ref, sr_ref)

        x = x_ref[rows, :]
        xb = x.astype(BF16)

        u = _dot(xb, wu_ref[...]) + bu_ref[...]
        xq = (_dot(xb, wxq_ref[...]) + bxq_ref[...]).astype(BF16)

        ext_scr[carry + r0:carry + r0 + TIME_TILE, :] = u
        pos = t * tt + r0 + lax.broadcasted_iota(jnp.int32, (TIME_TILE, 1), 0)
        scores = [_dot_nt(xq[:, _hs(h)], mk_ref[:, _hs(h)]) for h in range(N_HEADS)]
        yp_groups = []
        for g, window in enumerate(POOL_WINDOWS):
            cols = slice(g * POOL_GROUP_DIM, (g + 1) * POOL_GROUP_DIM)
            ext = ext_scr[r0:r0 + carry + TIME_TILE, cols]
            wsum = _pool_window_sums(ext, window)[carry:, :]
            cnt = jnp.minimum(window, pos + 1).astype(F32)
            dlt = wsum / cnt - u[:, cols]
            yp_groups.append(_dot(dlt.astype(BF16), wpool_ref[g]))
        if r0 + TIME_TILE == tt:
            ext_scr[0:carry, :] = u[TIME_TILE - carry:, :]
            pool_ref[0] = u[TIME_TILE - carry:, :]
        g1 = _dot(xb, wg1_ref[...]) + bg1_ref[...]

        heads = []
        for h in range(N_HEADS):
            s = scores[h] * (HEAD_DIM ** -0.5)
            p = jnp.exp(s - jnp.max(s, axis=-1, keepdims=True))
            p = p / jnp.sum(p, axis=-1, keepdims=True)
            heads.append(_dot(p.astype(BF16), mv_ref[:, _hs(h)]).astype(BF16))
        g2 = _dot(xb, wg2_ref[...]) + bg2_ref[...]
        yp = jnp.concatenate(yp_groups, axis=1) * pscale_ref[...]
        yx = _dot(jnp.concatenate(heads, axis=1), wxo_ref[...])
        mix = ymg_ref[rows, :] + jax.nn.sigmoid(g1) * yp + jax.nn.sigmoid(g2) * yx
        y = ALPHA * x + _dot(mix.astype(BF16), wo_ref[...])
        x1_ref[rows, :] = _layer_norm(y, lng_ref[...], lnb_ref[...])


def _mix_prompt(x2d, ymg, mk, mv, batch, w_rest, b_rest, w_pool, pool_scale, w_x_out, w_o,
                ln_g, ln_b, proj_qkvo_s, kw_s, dec_s, state_c, prev_c, layer):
    rows, d = x2d.shape
    nt = rows // batch // MIX_TILE
    n_mem = mk.shape[0] // batch
    carry = 2 * SUBLANES
    nseq = state_c.shape[1]
    ride = nseq // (batch * nt)
    assert ride * batch * nt == nseq and ride % (MIX_TILE // TIME_TILE) == 0

    def lw(shape, col):
        return pl.BlockSpec((None,) + shape, lambda b, t: (layer, 0, col),
                            pipeline_mode=pl.Buffered(1))

    def sample_rows(col):
        return pl.BlockSpec((nseq, d), lambda b, t: (0, col))

    row_spec = pl.BlockSpec((MIX_TILE, d), lambda b, t: (b * nt + t, 0))
    mem_spec = pl.BlockSpec((n_mem, d), lambda b, t: (b, 0))
    c_spec = pl.BlockSpec((None, ride, N_HEADS, HEAD_DIM, HEAD_DIM),
                          lambda b, t: (layer, b * nt + t, 0, 0, 0))
    return pl.pallas_call(
        _mix_prompt_kernel,
        grid=(batch, nt),
        in_specs=[row_spec, row_spec, mem_spec, mem_spec,
                  lw((d, d), REST_U), lw((1, d), REST_U),
                  lw((d, d), REST_XQ), lw((1, d), REST_XQ),
                  lw((d, d), REST_G1), lw((1, d), REST_G1),
                  lw((d, d), REST_G2), lw((1, d), REST_G2),
                  pl.BlockSpec((None, len(POOL_WINDOWS), POOL_GROUP_DIM, POOL_GROUP_DIM),
                               lambda b, t: (layer, 0, 0, 0)),
                  lw((1, d), 0), lw((d, d), 0), lw((d, d), 0), lw((1, d), 0), lw((1, d), 0),
                  sample_rows(0), sample_rows(2), sample_rows(0), sample_rows(0), c_spec,
                  pl.BlockSpec(memory_space=pl.ANY)],
        out_specs=[row_spec, pl.BlockSpec((1, carry, d), lambda b, t: (b, 0, 0)),
                   sample_rows(0), c_spec],
        out_shape=[jax.ShapeDtypeStruct((rows, d), F32),
                   jax.ShapeDtypeStruct((batch, carry, d), F32),
                   jax.ShapeDtypeStruct((nseq, d), F32),
                   jax.ShapeDtypeStruct(state_c.shape, F32)],
        input_output_aliases={23: 3},
        scratch_shapes=[pltpu.VMEM((carry + MIX_TILE, d), F32)],
        compiler_params=_params(2),
        name="mix_prompt",
    )(x2d, ymg, mk, mv, w_rest, b_rest, w_rest, b_rest, w_rest, b_rest, w_rest, b_rest,
      w_pool, pool_scale, w_x_out, w_o, ln_g, ln_b,
      proj_qkvo_s, proj_qkvo_s, kw_s, dec_s, state_c, prev_c)


def _ffn_chunk(xb, j, d_ff, wup_ref, cw_ref, cb_ref, wdn_ref, conv_taps):
    halves = _ffn_up(xb, j, d_ff, wup_ref)
    return _ffn_gate_down(halves, j, d_ff, cw_ref, cb_ref, wdn_ref, conv_taps), halves


def _ffn_up(xb, j, d_ff, wup_ref):
    return [_dot(xb, wup_ref[:, base + j * FF_CHUNK:base + (j + 1) * FF_CHUNK])
            for base in (0, d_ff)]


def _ffn_gate_down(halves, j, d_ff, cw_ref, cb_ref, wdn_ref, conv_taps):
    conv = []
    for base, hup in zip((0, d_ff), halves):
        cols = slice(base + j * FF_CHUNK, base + (j + 1) * FF_CHUNK)
        prev2, prev1 = conv_taps(hup, cols)
        conv.append(prev2 * cw_ref[0:1, cols] + prev1 * cw_ref[1:2, cols]
                    + hup * cw_ref[2:3, cols] + cb_ref[:, cols])
    act = jax.nn.gelu(conv[0]) * conv[1]
    return _dot(act.astype(BF16), wdn_ref[j * FF_CHUNK:(j + 1) * FF_CHUNK, :])


def _ffn_prompt_kernel(x_ref, wup_ref, cw_ref, cb_ref, wdn_ref, lng_ref, lnb_ref,
                       o_ref, conv_ref, ext_scr):
    t = pl.program_id(1)
    tt = x_ref.shape[0]
    d_ff = wdn_ref.shape[0]
    n_chunks = d_ff // FF_CHUNK

    @pl.when(t == 0)
    def _():
        ext_scr[0:SUBLANES, :] = jnp.zeros((SUBLANES, ext_scr.shape[1]), F32)

    items = [(r0, j) for r0 in range(0, tt, FF_SUB_ROWS) for j in range(n_chunks)]
    xs = {r0: x_ref[r0:r0 + FF_SUB_ROWS, :] for r0 in range(0, tt, FF_SUB_ROWS)}
    xbs = {r0: xs[r0].astype(BF16) for r0 in xs}

    def up(item):
        r0, j = item
        return _ffn_up(xbs[r0], j, d_ff, wup_ref)

    ups = [up(item) for item in items[:FF_LOOKAHEAD]]
    acc = None
    for i, (r0, j) in enumerate(items):
        if i + FF_LOOKAHEAD < len(items):
            ups.append(up(items[i + FF_LOOKAHEAD]))
        halves = ups[i]
        ups[i] = None
        lo = SUBLANES + r0

        def conv_taps(hup, cols):
            ext_scr[lo:lo + FF_SUB_ROWS, cols] = hup
            return (ext_scr[lo - 2:lo - 2 + FF_SUB_ROWS, cols],
                    ext_scr[lo - 1:lo - 1 + FF_SUB_ROWS, cols])

        part = _ffn_gate_down(halves, j, d_ff, cw_ref, cb_ref, wdn_ref, conv_taps)
        acc = part if j == 0 else part + acc
        if r0 + FF_SUB_ROWS == tt:
            for base, hup in zip((0, d_ff), halves):
                cols = slice(base + j * FF_CHUNK, base + (j + 1) * FF_CHUNK)
                ext_scr[0:SUBLANES, cols] = hup[FF_SUB_ROWS - SUBLANES:, :]
                conv_ref[0, :, cols] = hup[FF_SUB_ROWS - SUBLANES:, :]
        if j == n_chunks - 1:
            o_ref[r0:r0 + FF_SUB_ROWS, :] = _layer_norm(ALPHA * xs[r0] + acc, lng_ref[...],
                                                       lnb_ref[...])


def _ffn_prompt(x2d, batch, w_up, conv_w, conv_b, w_down, ln_g, ln_b, layer):
    rows, d = x2d.shape
    nt = rows // batch // FF_TILE
    d_ff = w_down.shape[1]

    def lw(shape):
        return pl.BlockSpec((None,) + shape, lambda b, t: (layer, 0, 0),
                            pipeline_mode=pl.Buffered(1))

    row_spec = pl.BlockSpec((FF_TILE, d), lambda b, t: (b * nt + t, 0))
    return pl.pallas_call(
        _ffn_prompt_kernel,
        grid=(batch, nt),
        in_specs=[row_spec, lw((d, 2 * d_ff)), lw((CONV_WIDTH, 2 * d_ff)), lw((1, 2 * d_ff)),
                  lw((d_ff, d)), lw((1, d)), lw((1, d))],
        out_specs=[row_spec, pl.BlockSpec((1, SUBLANES, 2 * d_ff), lambda b, t: (b, 0, 0))],
        out_shape=[jax.ShapeDtypeStruct((rows, d), F32),
                   jax.ShapeDtypeStruct((batch, SUBLANES, 2 * d_ff), F32)],
        scratch_shapes=[pltpu.VMEM((SUBLANES + FF_TILE, 2 * d_ff), F32)],
        compiler_params=_params(2),
        name="ffn_prompt",
    )(x2d, w_up, conv_w, conv_b, w_down, ln_g, ln_b)


def _ffn_sample_kernel(x_ref, st_ref, wup_ref, cw_ref, cb_ref, wdn_ref, lng_ref, lnb_ref,
                       o_ref, stn_ref):
    d_ff = wdn_ref.shape[0]
    x = x_ref[...]
    xb = x.astype(BF16)

    def conv_taps(hup, cols):
        return st_ref[:, 0, cols], st_ref[:, 1, cols]

    acc = None
    for j in range(d_ff // FF_CHUNK):
        part, halves = _ffn_chunk(xb, j, d_ff, wup_ref, cw_ref, cb_ref, wdn_ref, conv_taps)
        for base, hup in zip((0, d_ff), halves):
            cols = slice(base + j * FF_CHUNK, base + (j + 1) * FF_CHUNK)
            stn_ref[:, 0, cols] = st_ref[:, 1, cols]
            stn_ref[:, 1, cols] = hup
        acc = part if acc is None else acc + part
    o_ref[...] = _layer_norm(ALPHA * x + acc, lng_ref[...], lnb_ref[...])


def _ffn_sample(x, conv_state, w_up, conv_w, conv_b, w_down, ln_g, ln_b, layer, tile):
    rows, d = x.shape
    d_ff = w_down.shape[1]
    st_shape = conv_state.shape[2:]

    def lw(shape):
        return pl.BlockSpec((None,) + shape, lambda i: (layer, 0, 0))

    return pl.pallas_call(
        _ffn_sample_kernel,
        grid=(rows // tile,),
        in_specs=[pl.BlockSpec((tile, d), lambda i: (i, 0)),
                  pl.BlockSpec((None, tile) + st_shape, lambda i: (layer, i, 0, 0)),
                  lw((d, 2 * d_ff)), lw((CONV_WIDTH, 2 * d_ff)), lw((1, 2 * d_ff)),
                  lw((d_ff, d)), lw((1, d)), lw((1, d))],
        out_specs=[pl.BlockSpec((tile, d), lambda i: (i, 0)),
                   pl.BlockSpec((tile,) + st_shape, lambda i: (i, 0, 0))],
        out_shape=[jax.ShapeDtypeStruct((rows, d), F32),
                   jax.ShapeDtypeStruct((rows,) + st_shape, F32)],
        compiler_params=_params(1),
        name="ffn_sample",
    )(x, conv_state, w_up, conv_w, conv_b, w_down, ln_g, ln_b)


def _mlstm_pre_kernel(q_ref, k_ref, g_ref, n_ref, m_ref,
                      kw_ref, dec_ref, s_ref, den_ref, nn_ref, mn_ref):
    gi = g_ref[:, 0:N_HEADS]
    lf = _log_sigmoid(g_ref[:, LANES:LANES + N_HEADS])
    m0 = m_ref[...]
    mt = jnp.maximum(m0 + lf, gi)
    inter_all = jnp.exp(m0 + lf - mt)
    dm_all = jnp.exp(gi - mt)
    emt_all = jnp.exp(-mt)
    mn_ref[...] = mt
    for h in range(N_HEADS):
        col = slice(h, h + 1)
        q = q_ref[:, _hs(h)]
        k = k_ref[:, _hs(h)] * (HEAD_DIM ** -0.5)
        n_old = n_ref[:, h, :]
        inter, dm = inter_all[:, col], dm_all[:, col]
        s = jnp.sum(q * k, axis=1, keepdims=True) * dm
        den = s + inter * jnp.sum(q * n_old, axis=1, keepdims=True)
        kw = k * dm
        kw_ref[:, _hs(h)] = kw
        dec_ref[:, _hs(h)] = jnp.broadcast_to(inter, kw.shape)
        s_ref[:, _hs(h)] = jnp.broadcast_to(s, kw.shape)
        den_ref[:, _hs(h)] = jnp.broadcast_to(jnp.maximum(jnp.abs(den), emt_all[:, col]), kw.shape)
        nn_ref[:, h, :] = inter * n_old + kw


def _mlstm_pre(proj_qkvo, gates, state_n, state_m, layer, tile):
    nseq = proj_qkvo.shape[0]
    d = N_HEADS * HEAD_DIM

    def rows_at(col):
        return pl.BlockSpec((tile, d), lambda i: (i, col))

    wide_sds = jax.ShapeDtypeStruct((nseq, d), F32)
    return pl.pallas_call(
        _mlstm_pre_kernel,
        grid=(nseq // tile,),
        in_specs=[rows_at(0), rows_at(1),
                  pl.BlockSpec((tile, 2 * LANES), lambda i: (i, 0)),
                  pl.BlockSpec((None, tile, N_HEADS, HEAD_DIM), lambda i: (layer, i, 0, 0)),
                  pl.BlockSpec((None, tile, N_HEADS), lambda i: (layer, i, 0))],
        out_specs=[rows_at(0)] * 4 + [pl.BlockSpec((tile, N_HEADS, HEAD_DIM), lambda i: (i, 0, 0)),
                                      pl.BlockSpec((tile, N_HEADS), lambda i: (i, 0))],
        out_shape=[wide_sds] * 4 + [jax.ShapeDtypeStruct((nseq, N_HEADS, HEAD_DIM), F32),
                                    jax.ShapeDtypeStruct((nseq, N_HEADS), F32)],
        compiler_params=_params(1),
        name="mlstm_pre",
    )(proj_qkvo, proj_qkvo, gates, state_n, state_m)


def _sample_memory_update(seq, i, q_ref, v_ref, kw_ref, dec_ref, c_ref, cn_ref, r_ref):
    mxu_rows = 2 * SUBLANES
    row0 = lax.broadcasted_iota(jnp.int32, (mxu_rows, HEAD_DIM), 0) == 0
    row = pl.ds(seq, 1)
    for h in range(N_HEADS):
        qrow = jnp.broadcast_to(q_ref[row, _hs(h)], (mxu_rows, HEAD_DIM)).astype(BF16)
        c_old = c_ref[i, h]
        r_ref[row, _hs(h)] = _dot(qrow, c_old.astype(BF16))[0:1, :]
        kw = jnp.where(row0, jnp.broadcast_to(kw_ref[row, _hs(h)], (mxu_rows, HEAD_DIM)), 0.0)
        vrow = jnp.broadcast_to(v_ref[row, _hs(h)], (mxu_rows, HEAD_DIM))
        cn_ref[i, h] = (dec_ref[row, _hs(h)] * c_old
                        + _dot_tn(kw.astype(BF16), vrow.astype(BF16)))


def _merge_step_kernel(x_ref, r_ref, s_ref, den_ref, dec_ref, v_ref, o_ref, mhg_ref, ox_ref,
                       u_ref, g0_ref, g1_ref, g2_ref, pool_ref,
                       wmo_ref, wpool_ref, pscale_ref, wxo_ref, wo_ref, lng_ref, lnb_ref,
                       x1_ref, pooln_ref):
    x = x_ref[...]
    u = u_ref[...]
    ht = (s_ref[...] * v_ref[...] + dec_ref[...] * r_ref[...]) / den_ref[...]
    hc = jax.nn.sigmoid(o_ref[...]) * ht
    hn = jnp.concatenate([_head_norm(hc[:, _hs(h)]) for h in range(N_HEADS)], axis=1)
    hn = hn * mhg_ref[...]
    for r in range(1, POOL_BUF):
        pooln_ref[:, r - 1, :] = pool_ref[:, r, :]
    pooln_ref[:, POOL_BUF - 1, :] = u
    yp_groups = []
    for g, window in enumerate(POOL_WINDOWS):
        cols = slice(g * POOL_GROUP_DIM, (g + 1) * POOL_GROUP_DIM)
        wsum = u[:, cols]
        for back in range(1, window):
            wsum = wsum + pool_ref[:, POOL_BUF - back, cols]
        cnt = float(min(window, PAST_LEN + 1))
        dlt = wsum / cnt - u[:, cols]
        yp_groups.append(_dot(dlt.astype(BF16), wpool_ref[g]))
    yp = jnp.concatenate(yp_groups, axis=1) * pscale_ref[...]
    ym = _dot(hn.astype(BF16), wmo_ref[...])
    yx = _dot(ox_ref[...].astype(BF16), wxo_ref[...])
    mix = (jax.nn.sigmoid(g0_ref[...]) * ym + jax.nn.sigmoid(g1_ref[...]) * yp
           + jax.nn.sigmoid(g2_ref[...]) * yx)
    y = ALPHA * x + _dot(mix.astype(BF16), wo_ref[...])
    x1_ref[...] = _layer_norm(y, lng_ref[...], lnb_ref[...])


def _merge_step(x, readout, s_rep, den_rep, dec_rep, proj_qkvo, mh_g, ox, proj_rest, pool_state,
                w_m_out, w_pool, pool_scale, w_x_out, w_o, ln_g, ln_b, layer, tile):
    rows, d = x.shape
    st_shape = pool_state.shape[2:]

    def lw(shape):
        return pl.BlockSpec((None,) + shape, lambda i: (layer,) + (0,) * len(shape))

    def rows_at(col):
        return pl.BlockSpec((tile, d), lambda i: (i, col))

    return pl.pallas_call(
        _merge_step_kernel,
        grid=(rows // tile,),
        in_specs=[rows_at(0), rows_at(0), rows_at(0), rows_at(0), rows_at(0),
                  rows_at(2), rows_at(3), lw((1, d)), rows_at(0),
                  rows_at(REST_U), rows_at(REST_G0), rows_at(REST_G1), rows_at(REST_G2),
                  pl.BlockSpec((None, tile) + st_shape, lambda i: (layer, i, 0, 0)),
                  lw((d, d)), lw((len(POOL_WINDOWS), POOL_GROUP_DIM, POOL_GROUP_DIM)),
                  lw((1, d)), lw((d, d)), lw((d, d)), lw((1, d)), lw((1, d))],
        out_specs=[rows_at(0), pl.BlockSpec((tile,) + st_shape, lambda i: (i, 0, 0))],
        out_shape=[jax.ShapeDtypeStruct((rows, d), F32),
                   jax.ShapeDtypeStruct((rows,) + st_shape, F32)],
        compiler_params=_params(1),
        name="merge_step",
    )(x, readout, s_rep, den_rep, dec_rep, proj_qkvo, proj_qkvo, mh_g, ox,
      proj_rest, proj_rest, proj_rest, proj_rest, pool_state, w_m_out, w_pool,
      pool_scale, w_x_out, w_o, ln_g, ln_b)


def kernel(x_prompt, mem_prompt, x_sample, cache_mem_k, cache_mem_v, state_C, state_n, state_m,
           state_pool, state_conv, ln_in_g, ln_in_b, w_in, b_in, mh_g, w_m_out, w_pool,
           pool_scale, w_mem_kv, w_x_out, w_o, ln1_g, ln1_b, w_up, conv_w, conv_b, w_down,
           ln2_g, ln2_b):
    batch, seq, d = x_prompt.shape
    nseq = x_sample.shape[0]
    n_mem = mem_prompt.shape[1]
    depth = w_in.shape[0]
    d_ff = w_down.shape[1]
    wide = N_HEADS * d
    gate_off = wide
    rest_off = wide + 2 * N_HEADS
    assert d == N_HEADS * HEAD_DIM and d_ff % FF_CHUNK == 0
    assert all(seq % tile == 0 for tile in (MLSTM_TILE, MIX_TILE, FF_TILE))
    assert depth == DEPTH and x_sample.shape[1] == 1

    def row(v):
        return v.reshape(depth, 1, v.shape[-1])

    w_qkvo, w_rest, w_if = _split_in_proj(w_in, wide, rest_off)
    b_qkvo, b_rest = row(b_in[:, :wide]), row(b_in[:, rest_off:])
    lane_pad = ((0, 0), (0, 0), (0, LANES - N_HEADS))
    b_if = row(jnp.concatenate(
        [jnp.pad(b_in[:, gate_off:gate_off + N_HEADS], lane_pad[1:]),
         jnp.pad(b_in[:, gate_off + N_HEADS:rest_off], lane_pad[1:])], axis=1))
    w_m_out_b, w_pool_b, w_x_out_b, w_o_b = (w.astype(BF16) for w in (w_m_out, w_pool, w_x_out, w_o))
    w_up_b, w_down_b = w_up.astype(BF16), w_down.astype(BF16)
    w_kv_b = w_mem_kv.astype(BF16)
    mh_g_r, pool_scale_r, conv_b_r = row(mh_g), row(pool_scale), row(conv_b)
    ln1_g_r, ln1_b_r, ln2_g_r, ln2_b_r = row(ln1_g), row(ln1_b), row(ln2_g), row(ln2_b)

    xp = x_prompt.reshape(batch * seq, d)
    xs = x_sample.reshape(nseq, d)
    mem2d = mem_prompt.reshape(batch * n_mem, d)

    xp = _layer_norm_rows(xp, ln_in_g.reshape(1, d), ln_in_b.reshape(1, d), 2 * TIME_TILE)
    xs = _layer_norm_rows(xs, ln_in_g.reshape(1, d), ln_in_b.reshape(1, d), nseq)

    outs = {name: [] for name in ("pC", "pn", "pm", "ppool", "pconv", "sn", "sm", "spool", "sconv")}
    p_mem = (jnp.zeros((depth, batch, n_mem, N_HEADS, HEAD_DIM), F32),) * 2
    s_c = jnp.zeros(state_C.shape, F32)
    for l in range(depth):
        proj_qkvo = _matmul_bias(xs, w_qkvo, b_qkvo, l, nseq, d, "proj_sample")
        proj_rest = _matmul_bias(xs, w_rest, b_rest, l, nseq, d, "proj_sample")
        gates = _matmul_bias(xs, w_if, b_if, l, nseq, 2 * LANES, "gates_sample")
        kw_s, dec_s, s_rep, den_rep, s_n, s_m = _mlstm_pre(proj_qkvo, gates, state_n, state_m,
                                                          l, nseq // 2)

        p_mem, (mk_b, mv_b) = _mem_kv(mem2d, w_kv_b, p_mem, l, batch)
        ymg, p_c, p_n, p_m, ox = _mlstm_prompt(xp, batch, w_qkvo, b_qkvo, w_if, b_if, w_rest,
                                               b_rest, mh_g_r, w_m_out_b, proj_rest,
                                               cache_mem_k, cache_mem_v, l)
        x1, p_pool, readout, s_c = _mix_prompt(xp, ymg, mk_b, mv_b, batch, w_rest, b_rest,
                                               w_pool_b, pool_scale_r, w_x_out_b, w_o_b, ln1_g_r,
                                               ln1_b_r, proj_qkvo, kw_s, dec_s, state_C, s_c, l)
        xp, p_conv = _ffn_prompt(x1, batch, w_up_b, conv_w, conv_b_r, w_down_b, ln2_g_r,
                                 ln2_b_r, l)
        outs["pC"].append(p_c)
        outs["pn"].append(p_n)
        outs["pm"].append(p_m[:, 0, :N_HEADS])
        outs["ppool"].append(p_pool[:, 2 * SUBLANES - POOL_BUF:, :])
        outs["pconv"].append(p_conv[:, SUBLANES - (CONV_WIDTH - 1):, :])

        x1s, s_pool = _merge_step(xs, readout, s_rep, den_rep, dec_s, proj_qkvo, mh_g_r, ox,
                                  proj_rest, state_pool, w_m_out_b, w_pool_b, pool_scale_r,
                                  w_x_out_b, w_o_b, ln1_g_r, ln1_b_r, l, nseq // 2)
        xs, s_conv = _ffn_sample(x1s, state_conv, w_up_b, conv_w, conv_b_r, w_down_b, ln2_g_r,
                                 ln2_b_r, l, nseq // 2)
        outs["sn"].append(s_n)
        outs["sm"].append(s_m)
        outs["spool"].append(s_pool)
        outs["sconv"].append(s_conv)

    stacked = {k: jnp.stack(v) for k, v in outs.items()}
    return (xp.reshape(batch, seq, d), xs.reshape(nseq, 1, d),
            stacked["pC"], stacked["pn"], stacked["pm"], stacked["ppool"], stacked["pconv"],
            p_mem[0], p_mem[1],
            s_c, stacked["sn"], stacked["sm"], stacked["spool"], stacked["sconv"])
```

```python
import jax
import jax.numpy as jnp
from jax import lax
from jax.experimental import pallas as pl
from jax.experimental.pallas import tpu as pltpu

F32 = jnp.float32
BF16 = jnp.bfloat16

N_HEADS = 4
HEAD_DIM = 256
POOL_WINDOWS = (2, 4, 8, 16)
POOL_GROUP_DIM = 256
POOL_BUF = 15
CONV_WIDTH = 3
DEPTH = 4
PAST_LEN = 16384
ALPHA = (2.0 * DEPTH) ** 0.25
LN_EPS = 1e-5
NEG = -1e30

SUBLANES = 8
LANES = 128
TIME_TILE = 256
MLSTM_TILE = 512
FF_CHUNK = 256
MIX_TILE = 256
FF_TILE = 512
FF_SUB_ROWS = 256
FF_LOOKAHEAD = 3
VMEM_LIMIT = 56 * 1024 * 1024

REST_U, REST_XQ, REST_G0, REST_G1, REST_G2 = range(5)


def _params(n_axes):
    return pltpu.CompilerParams(
        dimension_semantics=("arbitrary",) * n_axes, vmem_limit_bytes=VMEM_LIMIT)


def _dot(a, b):
    return jnp.dot(a, b, preferred_element_type=F32)


def _dot_nt(a, b):
    return lax.dot_general(a, b, (((1,), (1,)), ((), ())), preferred_element_type=F32)


def _dot_tn(a, b):
    return lax.dot_general(a, b, (((0,), (0,)), ((), ())), preferred_element_type=F32)


def _layer_norm(x, g, b):
    mu = jnp.mean(x, axis=-1, keepdims=True)
    xc = x - mu
    var = jnp.mean(xc * xc, axis=-1, keepdims=True)
    return xc * lax.rsqrt(var + LN_EPS) * g + b


def _head_norm(x):
    mu = jnp.mean(x, axis=-1, keepdims=True)
    xc = x - mu
    var = jnp.mean(xc * xc, axis=-1, keepdims=True)
    return xc * lax.rsqrt(var + LN_EPS)


def _log_sigmoid(x):
    return jnp.minimum(x, 0.0) - jnp.log1p(jnp.exp(-jnp.abs(x)))


def _scan_rows(x, op):
    n = x.shape[0]
    rows = lax.broadcasted_iota(jnp.int32, x.shape, 0)
    shift = 1
    while shift < n:
        x = jnp.where(rows >= shift, op(x, pltpu.roll(x, shift, 0)), x)
        shift *= 2
    return x


def _hs(h):
    return slice(h * HEAD_DIM, (h + 1) * HEAD_DIM)


def _ln_kernel(x_ref, g_ref, b_ref, o_ref):
    o_ref[...] = _layer_norm(x_ref[...], g_ref[...], b_ref[...])


def _layer_norm_rows(x, g, b, tile):
    rows, d = x.shape
    return pl.pallas_call(
        _ln_kernel,
        grid=(rows // tile,),
        in_specs=[pl.BlockSpec((tile, d), lambda i: (i, 0)),
                  pl.BlockSpec((1, d), lambda i: (0, 0)),
                  pl.BlockSpec((1, d), lambda i: (0, 0))],
        out_specs=pl.BlockSpec((tile, d), lambda i: (i, 0)),
        out_shape=jax.ShapeDtypeStruct((rows, d), F32),
        compiler_params=_params(1),
        name="ln_rows",
    )(x, g, b)


def _mm_kernel(x_ref, w_ref, b_ref, o_ref):
    o_ref[...] = _dot(x_ref[...].astype(BF16), w_ref[...]) + b_ref[...]


def _matmul_bias(x, w_all, b_all, layer, tm, tn, name):
    rows, kdim = x.shape
    n = w_all.shape[-1]
    return pl.pallas_call(
        _mm_kernel,
        grid=(n // tn, rows // tm),
        in_specs=[pl.BlockSpec((tm, kdim), lambda j, i: (i, 0)),
                  pl.BlockSpec((None, kdim, tn), lambda j, i: (layer, 0, j)),
                  pl.BlockSpec((None, 1, tn), lambda j, i: (layer, 0, j))],
        out_specs=pl.BlockSpec((tm, tn), lambda j, i: (i, j)),
        out_shape=jax.ShapeDtypeStruct((rows, n), F32),
        compiler_params=_params(2),
        name=name,
    )(x, w_all, b_all)


def _mem_kv_kernel(x_ref, w_ref, k_ref, v_ref, kb_ref, vb_ref):
    d = N_HEADS * HEAD_DIM
    res = _dot(x_ref[...].astype(BF16), w_ref[...])
    for h in range(N_HEADS):
        k_ref[:, h, :] = res[:, _hs(h)]
        v_ref[:, h, :] = res[:, d + h * HEAD_DIM:d + (h + 1) * HEAD_DIM]
    kb_ref[...] = res[:, :d].astype(BF16)
    vb_ref[...] = res[:, d:].astype(BF16)


def _mem_kv(mem2d, w_kv, batch):
    rows, d = mem2d.shape
    n_mem = rows // batch
    depth = w_kv.shape[0]
    out_spec = pl.BlockSpec((None, None, n_mem, N_HEADS, HEAD_DIM), lambda l, b: (l, b, 0, 0, 0))
    out_sds = jax.ShapeDtypeStruct((depth, batch, n_mem, N_HEADS, HEAD_DIM), F32)
    dense_spec = pl.BlockSpec((None, n_mem, d), lambda l, b: (l, b, 0))
    dense_sds = jax.ShapeDtypeStruct((depth, rows, d), BF16)
    return pl.pallas_call(
        _mem_kv_kernel,
        grid=(depth, batch),
        in_specs=[pl.BlockSpec((n_mem, d), lambda l, b: (b, 0)),
                  pl.BlockSpec((None, d, 2 * d), lambda l, b: (l, 0, 0))],
        out_specs=[out_spec, out_spec, dense_spec, dense_spec],
        out_shape=[out_sds, out_sds, dense_sds, dense_sds],
        compiler_params=_params(2),
        name="mem_kv",
    )(mem2d, w_kv)


def _sample_attn_scores(q_row, keys):
    sub = lax.broadcasted_iota(jnp.int32, (SUBLANES, HEAD_DIM), 0)
    q = jnp.zeros((SUBLANES, HEAD_DIM), F32)
    for h in range(N_HEADS):
        q = jnp.where(sub == h, jnp.broadcast_to(q_row(h), (SUBLANES, HEAD_DIM)), q)
    keys2d = keys.reshape(keys.shape[0] * N_HEADS, HEAD_DIM).astype(BF16)
    return _dot_nt(q.astype(BF16), keys2d)


def _sample_attn_softmax(s):
    s = s * (HEAD_DIM ** -0.5)
    head = lax.broadcasted_iota(jnp.int32, s.shape, 0) % SUBLANES
    col_head = lax.broadcasted_iota(jnp.int32, s.shape, 1) % N_HEADS
    s = jnp.where((col_head == head) | (head >= N_HEADS), s, -jnp.inf)
    p = jnp.exp(s - jnp.max(s, axis=-1, keepdims=True))
    return p / jnp.sum(p, axis=-1, keepdims=True)


def _sample_attn_output(p, vals):
    vals2d = vals.reshape(vals.shape[0] * N_HEADS, HEAD_DIM).astype(BF16)
    return _dot(p.astype(BF16), vals2d)


def _mlstm_prompt_kernel(x_ref, wqkvo_ref, bqkvo_ref, wif_ref, bif_ref, wg0_ref, bg0_ref,
                         mhg_ref, wmo_ref, xq_ref, sk_ref, sv_ref,
                         y_ref, c_ref, n_ref, m_ref, ox_ref, s_scr):
    t = pl.program_id(1)
    d = N_HEADS * HEAD_DIM
    ride = sk_ref.shape[0]
    seq0 = (pl.program_id(0) * pl.num_programs(1) + t) * ride

    @pl.when(t == 0)
    def _():
        c_ref[...] = jnp.zeros(c_ref.shape, F32)
        n_ref[...] = jnp.zeros(n_ref.shape, F32)
        m_ref[...] = jnp.zeros(m_ref.shape, F32)

    for i in range(ride):
        s_scr[i * SUBLANES:(i + 1) * SUBLANES, :] = _sample_attn_scores(
            lambda h, i=i: xq_ref[pl.ds(seq0 + i, 1), _hs(h)], sk_ref[i])

    lc = TIME_TILE
    causal = (lax.broadcasted_iota(jnp.int32, (lc, lc), 1)
              <= lax.broadcasted_iota(jnp.int32, (lc, lc), 0))

    for chunk, r0 in enumerate(range(0, x_ref.shape[0], lc)):
        rows = slice(r0, r0 + lc)
        xb = x_ref[rows, :].astype(BF16)
        gates = _dot(xb, wif_ref[...]) + bif_ref[...]
        gi, gf = gates[:, 0:LANES], gates[:, LANES:2 * LANES]
        lf = _log_sigmoid(gf)
        bcum = _scan_rows(lf, jnp.add)
        a = gi - bcum
        m0 = m_ref[0]
        mt = bcum + jnp.maximum(m0, _scan_rows(a, jnp.maximum))
        inter = jnp.exp(m0 + bcum - mt)
        bm = bcum - mt
        emt = jnp.exp(-mt)
        b_last = bcum[lc - 1:lc, :]
        m_last = mt[lc - 1:lc, :]
        ws = jnp.exp(a + b_last - m_last)
        decay = jnp.exp(m0 + b_last - m_last)
        a_t = a.T

        def head_proj(h, xb=xb):
            def proj(part):
                cols = slice(part * d + h * HEAD_DIM, part * d + (h + 1) * HEAD_DIM)
                return _dot(xb, wqkvo_ref[:, cols]) + bqkvo_ref[:, cols]
            return proj(0), proj(1) * (HEAD_DIM ** -0.5), proj(2), proj(3)

        hn_heads = []
        ahead = head_proj(0)
        for h in range(N_HEADS):
            q, k, v, o = ahead
            if h + 1 < N_HEADS:
                ahead = head_proj(h + 1)
            else:
                g0 = _dot(xb, wg0_ref[...]) + bg0_ref[...]
            qb = q.astype(BF16)
            vb = v.astype(BF16)
            col = slice(h, h + 1)
            logd = a_t[h:h + 1, :] + bm[:, col]
            dmat = jnp.exp(jnp.where(causal, logd, NEG))
            s = _dot_nt(qb, k.astype(BF16)) * dmat
            c_old = c_ref[0, h]
            n_old = n_ref[0, h:h + 1, :]
            inter_h = inter[:, col]
            num = _dot(s.astype(BF16), vb) + inter_h * _dot(qb, c_old.astype(BF16))
            den = (jnp.sum(s, axis=1, keepdims=True)
                   + inter_h * jnp.sum(q * n_old, axis=1, keepdims=True))
            ht = num / jnp.maximum(jnp.abs(den), emt[:, col])
            kw = k * ws[:, col]
            c_ref[0, h] = decay[:, col] * c_old + _dot_tn(kw.astype(BF16), vb)
            n_ref[0, h:h + 1, :] = decay[:, col] * n_old + jnp.sum(kw, axis=0, keepdims=True)
            hn = _head_norm(jax.nn.sigmoid(o) * ht) * mhg_ref[:, _hs(h)]
            hn_heads.append(hn.astype(BF16))
            if chunk == 0 and h == 0:
                s_scr[...] = _sample_attn_softmax(s_scr[...])
        m_ref[0] = m_last

        if chunk == 0:
            for i in range(ride):
                out = _sample_attn_output(s_scr[i * SUBLANES:(i + 1) * SUBLANES, :], sv_ref[i])
                for h in range(N_HEADS):
                    ox_ref[pl.ds(seq0 + i, 1), _hs(h)] = out[h:h + 1, :]

        ym = _dot(jnp.concatenate(hn_heads, axis=1), wmo_ref[...])
        y_ref[rows, :] = jax.nn.sigmoid(g0) * ym


def _mlstm_prompt(x2d, batch, w_qkvo, b_qkvo, w_if, b_if, w_rest, b_rest, mh_g, w_m_out,
                  proj_rest_s, mem_k, mem_v, layer):
    rows, d = x2d.shape
    nt = rows // batch // MLSTM_TILE
    wide = N_HEADS * d
    nseq, n_mem = mem_k.shape[1], mem_k.shape[2]
    ride = nseq // (batch * nt)
    assert ride * batch * nt == nseq

    def lw(shape, col):
        return pl.BlockSpec((None,) + shape, lambda b, t: (layer, 0, col),
                            pipeline_mode=pl.Buffered(1))

    kv_spec = pl.BlockSpec((None, ride, n_mem, N_HEADS, HEAD_DIM),
                           lambda b, t: (layer, b * nt + t, 0, 0, 0))
    return pl.pallas_call(
        _mlstm_prompt_kernel,
        grid=(batch, nt),
        in_specs=[pl.BlockSpec((MLSTM_TILE, d), lambda b, t: (b * nt + t, 0)),
                  lw((d, wide), 0), lw((1, wide), 0),
                  lw((d, 2 * LANES), 0), lw((1, 2 * LANES), 0),
                  lw((d, d), REST_G0), lw((1, d), REST_G0),
                  lw((1, d), 0), lw((d, d), 0),
                  pl.BlockSpec((nseq, d), lambda b, t: (0, REST_XQ)), kv_spec, kv_spec],
        out_specs=[pl.BlockSpec((MLSTM_TILE, d), lambda b, t: (b * nt + t, 0)),
                   pl.BlockSpec((1, N_HEADS, HEAD_DIM, HEAD_DIM), lambda b, t: (b, 0, 0, 0)),
                   pl.BlockSpec((1, N_HEADS, HEAD_DIM), lambda b, t: (b, 0, 0)),
                   pl.BlockSpec((1, 1, LANES), lambda b, t: (b, 0, 0)),
                   pl.BlockSpec((nseq, d), lambda b, t: (0, 0))],
        out_shape=[jax.ShapeDtypeStruct((rows, d), F32),
                   jax.ShapeDtypeStruct((batch, N_HEADS, HEAD_DIM, HEAD_DIM), F32),
                   jax.ShapeDtypeStruct((batch, N_HEADS, HEAD_DIM), F32),
                   jax.ShapeDtypeStruct((batch, 1, LANES), F32),
                   jax.ShapeDtypeStruct((nseq, d), F32)],
        scratch_shapes=[pltpu.VMEM((ride * SUBLANES, N_HEADS * n_mem), F32)],
        compiler_params=_params(2),
        name="mlstm_prompt",
    )(x2d, w_qkvo, b_qkvo, w_if, b_if, w_rest, b_rest, mh_g, w_m_out, proj_rest_s, mem_k, mem_v)


def _pool_window_sums(ext, window):
    s, span = ext, 1
    while span < window:
        s = s + pltpu.roll(s, span, 0)
        span *= 2
    return s


def _mix_prompt_kernel(x_ref, ymg_ref, mk_ref, mv_ref, wu_ref, bu_ref, wxq_ref, bxq_ref,
                       wg1_ref, bg1_ref, wg2_ref, bg2_ref, wpool_ref, pscale_ref, wxo_ref,
                       wo_ref, lng_ref, lnb_ref, sq_ref, sv_ref, skw_ref, sdec_ref, sc_ref,
                       *refs):
    x1_ref, pool_ref, sr_ref, scn_ref, ext_scr = refs[-5:]
    t = pl.program_id(1)
    tt = x_ref.shape[0]
    carry = 2 * SUBLANES
    ride = sc_ref.shape[0]
    seq0 = (pl.program_id(0) * pl.num_programs(1) + t) * ride

    @pl.when(t == 0)
    def _():
        ext_scr[0:carry, :] = jnp.zeros((carry, ext_scr.shape[1]), F32)

    for sub, r0 in enumerate(range(0, tt, TIME_TILE)):
        rows = slice(r0, r0 + TIME_TILE)
        for i in range(sub * ride // (tt // TIME_TILE), (sub + 1) * ride // (tt // TIME_TILE)):
            _sample_memory_update(seq0 + i, i, sq_ref, sv_ref, skw_ref, sdec_ref, sc_ref,
                                  scn_ref, sr_ref)

        x = x_ref[rows, :]
        xb = x.astype(BF16)

        u = _dot(xb, wu_ref[...]) + bu_ref[...]
        xq = (_dot(xb, wxq_ref[...]) + bxq_ref[...]).astype(BF16)

        ext_scr[carry + r0:carry + r0 + TIME_TILE, :] = u
        pos = t * tt + r0 + lax.broadcasted_iota(jnp.int32, (TIME_TILE, 1), 0)
        scores = [_dot_nt(xq[:, _hs(h)], mk_ref[:, _hs(h)]) for h in range(N_HEADS)]
        yp_groups = []
        for g, window in enumerate(POOL_WINDOWS):
            cols = slice(g * POOL_GROUP_DIM, (g + 1) * POOL_GROUP_DIM)
            ext = ext_scr[r0:r0 + carry + TIME_TILE, cols]
            wsum = _pool_window_sums(ext, window)[carry:, :]
            cnt = jnp.minimum(window, pos + 1).astype(F32)
            dlt = wsum / cnt - u[:, cols]
            yp_groups.append(_dot(dlt.astype(BF16), wpool_ref[g]))
        if r0 + TIME_TILE == tt:
            ext_scr[0:carry, :] = u[TIME_TILE - carry:, :]
            pool_ref[0] = u[TIME_TILE - carry:, :]
        g1 = _dot(xb, wg1_ref[...]) + bg1_ref[...]

        heads = []
        for h in range(N_HEADS):
            s = scores[h] * (HEAD_DIM ** -0.5)
            p = jnp.exp(s - jnp.max(s, axis=-1, keepdims=True))
            p = p / jnp.sum(p, axis=-1, keepdims=True)
            heads.append(_dot(p.astype(BF16), mv_ref[:, _hs(h)]).astype(BF16))
        g2 = _dot(xb, wg2_ref[...]) + bg2_ref[...]
        yp = jnp.concatenate(yp_groups, axis=1) * pscale_ref[...]
        yx = _dot(jnp.concatenate(heads, axis=1), wxo_ref[...])
        mix = ymg_ref[rows, :] + jax.nn.sigmoid(g1) * yp + jax.nn.sigmoid(g2) * yx
        y = ALPHA * x + _dot(mix.astype(BF16), wo_ref[...])
        x1_ref[rows, :] = _layer_norm(y, lng_ref[...], lnb_ref[...])


def _mix_prompt(x2d, ymg, mk, mv, batch, w_rest, b_rest, w_pool, pool_scale, w_x_out, w_o,
                ln_g, ln_b, proj_qkvo_s, kw_s, dec_s, state_c, prev_c, layer):
    rows, d = x2d.shape
    nt = rows // batch // MIX_TILE
    n_mem = mk.shape[1] // batch
    carry = 2 * SUBLANES
    depth, nseq = state_c.shape[:2]
    ride = nseq // (batch * nt)
    assert ride * batch * nt == nseq and ride % (MIX_TILE // TIME_TILE) == 0

    def lw(shape, col):
        return pl.BlockSpec((None,) + shape, lambda b, t: (layer, 0, col),
                            pipeline_mode=pl.Buffered(1))

    def sample_rows(col):
        return pl.BlockSpec((nseq, d), lambda b, t: (0, col))

    row_spec = pl.BlockSpec((MIX_TILE, d), lambda b, t: (b * nt + t, 0))
    mem_spec = pl.BlockSpec((None, n_mem, d), lambda b, t: (layer, b, 0))
    c_block = (ride, N_HEADS, HEAD_DIM, HEAD_DIM)
    c_spec = pl.BlockSpec((None,) + c_block, lambda b, t: (layer, b * nt + t, 0, 0, 0))
    if prev_c is None:
        c_out_spec = pl.BlockSpec((depth,) + c_block, lambda b, t: (0, b * nt + t, 0, 0, 0))
        chain_specs, chain_args, aliases = [], [], {}
    else:
        c_out_spec = c_spec
        chain_specs, chain_args, aliases = [pl.BlockSpec(memory_space=pl.ANY)], [prev_c], {23: 3}
    return pl.pallas_call(
        _mix_prompt_kernel,
        grid=(batch, nt),
        in_specs=[row_spec, row_spec, mem_spec, mem_spec,
                  lw((d, d), REST_U), lw((1, d), REST_U),
                  lw((d, d), REST_XQ), lw((1, d), REST_XQ),
                  lw((d, d), REST_G1), lw((1, d), REST_G1),
                  lw((d, d), REST_G2), lw((1, d), REST_G2),
                  pl.BlockSpec((None, len(POOL_WINDOWS), POOL_GROUP_DIM, POOL_GROUP_DIM),
                               lambda b, t: (layer, 0, 0, 0)),
                  lw((1, d), 0), lw((d, d), 0), lw((d, d), 0), lw((1, d), 0), lw((1, d), 0),
                  sample_rows(0), sample_rows(2), sample_rows(0), sample_rows(0), c_spec]
        + chain_specs,
        out_specs=[row_spec, pl.BlockSpec((1, carry, d), lambda b, t: (b, 0, 0)),
                   sample_rows(0), c_out_spec],
        out_shape=[jax.ShapeDtypeStruct((rows, d), F32),
                   jax.ShapeDtypeStruct((batch, carry, d), F32),
                   jax.ShapeDtypeStruct((nseq, d), F32),
                   jax.ShapeDtypeStruct(state_c.shape, F32)],
        input_output_aliases=aliases,
        scratch_shapes=[pltpu.VMEM((carry + MIX_TILE, d), F32)],
        compiler_params=_params(2),
        name="mix_prompt",
    )(x2d, ymg, mk, mv, w_rest, b_rest, w_rest, b_rest, w_rest, b_rest, w_rest, b_rest,
      w_pool, pool_scale, w_x_out, w_o, ln_g, ln_b,
      proj_qkvo_s, proj_qkvo_s, kw_s, dec_s, state_c, *chain_args)


def _ffn_chunk(xb, j, d_ff, wup_ref, cw_ref, cb_ref, wdn_ref, conv_taps):
    halves = _ffn_up(xb, j, d_ff, wup_ref)
    return _ffn_gate_down(halves, j, d_ff, cw_ref, cb_ref, wdn_ref, conv_taps), halves


def _ffn_up(xb, j, d_ff, wup_ref):
    return [_dot(xb, wup_ref[:, base + j * FF_CHUNK:base + (j + 1) * FF_CHUNK])
            for base in (0, d_ff)]


def _ffn_gate_down(halves, j, d_ff, cw_ref, cb_ref, wdn_ref, conv_taps):
    conv = []
    for base, hup in zip((0, d_ff), halves):
        cols = slice(base + j * FF_CHUNK, base + (j + 1) * FF_CHUNK)
        prev2, prev1 = conv_taps(hup, cols)
        conv.append(prev2 * cw_ref[0:1, cols] + prev1 * cw_ref[1:2, cols]
                    + hup * cw_ref[2:3, cols] + cb_ref[:, cols])
    act = jax.nn.gelu(conv[0]) * conv[1]
    return _dot(act.astype(BF16), wdn_ref[j * FF_CHUNK:(j + 1) * FF_CHUNK, :])


def _ffn_prompt_kernel(x_ref, wup_ref, cw_ref, cb_ref, wdn_ref, lng_ref, lnb_ref,
                       o_ref, conv_ref, ext_scr):
    t = pl.program_id(1)
    tt = x_ref.shape[0]
    d_ff = wdn_ref.shape[0]
    n_chunks = d_ff // FF_CHUNK

    @pl.when(t == 0)
    def _():
        ext_scr[0:SUBLANES, :] = jnp.zeros((SUBLANES, ext_scr.shape[1]), F32)

    items = [(r0, j) for r0 in range(0, tt, FF_SUB_ROWS) for j in range(n_chunks)]
    xs = {r0: x_ref[r0:r0 + FF_SUB_ROWS, :] for r0 in range(0, tt, FF_SUB_ROWS)}
    xbs = {r0: xs[r0].astype(BF16) for r0 in xs}

    def up(item):
        r0, j = item
        return _ffn_up(xbs[r0], j, d_ff, wup_ref)

    ups = [up(item) for item in items[:FF_LOOKAHEAD]]
    acc = None
    for i, (r0, j) in enumerate(items):
        if i + FF_LOOKAHEAD < len(items):
            ups.append(up(items[i + FF_LOOKAHEAD]))
        halves = ups[i]
        ups[i] = None
        lo = SUBLANES + r0

        def conv_taps(hup, cols):
            ext_scr[lo:lo + FF_SUB_ROWS, cols] = hup
            return (ext_scr[lo - 2:lo - 2 + FF_SUB_ROWS, cols],
                    ext_scr[lo - 1:lo - 1 + FF_SUB_ROWS, cols])

        part = _ffn_gate_down(halves, j, d_ff, cw_ref, cb_ref, wdn_ref, conv_taps)
        acc = part if j == 0 else part + acc
        if r0 + FF_SUB_ROWS == tt:
            for base, hup in zip((0, d_ff), halves):
                cols = slice(base + j * FF_CHUNK, base + (j + 1) * FF_CHUNK)
                ext_scr[0:SUBLANES, cols] = hup[FF_SUB_ROWS - SUBLANES:, :]
                conv_ref[0, :, cols] = hup[FF_SUB_ROWS - SUBLANES:, :]
        if j == n_chunks - 1:
            o_ref[r0:r0 + FF_SUB_ROWS, :] = _layer_norm(ALPHA * xs[r0] + acc, lng_ref[...],
                                                       lnb_ref[...])


def _ffn_prompt(x2d, batch, w_up, conv_w, conv_b, w_down, ln_g, ln_b, layer):
    rows, d = x2d.shape
    nt = rows // batch // FF_TILE
    d_ff = w_down.shape[1]

    def lw(shape):
        return pl.BlockSpec((None,) + shape, lambda b, t: (layer, 0, 0),
                            pipeline_mode=pl.Buffered(1))

    row_spec = pl.BlockSpec((FF_TILE, d), lambda b, t: (b * nt + t, 0))
    return pl.pallas_call(
        _ffn_prompt_kernel,
        grid=(batch, nt),
        in_specs=[row_spec, lw((d, 2 * d_ff)), lw((CONV_WIDTH, 2 * d_ff)), lw((1, 2 * d_ff)),
                  lw((d_ff, d)), lw((1, d)), lw((1, d))],
        out_specs=[row_spec, pl.BlockSpec((1, SUBLANES, 2 * d_ff), lambda b, t: (b, 0, 0))],
        out_shape=[jax.ShapeDtypeStruct((rows, d), F32),
                   jax.ShapeDtypeStruct((batch, SUBLANES, 2 * d_ff), F32)],
        scratch_shapes=[pltpu.VMEM((SUBLANES + FF_TILE, 2 * d_ff), F32)],
        compiler_params=_params(2),
        name="ffn_prompt",
    )(x2d, w_up, conv_w, conv_b, w_down, ln_g, ln_b)


def _ffn_sample_kernel(x_ref, st_ref, wup_ref, cw_ref, cb_ref, wdn_ref, lng_ref, lnb_ref,
                       o_ref, stn_ref):
    d_ff = wdn_ref.shape[0]
    x = x_ref[...]
    xb = x.astype(BF16)

    def conv_taps(hup, cols):
        return st_ref[:, 0, cols], st_ref[:, 1, cols]

    acc = None
    for j in range(d_ff // FF_CHUNK):
        part, halves = _ffn_chunk(xb, j, d_ff, wup_ref, cw_ref, cb_ref, wdn_ref, conv_taps)
        for base, hup in zip((0, d_ff), halves):
            cols = slice(base + j * FF_CHUNK, base + (j + 1) * FF_CHUNK)
            stn_ref[:, 0, cols] = st_ref[:, 1, cols]
            stn_ref[:, 1, cols] = hup
        acc = part if acc is None else acc + part
    o_ref[...] = _layer_norm(ALPHA * x + acc, lng_ref[...], lnb_ref[...])


def _ffn_sample(x, conv_state, w_up, conv_w, conv_b, w_down, ln_g, ln_b, layer, tile):
    rows, d = x.shape
    d_ff = w_down.shape[1]
    st_shape = conv_state.shape[2:]

    def lw(shape):
        return pl.BlockSpec((None,) + shape, lambda i: (layer, 0, 0))

    return pl.pallas_call(
        _ffn_sample_kernel,
        grid=(rows // tile,),
        in_specs=[pl.BlockSpec((tile, d), lambda i: (i, 0)),
                  pl.BlockSpec((None, tile) + st_shape, lambda i: (layer, i, 0, 0)),
                  lw((d, 2 * d_ff)), lw((CONV_WIDTH, 2 * d_ff)), lw((1, 2 * d_ff)),
                  lw((d_ff, d)), lw((1, d)), lw((1, d))],
        out_specs=[pl.BlockSpec((tile, d), lambda i: (i, 0)),
                   pl.BlockSpec((tile,) + st_shape, lambda i: (i, 0, 0))],
        out_shape=[jax.ShapeDtypeStruct((rows, d), F32),
                   jax.ShapeDtypeStruct((rows,) + st_shape, F32)],
        compiler_params=_params(1),
        name="ffn_sample",
    )(x, conv_state, w_up, conv_w, conv_b, w_down, ln_g, ln_b)


def _mlstm_pre_kernel(q_ref, k_ref, g_ref, n_ref, m_ref,
                      kw_ref, dec_ref, s_ref, den_ref, nn_ref, mn_ref):
    gi = g_ref[:, 0:N_HEADS]
    lf = _log_sigmoid(g_ref[:, LANES:LANES + N_HEADS])
    m0 = m_ref[...]
    mt = jnp.maximum(m0 + lf, gi)
    inter_all = jnp.exp(m0 + lf - mt)
    dm_all = jnp.exp(gi - mt)
    emt_all = jnp.exp(-mt)
    mn_ref[...] = mt
    for h in range(N_HEADS):
        col = slice(h, h + 1)
        q = q_ref[:, _hs(h)]
        k = k_ref[:, _hs(h)] * (HEAD_DIM ** -0.5)
        n_old = n_ref[:, h, :]
        inter, dm = inter_all[:, col], dm_all[:, col]
        s = jnp.sum(q * k, axis=1, keepdims=True) * dm
        den = s + inter * jnp.sum(q * n_old, axis=1, keepdims=True)
        kw = k * dm
        kw_ref[:, _hs(h)] = kw
        dec_ref[:, _hs(h)] = jnp.broadcast_to(inter, kw.shape)
        s_ref[:, _hs(h)] = jnp.broadcast_to(s, kw.shape)
        den_ref[:, _hs(h)] = jnp.broadcast_to(jnp.maximum(jnp.abs(den), emt_all[:, col]), kw.shape)
        nn_ref[:, h, :] = inter * n_old + kw


def _mlstm_pre(proj_qkvo, gates, state_n, state_m, layer, tile):
    nseq = proj_qkvo.shape[0]
    d = N_HEADS * HEAD_DIM

    def rows_at(col):
        return pl.BlockSpec((tile, d), lambda i: (i, col))

    wide_sds = jax.ShapeDtypeStruct((nseq, d), F32)
    return pl.pallas_call(
        _mlstm_pre_kernel,
        grid=(nseq // tile,),
        in_specs=[rows_at(0), rows_at(1),
                  pl.BlockSpec((tile, 2 * LANES), lambda i: (i, 0)),
                  pl.BlockSpec((None, tile, N_HEADS, HEAD_DIM), lambda i: (layer, i, 0, 0)),
                  pl.BlockSpec((None, tile, N_HEADS), lambda i: (layer, i, 0))],
        out_specs=[rows_at(0)] * 4 + [pl.BlockSpec((tile, N_HEADS, HEAD_DIM), lambda i: (i, 0, 0)),
                                      pl.BlockSpec((tile, N_HEADS), lambda i: (i, 0))],
        out_shape=[wide_sds] * 4 + [jax.ShapeDtypeStruct((nseq, N_HEADS, HEAD_DIM), F32),
                                    jax.ShapeDtypeStruct((nseq, N_HEADS), F32)],
        compiler_params=_params(1),
        name="mlstm_pre",
    )(proj_qkvo, proj_qkvo, gates, state_n, state_m)


def _sample_memory_update(seq, i, q_ref, v_ref, kw_ref, dec_ref, c_ref, cn_ref, r_ref):
    mxu_rows = 2 * SUBLANES
    row0 = lax.broadcasted_iota(jnp.int32, (mxu_rows, HEAD_DIM), 0) == 0
    row = pl.ds(seq, 1)
    for h in range(N_HEADS):
        qrow = jnp.broadcast_to(q_ref[row, _hs(h)], (mxu_rows, HEAD_DIM)).astype(BF16)
        c_old = c_ref[i, h]
        r_ref[row, _hs(h)] = _dot(qrow, c_old.astype(BF16))[0:1, :]
        kw = jnp.where(row0, jnp.broadcast_to(kw_ref[row, _hs(h)], (mxu_rows, HEAD_DIM)), 0.0)
        vrow = jnp.broadcast_to(v_ref[row, _hs(h)], (mxu_rows, HEAD_DIM))
        c_new = dec_ref[row, _hs(h)] * c_old + _dot_tn(kw.astype(BF16), vrow.astype(BF16))
        if len(cn_ref.shape) == 5:
            for layer_slot in range(cn_ref.shape[0]):
                cn_ref[layer_slot, i, h] = c_new
        else:
            cn_ref[i, h] = c_new


def _merge_step_kernel(x_ref, r_ref, s_ref, den_ref, dec_ref, v_ref, o_ref, mhg_ref, ox_ref,
                       u_ref, g0_ref, g1_ref, g2_ref, pool_ref,
                       wmo_ref, wpool_ref, pscale_ref, wxo_ref, wo_ref, lng_ref, lnb_ref,
                       x1_ref, pooln_ref):
    x = x_ref[...]
    u = u_ref[...]
    ht = (s_ref[...] * v_ref[...] + dec_ref[...] * r_ref[...]) / den_ref[...]
    hc = jax.nn.sigmoid(o_ref[...]) * ht
    hn = jnp.concatenate([_head_norm(hc[:, _hs(h)]) for h in range(N_HEADS)], axis=1)
    hn = hn * mhg_ref[...]
    for r in range(1, POOL_BUF):
        pooln_ref[:, r - 1, :] = pool_ref[:, r, :]
    pooln_ref[:, POOL_BUF - 1, :] = u
    yp_groups = []
    for g, window in enumerate(POOL_WINDOWS):
        cols = slice(g * POOL_GROUP_DIM, (g + 1) * POOL_GROUP_DIM)
        wsum = u[:, cols]
        for back in range(1, window):
            wsum = wsum + pool_ref[:, POOL_BUF - back, cols]
        cnt = float(min(window, PAST_LEN + 1))
        dlt = wsum / cnt - u[:, cols]
        yp_groups.append(_dot(dlt.astype(BF16), wpool_ref[g]))
    yp = jnp.concatenate(yp_groups, axis=1) * pscale_ref[...]
    ym = _dot(hn.astype(BF16), wmo_ref[...])
    yx = _dot(ox_ref[...].astype(BF16), wxo_ref[...])
    mix = (jax.nn.sigmoid(g0_ref[...]) * ym + jax.nn.sigmoid(g1_ref[...]) * yp
           + jax.nn.sigmoid(g2_ref[...]) * yx)
    y = ALPHA * x + _dot(mix.astype(BF16), wo_ref[...])
    x1_ref[...] = _layer_norm(y, lng_ref[...], lnb_ref[...])


def _merge_step(x, readout, s_rep, den_rep, dec_rep, proj_qkvo, mh_g, ox, proj_rest, pool_state,
                w_m_out, w_pool, pool_scale, w_x_out, w_o, ln_g, ln_b, layer, tile):
    rows, d = x.shape
    st_shape = pool_state.shape[2:]

    def lw(shape):
        return pl.BlockSpec((None,) + shape, lambda i: (layer,) + (0,) * len(shape))

    def rows_at(col):
        return pl.BlockSpec((tile, d), lambda i: (i, col))

    return pl.pallas_call(
        _merge_step_kernel,
        grid=(rows // tile,),
        in_specs=[rows_at(0), rows_at(0), rows_at(0), rows_at(0), rows_at(0),
                  rows_at(2), rows_at(3), lw((1, d)), rows_at(0),
                  rows_at(REST_U), rows_at(REST_G0), rows_at(REST_G1), rows_at(REST_G2),
                  pl.BlockSpec((None, tile) + st_shape, lambda i: (layer, i, 0, 0)),
                  lw((d, d)), lw((len(POOL_WINDOWS), POOL_GROUP_DIM, POOL_GROUP_DIM)),
                  lw((1, d)), lw((d, d)), lw((d, d)), lw((1, d)), lw((1, d))],
        out_specs=[rows_at(0), pl.BlockSpec((tile,) + st_shape, lambda i: (i, 0, 0))],
        out_shape=[jax.ShapeDtypeStruct((rows, d), F32),
                   jax.ShapeDtypeStruct((rows,) + st_shape, F32)],
        compiler_params=_params(1),
        name="merge_step",
    )(x, readout, s_rep, den_rep, dec_rep, proj_qkvo, proj_qkvo, mh_g, ox,
      proj_rest, proj_rest, proj_rest, proj_rest, pool_state, w_m_out, w_pool,
      pool_scale, w_x_out, w_o, ln_g, ln_b)


def kernel(x_prompt, mem_prompt, x_sample, cache_mem_k, cache_mem_v, state_C, state_n, state_m,
           state_pool, state_conv, ln_in_g, ln_in_b, w_in, b_in, mh_g, w_m_out, w_pool,
           pool_scale, w_mem_kv, w_x_out, w_o, ln1_g, ln1_b, w_up, conv_w, conv_b, w_down,
           ln2_g, ln2_b):
    batch, seq, d = x_prompt.shape
    nseq = x_sample.shape[0]
    n_mem = mem_prompt.shape[1]
    depth = w_in.shape[0]
    d_ff = w_down.shape[1]
    wide = N_HEADS * d
    gate_off = wide
    rest_off = wide + 2 * N_HEADS
    assert d == N_HEADS * HEAD_DIM and d_ff % FF_CHUNK == 0
    assert all(seq % tile == 0 for tile in (MLSTM_TILE, MIX_TILE, FF_TILE))
    assert depth == DEPTH and x_sample.shape[1] == 1

    def row(v):
        return v.reshape(depth, 1, v.shape[-1])

    w_qkvo, w_rest = w_in[:, :, :wide].astype(BF16), w_in[:, :, rest_off:].astype(BF16)
    b_qkvo, b_rest = row(b_in[:, :wide]), row(b_in[:, rest_off:])
    lane_pad = ((0, 0), (0, 0), (0, LANES - N_HEADS))
    w_if = jnp.concatenate(
        [jnp.pad(w_in[:, :, gate_off:gate_off + N_HEADS], lane_pad),
         jnp.pad(w_in[:, :, gate_off + N_HEADS:rest_off], lane_pad)], axis=2).astype(BF16)
    b_if = row(jnp.concatenate(
        [jnp.pad(b_in[:, gate_off:gate_off + N_HEADS], lane_pad[1:]),
         jnp.pad(b_in[:, gate_off + N_HEADS:rest_off], lane_pad[1:])], axis=1))
    w_m_out_b, w_pool_b, w_x_out_b, w_o_b = (w.astype(BF16) for w in (w_m_out, w_pool, w_x_out, w_o))
    w_up_b, w_down_b = w_up.astype(BF16), w_down.astype(BF16)
    w_kv_b = w_mem_kv.astype(BF16)
    mh_g_r, pool_scale_r, conv_b_r = row(mh_g), row(pool_scale), row(conv_b)
    ln1_g_r, ln1_b_r, ln2_g_r, ln2_b_r = row(ln1_g), row(ln1_b), row(ln2_g), row(ln2_b)

    xp = x_prompt.reshape(batch * seq, d)
    xs = x_sample.reshape(nseq, d)
    mem2d = mem_prompt.reshape(batch * n_mem, d)

    xp = _layer_norm_rows(xp, ln_in_g.reshape(1, d), ln_in_b.reshape(1, d), 2 * TIME_TILE)
    xs = _layer_norm_rows(xs, ln_in_g.reshape(1, d), ln_in_b.reshape(1, d), nseq)

    outs = {name: [] for name in ("pC", "pn", "pm", "ppool", "pconv", "sn", "sm", "spool", "sconv")}
    p_mem_k, p_mem_v, mk_b, mv_b = _mem_kv(mem2d, w_kv_b, batch)
    s_c = None
    for l in range(depth):
        proj_qkvo = _matmul_bias(xs, w_qkvo, b_qkvo, l, nseq, d, "proj_sample")
        proj_rest = _matmul_bias(xs, w_rest, b_rest, l, nseq, d, "proj_sample")
        gates = _matmul_bias(xs, w_if, b_if, l, nseq, 2 * LANES, "gates_sample")
        kw_s, dec_s, s_rep, den_rep, s_n, s_m = _mlstm_pre(proj_qkvo, gates, state_n, state_m,
                                                          l, nseq // 2)

        ymg, p_c, p_n, p_m, ox = _mlstm_prompt(xp, batch, w_qkvo, b_qkvo, w_if, b_if, w_rest,
                                               b_rest, mh_g_r, w_m_out_b, proj_rest,
                                               cache_mem_k, cache_mem_v, l)
        x1, p_pool, readout, s_c = _mix_prompt(xp, ymg, mk_b, mv_b, batch, w_rest, b_rest,
                                               w_pool_b, pool_scale_r, w_x_out_b, w_o_b, ln1_g_r,
                                               ln1_b_r, proj_qkvo, kw_s, dec_s, state_C, s_c, l)
        xp, p_conv = _ffn_prompt(x1, batch, w_up_b, conv_w, conv_b_r, w_down_b, ln2_g_r,
                                 ln2_b_r, l)
        outs["pC"].append(p_c)
        outs["pn"].append(p_n)
        outs["pm"].append(p_m[:, 0, :N_HEADS])
        outs["ppool"].append(p_pool[:, 2 * SUBLANES - POOL_BUF:, :])
        outs["pconv"].append(p_conv[:, SUBLANES - (CONV_WIDTH - 1):, :])

        x1s, s_pool = _merge_step(xs, readout, s_rep, den_rep, dec_s, proj_qkvo, mh_g_r, ox,
                                  proj_rest, state_pool, w_m_out_b, w_pool_b, pool_scale_r,
                                  w_x_out_b, w_o_b, ln1_g_r, ln1_b_r, l, nseq // 2)
        xs, s_conv = _ffn_sample(x1s, state_conv, w_up_b, conv_w, conv_b_r, w_down_b, ln2_g_r,
                                 ln2_b_r, l, nseq // 2)
        outs["sn"].append(s_n)
        outs["sm"].append(s_m)
        outs["spool"].append(s_pool)
        outs["sconv"].append(s_conv)

    stacked = {k: jnp.stack(v) for k, v in outs.items()}
    return (xp.reshape(batch, seq, d), xs.reshape(nseq, 1, d),
            stacked["pC"], stacked["pn"], stacked["pm"], stacked["ppool"], stacked["pconv"],
            p_mem_k, p_mem_v,
            s_c, stacked["sn"], stacked["sm"], stacked["spool"], stacked["sconv"])
```

```python
import jax
import jax.numpy as jnp
from jax import lax
from jax.experimental import pallas as pl
from jax.experimental.pallas import tpu as pltpu

F32 = jnp.float32
BF16 = jnp.bfloat16

N_HEADS = 4
HEAD_DIM = 256
POOL_WINDOWS = (2, 4, 8, 16)
POOL_GROUP_DIM = 256
POOL_BUF = 15
CONV_WIDTH = 3
DEPTH = 4
PAST_LEN = 16384
ALPHA = (2.0 * DEPTH) ** 0.25
LN_EPS = 1e-5
NEG = -1e30

SUBLANES = 8
LANES = 128
TIME_TILE = 256
MLSTM_TILE = 256
HEAD_LOOKAHEAD = 1
FF_CHUNK = 256
MIX_TILE = 256
FF_TILE = 512
FF_SUB_ROWS = 256
FF_LOOKAHEAD = 3
VMEM_LIMIT = 56 * 1024 * 1024

REST_U, REST_XQ, REST_G0, REST_G1, REST_G2 = range(5)


def _params(n_axes):
    return pltpu.CompilerParams(
        dimension_semantics=("arbitrary",) * n_axes, vmem_limit_bytes=VMEM_LIMIT)


def _dot(a, b):
    return jnp.dot(a, b, preferred_element_type=F32)


def _dot_nt(a, b):
    return lax.dot_general(a, b, (((1,), (1,)), ((), ())), preferred_element_type=F32)


def _dot_tn(a, b):
    return lax.dot_general(a, b, (((0,), (0,)), ((), ())), preferred_element_type=F32)


def _layer_norm(x, g, b):
    mu = jnp.mean(x, axis=-1, keepdims=True)
    xc = x - mu
    var = jnp.mean(xc * xc, axis=-1, keepdims=True)
    return xc * lax.rsqrt(var + LN_EPS) * g + b


def _head_norm(x):
    mu = jnp.mean(x, axis=-1, keepdims=True)
    xc = x - mu
    var = jnp.mean(xc * xc, axis=-1, keepdims=True)
    return xc * lax.rsqrt(var + LN_EPS)


def _log_sigmoid(x):
    return jnp.minimum(x, 0.0) - jnp.log1p(jnp.exp(-jnp.abs(x)))


def _scan_rows(x, op):
    n = x.shape[0]
    rows = lax.broadcasted_iota(jnp.int32, x.shape, 0)
    shift = 1
    while shift < n:
        x = jnp.where(rows >= shift, op(x, pltpu.roll(x, shift, 0)), x)
        shift *= 2
    return x


def _hs(h):
    return slice(h * HEAD_DIM, (h + 1) * HEAD_DIM)


def _ln_kernel(x_ref, g_ref, b_ref, o_ref):
    o_ref[...] = _layer_norm(x_ref[...], g_ref[...], b_ref[...])


def _layer_norm_rows(x, g, b, tile):
    rows, d = x.shape
    return pl.pallas_call(
        _ln_kernel,
        grid=(rows // tile,),
        in_specs=[pl.BlockSpec((tile, d), lambda i: (i, 0)),
                  pl.BlockSpec((1, d), lambda i: (0, 0)),
                  pl.BlockSpec((1, d), lambda i: (0, 0))],
        out_specs=pl.BlockSpec((tile, d), lambda i: (i, 0)),
        out_shape=jax.ShapeDtypeStruct((rows, d), F32),
        compiler_params=_params(1),
        name="ln_rows",
    )(x, g, b)


def _mm_kernel(x_ref, w_ref, b_ref, o_ref):
    o_ref[...] = _dot(x_ref[...].astype(BF16), w_ref[...]) + b_ref[...]


def _matmul_bias(x, w_all, b_all, layer, tm, tn, name):
    rows, kdim = x.shape
    n = b_all.shape[-1]
    return pl.pallas_call(
        _mm_kernel,
        grid=(n // tn, rows // tm),
        in_specs=[pl.BlockSpec((tm, kdim), lambda j, i: (i, 0)),
                  pl.BlockSpec((None, kdim, tn), lambda j, i: (layer, 0, j)),
                  pl.BlockSpec((None, 1, tn), lambda j, i: (layer, 0, j))],
        out_specs=pl.BlockSpec((tm, tn), lambda j, i: (i, j)),
        out_shape=jax.ShapeDtypeStruct((rows, n), F32),
        compiler_params=_params(2),
        name=name,
    )(x, w_all, b_all)


def _mem_kv_kernel(x_ref, w_ref, k_ref, v_ref, kb_ref, vb_ref):
    d = N_HEADS * HEAD_DIM
    res = _dot(x_ref[...].astype(BF16), w_ref[...])
    for h in range(N_HEADS):
        k_ref[:, h, :] = res[:, _hs(h)]
        v_ref[:, h, :] = res[:, d + h * HEAD_DIM:d + (h + 1) * HEAD_DIM]
    kb_ref[...] = res[:, :d].astype(BF16)
    vb_ref[...] = res[:, d:].astype(BF16)


def _mem_kv(mem2d, w_kv, batch):
    rows, d = mem2d.shape
    n_mem = rows // batch
    depth = w_kv.shape[0]
    out_spec = pl.BlockSpec((None, None, n_mem, N_HEADS, HEAD_DIM), lambda l, b: (l, b, 0, 0, 0))
    out_sds = jax.ShapeDtypeStruct((depth, batch, n_mem, N_HEADS, HEAD_DIM), F32)
    dense_spec = pl.BlockSpec((None, n_mem, d), lambda l, b: (l, b, 0))
    dense_sds = jax.ShapeDtypeStruct((depth, rows, d), BF16)
    return pl.pallas_call(
        _mem_kv_kernel,
        grid=(depth, batch),
        in_specs=[pl.BlockSpec((n_mem, d), lambda l, b: (b, 0)),
                  pl.BlockSpec((None, d, 2 * d), lambda l, b: (l, 0, 0))],
        out_specs=[out_spec, out_spec, dense_spec, dense_spec],
        out_shape=[out_sds, out_sds, dense_sds, dense_sds],
        compiler_params=_params(2),
        name="mem_kv",
    )(mem2d, w_kv)


def _sample_attn_scores(q_row, keys):
    sub = lax.broadcasted_iota(jnp.int32, (SUBLANES, HEAD_DIM), 0)
    q = jnp.zeros((SUBLANES, HEAD_DIM), F32)
    for h in range(N_HEADS):
        q = jnp.where(sub == h, jnp.broadcast_to(q_row(h), (SUBLANES, HEAD_DIM)), q)
    keys2d = keys.reshape(keys.shape[0] * N_HEADS, HEAD_DIM).astype(BF16)
    return _dot_nt(q.astype(BF16), keys2d)


def _sample_attn_softmax(s):
    s = s * (HEAD_DIM ** -0.5)
    head = lax.broadcasted_iota(jnp.int32, s.shape, 0) % SUBLANES
    col_head = lax.broadcasted_iota(jnp.int32, s.shape, 1) % N_HEADS
    s = jnp.where((col_head == head) | (head >= N_HEADS), s, -jnp.inf)
    p = jnp.exp(s - jnp.max(s, axis=-1, keepdims=True))
    return p / jnp.sum(p, axis=-1, keepdims=True)


def _sample_attn_output(p, vals):
    vals2d = vals.reshape(vals.shape[0] * N_HEADS, HEAD_DIM).astype(BF16)
    return _dot(p.astype(BF16), vals2d)


def _mlstm_prompt_kernel(x_ref, wqkvo_ref, bqkvo_ref, wif_ref, bif_ref, wg0_ref, bg0_ref,
                         mhg_ref, wmo_ref, xq_ref, sk_ref, sv_ref,
                         y_ref, c_ref, n_ref, m_ref, ox_ref, s_scr):
    t = pl.program_id(1)
    d = N_HEADS * HEAD_DIM
    ride = sk_ref.shape[0]
    seq0 = (pl.program_id(0) * pl.num_programs(1) + t) * ride

    @pl.when(t == 0)
    def _():
        c_ref[...] = jnp.zeros(c_ref.shape, F32)
        n_ref[...] = jnp.zeros(n_ref.shape, F32)
        m_ref[...] = jnp.zeros(m_ref.shape, F32)

    for i in range(ride):
        s_scr[i * SUBLANES:(i + 1) * SUBLANES, :] = _sample_attn_scores(
            lambda h, i=i: xq_ref[pl.ds(seq0 + i, 1), _hs(h)], sk_ref[i])

    lc = TIME_TILE
    causal = (lax.broadcasted_iota(jnp.int32, (lc, lc), 1)
              <= lax.broadcasted_iota(jnp.int32, (lc, lc), 0))

    for chunk, r0 in enumerate(range(0, x_ref.shape[0], lc)):
        rows = slice(r0, r0 + lc)
        xb = x_ref[rows, :].astype(BF16)
        gates = _dot(xb, wif_ref[...]) + bif_ref[...]
        gi, gf = gates[:, 0:LANES], gates[:, LANES:2 * LANES]
        lf = _log_sigmoid(gf)
        bcum = _scan_rows(lf, jnp.add)
        a = gi - bcum
        m0 = m_ref[0]
        mt = bcum + jnp.maximum(m0, _scan_rows(a, jnp.maximum))
        inter = jnp.exp(m0 + bcum - mt)
        bm = bcum - mt
        emt = jnp.exp(-mt)
        b_last = bcum[lc - 1:lc, :]
        m_last = mt[lc - 1:lc, :]
        ws = jnp.exp(a + b_last - m_last)
        decay = jnp.exp(m0 + b_last - m_last)
        a_t = a.T

        def head_proj(h, xb=xb):
            def proj(part):
                cols = slice(part * d + h * HEAD_DIM, part * d + (h + 1) * HEAD_DIM)
                return _dot(xb, wqkvo_ref[:, cols]) + bqkvo_ref[:, cols]
            return proj(0), proj(1) * (HEAD_DIM ** -0.5), proj(2), proj(3)

        hn_heads = []
        projs = [head_proj(h) for h in range(HEAD_LOOKAHEAD)]
        for h in range(N_HEADS):
            if h + HEAD_LOOKAHEAD < N_HEADS:
                projs.append(head_proj(h + HEAD_LOOKAHEAD))
            elif h + HEAD_LOOKAHEAD == N_HEADS:
                g0 = _dot(xb, wg0_ref[...]) + bg0_ref[...]
            q, k, v, o = projs[h]
            qb = q.astype(BF16)
            vb = v.astype(BF16)
            col = slice(h, h + 1)
            logd = a_t[h:h + 1, :] + bm[:, col]
            dmat = jnp.exp(jnp.where(causal, logd, NEG))
            s = _dot_nt(qb, k.astype(BF16)) * dmat
            c_old = c_ref[0, h]
            n_old = n_ref[0, h:h + 1, :]
            inter_h = inter[:, col]
            num = _dot(s.astype(BF16), vb) + inter_h * _dot(qb, c_old.astype(BF16))
            den = (jnp.sum(s, axis=1, keepdims=True)
                   + inter_h * jnp.sum(q * n_old, axis=1, keepdims=True))
            ht = num / jnp.maximum(jnp.abs(den), emt[:, col])
            kw = k * ws[:, col]
            c_ref[0, h] = decay[:, col] * c_old + _dot_tn(kw.astype(BF16), vb)
            n_ref[0, h:h + 1, :] = decay[:, col] * n_old + jnp.sum(kw, axis=0, keepdims=True)
            hn = _head_norm(jax.nn.sigmoid(o) * ht) * mhg_ref[:, _hs(h)]
            hn_heads.append(hn.astype(BF16))
            if chunk == 0 and h == 0:
                s_scr[...] = _sample_attn_softmax(s_scr[...])
        m_ref[0] = m_last

        if chunk == 0:
            for i in range(ride):
                out = _sample_attn_output(s_scr[i * SUBLANES:(i + 1) * SUBLANES, :], sv_ref[i])
                for h in range(N_HEADS):
                    ox_ref[pl.ds(seq0 + i, 1), _hs(h)] = out[h:h + 1, :]

        ym = _dot(jnp.concatenate(hn_heads, axis=1), wmo_ref[...])
        y_ref[rows, :] = jax.nn.sigmoid(g0) * ym


def _mlstm_prompt(x2d, batch, w_qkvo, b_qkvo, w_if, b_if, w_rest, b_rest, mh_g, w_m_out,
                  proj_rest_s, mem_k, mem_v, layer):
    rows, d = x2d.shape
    nt = rows // batch // MLSTM_TILE
    wide = N_HEADS * d
    nseq, n_mem = mem_k.shape[1], mem_k.shape[2]
    ride = nseq // (batch * nt)
    assert ride * batch * nt == nseq

    def lw(shape, col):
        return pl.BlockSpec((None,) + shape, lambda b, t: (layer, 0, col),
                            pipeline_mode=pl.Buffered(1))

    kv_spec = pl.BlockSpec((None, ride, n_mem, N_HEADS, HEAD_DIM),
                           lambda b, t: (layer, b * nt + t, 0, 0, 0))
    return pl.pallas_call(
        _mlstm_prompt_kernel,
        grid=(batch, nt),
        in_specs=[pl.BlockSpec((MLSTM_TILE, d), lambda b, t: (b * nt + t, 0)),
                  lw((d, wide), 0), lw((1, wide), 0),
                  lw((d, 2 * LANES), 0), lw((1, 2 * LANES), 0),
                  lw((d, d), REST_G0), lw((1, d), REST_G0),
                  lw((1, d), 0), lw((d, d), 0),
                  pl.BlockSpec((nseq, d), lambda b, t: (0, REST_XQ)), kv_spec, kv_spec],
        out_specs=[pl.BlockSpec((MLSTM_TILE, d), lambda b, t: (b * nt + t, 0)),
                   pl.BlockSpec((1, N_HEADS, HEAD_DIM, HEAD_DIM), lambda b, t: (b, 0, 0, 0)),
                   pl.BlockSpec((1, N_HEADS, HEAD_DIM), lambda b, t: (b, 0, 0)),
                   pl.BlockSpec((1, 1, LANES), lambda b, t: (b, 0, 0)),
                   pl.BlockSpec((nseq, d), lambda b, t: (0, 0))],
        out_shape=[jax.ShapeDtypeStruct((rows, d), F32),
                   jax.ShapeDtypeStruct((batch, N_HEADS, HEAD_DIM, HEAD_DIM), F32),
                   jax.ShapeDtypeStruct((batch, N_HEADS, HEAD_DIM), F32),
                   jax.ShapeDtypeStruct((batch, 1, LANES), F32),
                   jax.ShapeDtypeStruct((nseq, d), F32)],
        scratch_shapes=[pltpu.VMEM((ride * SUBLANES, N_HEADS * n_mem), F32)],
        compiler_params=_params(2),
        name="mlstm_prompt",
    )(x2d, w_qkvo, b_qkvo, w_if, b_if, w_rest, b_rest, mh_g, w_m_out, proj_rest_s, mem_k, mem_v)


def _pool_window_sums(ext, window):
    s, span = ext, 1
    while span < window:
        s = s + pltpu.roll(s, span, 0)
        span *= 2
    return s


def _mix_prompt_kernel(x_ref, ymg_ref, mk_ref, mv_ref, wu_ref, bu_ref, wxq_ref, bxq_ref,
                       wg1_ref, bg1_ref, wg2_ref, bg2_ref, wpool_ref, pscale_ref, wxo_ref,
                       wo_ref, lng_ref, lnb_ref, sq_ref, sv_ref, skw_ref, sdec_ref, sc_ref,
                       *refs):
    x1_ref, pool_ref, sr_ref, scn_ref, ext_scr = refs[-5:]
    t = pl.program_id(1)
    tt = x_ref.shape[0]
    carry = 2 * SUBLANES
    ride = sc_ref.shape[0]
    seq0 = (pl.program_id(0) * pl.num_programs(1) + t) * ride

    @pl.when(t == 0)
    def _():
        ext_scr[0:carry, :] = jnp.zeros((carry, ext_scr.shape[1]), F32)

    for sub, r0 in enumerate(range(0, tt, TIME_TILE)):
        rows = slice(r0, r0 + TIME_TILE)
        for i in range(sub * ride // (tt // TIME_TILE), (sub + 1) * ride // (tt // TIME_TILE)):
            _sample_memory_update(seq0 + i, i, sq_ref, sv_ref, skw_ref, sdec_ref, sc_ref,
                                  scn_ref, sr_ref)

        x = x_ref[rows, :]
        xb = x.astype(BF16)

        u = _dot(xb, wu_ref[...]) + bu_ref[...]
        xq = (_dot(xb, wxq_ref[...]) + bxq_ref[...]).astype(BF16)

        ext_scr[carry + r0:carry + r0 + TIME_TILE, :] = u
        pos = t * tt + r0 + lax.broadcasted_iota(jnp.int32, (TIME_TILE, 1), 0)
        scores = [_dot_nt(xq[:, _hs(h)], mk_ref[:, _hs(h)]) for h in range(N_HEADS)]
        yp_groups = []
        for g, window in enumerate(POOL_WINDOWS):
            cols = slice(g * POOL_GROUP_DIM, (g + 1) * POOL_GROUP_DIM)
            ext = ext_scr[r0:r0 + carry + TIME_TILE, cols]
            wsum = _pool_window_sums(ext, window)[carry:, :]
            cnt = jnp.minimum(window, pos + 1).astype(F32)
            dlt = wsum / cnt - u[:, cols]
            yp_groups.append(_dot(dlt.astype(BF16), wpool_ref[g]))
        if r0 + TIME_TILE == tt:
            ext_scr[0:carry, :] = u[TIME_TILE - carry:, :]
            pool_ref[0] = u[TIME_TILE - carry:, :]
        g1 = _dot(xb, wg1_ref[...]) + bg1_ref[...]

        heads = []
        for h in range(N_HEADS):
            s = scores[h] * (HEAD_DIM ** -0.5)
            p = jnp.exp(s - jnp.max(s, axis=-1, keepdims=True))
            p = p / jnp.sum(p, axis=-1, keepdims=True)
            heads.append(_dot(p.astype(BF16), mv_ref[:, _hs(h)]).astype(BF16))
        g2 = _dot(xb, wg2_ref[...]) + bg2_ref[...]
        yp = jnp.concatenate(yp_groups, axis=1) * pscale_ref[...]
        yx = _dot(jnp.concatenate(heads, axis=1), wxo_ref[...])
        mix = ymg_ref[rows, :] + jax.nn.sigmoid(g1) * yp + jax.nn.sigmoid(g2) * yx
        y = ALPHA * x + _dot(mix.astype(BF16), wo_ref[...])
        x1_ref[rows, :] = _layer_norm(y, lng_ref[...], lnb_ref[...])


def _mix_prompt(x2d, ymg, mk, mv, batch, w_rest, b_rest, w_pool, pool_scale, w_x_out, w_o,
                ln_g, ln_b, proj_qkvo_s, kw_s, dec_s, state_c, prev_c, layer):
    rows, d = x2d.shape
    nt = rows // batch // MIX_TILE
    n_mem = mk.shape[1] // batch
    carry = 2 * SUBLANES
    depth, nseq = state_c.shape[:2]
    ride = nseq // (batch * nt)
    assert ride * batch * nt == nseq and ride % (MIX_TILE // TIME_TILE) == 0

    def lw(shape, col):
        return pl.BlockSpec((None,) + shape, lambda b, t: (layer, 0, col),
                            pipeline_mode=pl.Buffered(1))

    def sample_rows(col):
        return pl.BlockSpec((nseq, d), lambda b, t: (0, col))

    row_spec = pl.BlockSpec((MIX_TILE, d), lambda b, t: (b * nt + t, 0))
    mem_spec = pl.BlockSpec((None, n_mem, d), lambda b, t: (layer, b, 0))
    c_block = (ride, N_HEADS, HEAD_DIM, HEAD_DIM)
    c_spec = pl.BlockSpec((None,) + c_block, lambda b, t: (layer, b * nt + t, 0, 0, 0))
    if prev_c is None:
        c_out_spec = pl.BlockSpec((depth,) + c_block, lambda b, t: (0, b * nt + t, 0, 0, 0))
        chain_specs, chain_args, aliases = [], [], {}
    else:
        c_out_spec = c_spec
        chain_specs, chain_args, aliases = [pl.BlockSpec(memory_space=pl.ANY)], [prev_c], {23: 3}
    return pl.pallas_call(
        _mix_prompt_kernel,
        grid=(batch, nt),
        in_specs=[row_spec, row_spec, mem_spec, mem_spec,
                  lw((d, d), REST_U), lw((1, d), REST_U),
                  lw((d, d), REST_XQ), lw((1, d), REST_XQ),
                  lw((d, d), REST_G1), lw((1, d), REST_G1),
                  lw((d, d), REST_G2), lw((1, d), REST_G2),
                  pl.BlockSpec((None, len(POOL_WINDOWS), POOL_GROUP_DIM, POOL_GROUP_DIM),
                               lambda b, t: (layer, 0, 0, 0)),
                  lw((1, d), 0), lw((d, d), 0), lw((d, d), 0), lw((1, d), 0), lw((1, d), 0),
                  sample_rows(0), sample_rows(2), sample_rows(0), sample_rows(0), c_spec]
        + chain_specs,
        out_specs=[row_spec, pl.BlockSpec((1, carry, d), lambda b, t: (b, 0, 0)),
                   sample_rows(0), c_out_spec],
        out_shape=[jax.ShapeDtypeStruct((rows, d), F32),
                   jax.ShapeDtypeStruct((batch, carry, d), F32),
                   jax.ShapeDtypeStruct((nseq, d), F32),
                   jax.ShapeDtypeStruct(state_c.shape, F32)],
        input_output_aliases=aliases,
        scratch_shapes=[pltpu.VMEM((carry + MIX_TILE, d), F32)],
        compiler_params=_params(2),
        name="mix_prompt",
    )(x2d, ymg, mk, mv, w_rest, b_rest, w_rest, b_rest, w_rest, b_rest, w_rest, b_rest,
      w_pool, pool_scale, w_x_out, w_o, ln_g, ln_b,
      proj_qkvo_s, proj_qkvo_s, kw_s, dec_s, state_c, *chain_args)


def _ffn_chunk(xb, j, d_ff, wup_ref, cw_ref, cb_ref, wdn_ref, conv_taps):
    halves = _ffn_up(xb, j, d_ff, wup_ref)
    return _ffn_gate_down(halves, j, d_ff, cw_ref, cb_ref, wdn_ref, conv_taps), halves


def _ffn_up(xb, j, d_ff, wup_ref):
    return [_dot(xb, wup_ref[:, base + j * FF_CHUNK:base + (j + 1) * FF_CHUNK])
            for base in (0, d_ff)]


def _ffn_gate_down(halves, j, d_ff, cw_ref, cb_ref, wdn_ref, conv_taps):
    conv = []
    for base, hup in zip((0, d_ff), halves):
        cols = slice(base + j * FF_CHUNK, base + (j + 1) * FF_CHUNK)
        prev2, prev1 = conv_taps(hup, cols)
        conv.append(prev2 * cw_ref[0:1, cols] + prev1 * cw_ref[1:2, cols]
                    + hup * cw_ref[2:3, cols] + cb_ref[:, cols])
    act = jax.nn.gelu(conv[0]) * conv[1]
    return _dot(act.astype(BF16), wdn_ref[j * FF_CHUNK:(j + 1) * FF_CHUNK, :])


def _ffn_prompt_kernel(x_ref, wup_ref, cw_ref, cb_ref, wdn_ref, lng_ref, lnb_ref,
                       o_ref, conv_ref, ext_scr):
    t = pl.program_id(1)
    tt = x_ref.shape[0]
    d_ff = wdn_ref.shape[0]
    n_chunks = d_ff // FF_CHUNK

    @pl.when(t == 0)
    def _():
        ext_scr[0:SUBLANES, :] = jnp.zeros((SUBLANES, ext_scr.shape[1]), F32)

    items = [(r0, j) for r0 in range(0, tt, FF_SUB_ROWS) for j in range(n_chunks)]
    xs = {r0: x_ref[r0:r0 + FF_SUB_ROWS, :] for r0 in range(0, tt, FF_SUB_ROWS)}
    xbs = {r0: xs[r0].astype(BF16) for r0 in xs}

    def up(item):
        r0, j = item
        return _ffn_up(xbs[r0], j, d_ff, wup_ref)

    ups = [up(item) for item in items[:FF_LOOKAHEAD]]
    acc = None
    for i, (r0, j) in enumerate(items):
        if i + FF_LOOKAHEAD < len(items):
            ups.append(up(items[i + FF_LOOKAHEAD]))
        halves = ups[i]
        ups[i] = None
        lo = SUBLANES + r0

        def conv_taps(hup, cols):
            ext_scr[lo:lo + FF_SUB_ROWS, cols] = hup
            return (ext_scr[lo - 2:lo - 2 + FF_SUB_ROWS, cols],
                    ext_scr[lo - 1:lo - 1 + FF_SUB_ROWS, cols])

        part = _ffn_gate_down(halves, j, d_ff, cw_ref, cb_ref, wdn_ref, conv_taps)
        acc = part if j == 0 else part + acc
        if r0 + FF_SUB_ROWS == tt:
            for base, hup in zip((0, d_ff), halves):
                cols = slice(base + j * FF_CHUNK, base + (j + 1) * FF_CHUNK)
                ext_scr[0:SUBLANES, cols] = hup[FF_SUB_ROWS - SUBLANES:, :]
                conv_ref[0, :, cols] = hup[FF_SUB_ROWS - SUBLANES:, :]
        if j == n_chunks - 1:
            o_ref[r0:r0 + FF_SUB_ROWS, :] = _layer_norm(ALPHA * xs[r0] + acc, lng_ref[...],
                                                       lnb_ref[...])


def _ffn_prompt(x2d, batch, w_up, conv_w, conv_b, w_down, ln_g, ln_b, layer):
    rows, d = x2d.shape
    nt = rows // batch // FF_TILE
    d_ff = w_down.shape[1]

    def lw(shape):
        return pl.BlockSpec((None,) + shape, lambda b, t: (layer, 0, 0),
                            pipeline_mode=pl.Buffered(1))

    row_spec = pl.BlockSpec((FF_TILE, d), lambda b, t: (b * nt + t, 0))
    return pl.pallas_call(
        _ffn_prompt_kernel,
        grid=(batch, nt),
        in_specs=[row_spec, lw((d, 2 * d_ff)), lw((CONV_WIDTH, 2 * d_ff)), lw((1, 2 * d_ff)),
                  lw((d_ff, d)), lw((1, d)), lw((1, d))],
        out_specs=[row_spec, pl.BlockSpec((1, SUBLANES, 2 * d_ff), lambda b, t: (b, 0, 0))],
        out_shape=[jax.ShapeDtypeStruct((rows, d), F32),
                   jax.ShapeDtypeStruct((batch, SUBLANES, 2 * d_ff), F32)],
        scratch_shapes=[pltpu.VMEM((SUBLANES + FF_TILE, 2 * d_ff), F32)],
        compiler_params=_params(2),
        name="ffn_prompt",
    )(x2d, w_up, conv_w, conv_b, w_down, ln_g, ln_b)


def _ffn_sample_kernel(x_ref, st_ref, wup_ref, cw_ref, cb_ref, wdn_ref, lng_ref, lnb_ref,
                       o_ref, stn_ref):
    d_ff = wdn_ref.shape[0]
    x = x_ref[...]
    xb = x.astype(BF16)

    def conv_taps(hup, cols):
        return st_ref[:, 0, cols], st_ref[:, 1, cols]

    acc = None
    for j in range(d_ff // FF_CHUNK):
        part, halves = _ffn_chunk(xb, j, d_ff, wup_ref, cw_ref, cb_ref, wdn_ref, conv_taps)
        for base, hup in zip((0, d_ff), halves):
            cols = slice(base + j * FF_CHUNK, base + (j + 1) * FF_CHUNK)
            stn_ref[:, 0, cols] = st_ref[:, 1, cols]
            stn_ref[:, 1, cols] = hup
        acc = part if acc is None else acc + part
    o_ref[...] = _layer_norm(ALPHA * x + acc, lng_ref[...], lnb_ref[...])


def _ffn_sample(x, conv_state, w_up, conv_w, conv_b, w_down, ln_g, ln_b, layer, tile):
    rows, d = x.shape
    d_ff = w_down.shape[1]
    st_shape = conv_state.shape[2:]

    def lw(shape):
        return pl.BlockSpec((None,) + shape, lambda i: (layer, 0, 0))

    return pl.pallas_call(
        _ffn_sample_kernel,
        grid=(rows // tile,),
        in_specs=[pl.BlockSpec((tile, d), lambda i: (i, 0)),
                  pl.BlockSpec((None, tile) + st_shape, lambda i: (layer, i, 0, 0)),
                  lw((d, 2 * d_ff)), lw((CONV_WIDTH, 2 * d_ff)), lw((1, 2 * d_ff)),
                  lw((d_ff, d)), lw((1, d)), lw((1, d))],
        out_specs=[pl.BlockSpec((tile, d), lambda i: (i, 0)),
                   pl.BlockSpec((tile,) + st_shape, lambda i: (i, 0, 0))],
        out_shape=[jax.ShapeDtypeStruct((rows, d), F32),
                   jax.ShapeDtypeStruct((rows,) + st_shape, F32)],
        compiler_params=_params(1),
        name="ffn_sample",
    )(x, conv_state, w_up, conv_w, conv_b, w_down, ln_g, ln_b)


def _mlstm_pre_kernel(q_ref, k_ref, x_ref, wif_ref, bif_ref, n_ref, m_ref,
                      kw_ref, dec_ref, s_ref, den_ref, nn_ref, mn_ref):
    gates = _dot(x_ref[...].astype(BF16), wif_ref[...]) + bif_ref[...]
    gi = gates[:, 0:N_HEADS]
    lf = _log_sigmoid(gates[:, LANES:LANES + N_HEADS])
    m0 = m_ref[...]
    mt = jnp.maximum(m0 + lf, gi)
    inter_all = jnp.exp(m0 + lf - mt)
    dm_all = jnp.exp(gi - mt)
    emt_all = jnp.exp(-mt)
    mn_ref[...] = mt
    for h in range(N_HEADS):
        col = slice(h, h + 1)
        q = q_ref[:, _hs(h)]
        k = k_ref[:, _hs(h)] * (HEAD_DIM ** -0.5)
        n_old = n_ref[:, h, :]
        inter, dm = inter_all[:, col], dm_all[:, col]
        s = jnp.sum(q * k, axis=1, keepdims=True) * dm
        den = s + inter * jnp.sum(q * n_old, axis=1, keepdims=True)
        kw = k * dm
        kw_ref[:, _hs(h)] = kw
        dec_ref[:, _hs(h)] = jnp.broadcast_to(inter, kw.shape)
        s_ref[:, _hs(h)] = jnp.broadcast_to(s, kw.shape)
        den_ref[:, _hs(h)] = jnp.broadcast_to(jnp.maximum(jnp.abs(den), emt_all[:, col]), kw.shape)
        nn_ref[:, h, :] = inter * n_old + kw


def _mlstm_pre(proj_qkvo, x, w_if, b_if, state_n, state_m, layer, tile):
    nseq = proj_qkvo.shape[0]
    d = N_HEADS * HEAD_DIM

    def rows_at(col):
        return pl.BlockSpec((tile, d), lambda i: (i, col))

    wide_sds = jax.ShapeDtypeStruct((nseq, d), F32)
    return pl.pallas_call(
        _mlstm_pre_kernel,
        grid=(nseq // tile,),
        in_specs=[rows_at(0), rows_at(1), rows_at(0),
                  pl.BlockSpec((None, d, 2 * LANES), lambda i: (layer, 0, 0)),
                  pl.BlockSpec((None, 1, 2 * LANES), lambda i: (layer, 0, 0)),
                  pl.BlockSpec((None, tile, N_HEADS, HEAD_DIM), lambda i: (layer, i, 0, 0)),
                  pl.BlockSpec((None, tile, N_HEADS), lambda i: (layer, i, 0))],
        out_specs=[rows_at(0)] * 4 + [pl.BlockSpec((tile, N_HEADS, HEAD_DIM), lambda i: (i, 0, 0)),
                                      pl.BlockSpec((tile, N_HEADS), lambda i: (i, 0))],
        out_shape=[wide_sds] * 4 + [jax.ShapeDtypeStruct((nseq, N_HEADS, HEAD_DIM), F32),
                                    jax.ShapeDtypeStruct((nseq, N_HEADS), F32)],
        compiler_params=_params(1),
        name="mlstm_pre",
    )(proj_qkvo, proj_qkvo, x, w_if, b_if, state_n, state_m)


def _sample_memory_update(seq, i, q_ref, v_ref, kw_ref, dec_ref, c_ref, cn_ref, r_ref):
    mxu_rows = 2 * SUBLANES
    row0 = lax.broadcasted_iota(jnp.int32, (mxu_rows, HEAD_DIM), 0) == 0
    row = pl.ds(seq, 1)
    for h in range(N_HEADS):
        qrow = jnp.broadcast_to(q_ref[row, _hs(h)], (mxu_rows, HEAD_DIM)).astype(BF16)
        c_old = c_ref[i, h]
        r_ref[row, _hs(h)] = _dot(qrow, c_old.astype(BF16))[0:1, :]
        kw = jnp.where(row0, jnp.broadcast_to(kw_ref[row, _hs(h)], (mxu_rows, HEAD_DIM)), 0.0)
        vrow = jnp.broadcast_to(v_ref[row, _hs(h)], (mxu_rows, HEAD_DIM))
        c_new = dec_ref[row, _hs(h)] * c_old + _dot_tn(kw.astype(BF16), vrow.astype(BF16))
        if len(cn_ref.shape) == 5:
            for layer_slot in range(cn_ref.shape[0]):
                cn_ref[layer_slot, i, h] = c_new
        else:
            cn_ref[i, h] = c_new


def _merge_step_kernel(x_ref, r_ref, s_ref, den_ref, dec_ref, v_ref, o_ref, mhg_ref, ox_ref,
                       u_ref, g0_ref, g1_ref, g2_ref, pool_ref,
                       wmo_ref, wpool_ref, pscale_ref, wxo_ref, wo_ref, lng_ref, lnb_ref,
                       x1_ref, pooln_ref):
    x = x_ref[...]
    u = u_ref[...]
    ht = (s_ref[...] * v_ref[...] + dec_ref[...] * r_ref[...]) / den_ref[...]
    hc = jax.nn.sigmoid(o_ref[...]) * ht
    hn = jnp.concatenate([_head_norm(hc[:, _hs(h)]) for h in range(N_HEADS)], axis=1)
    hn = hn * mhg_ref[...]
    for r in range(1, POOL_BUF):
        pooln_ref[:, r - 1, :] = pool_ref[:, r, :]
    pooln_ref[:, POOL_BUF - 1, :] = u
    yp_groups = []
    for g, window in enumerate(POOL_WINDOWS):
        cols = slice(g * POOL_GROUP_DIM, (g + 1) * POOL_GROUP_DIM)
        wsum = u[:, cols]
        for back in range(1, window):
            wsum = wsum + pool_ref[:, POOL_BUF - back, cols]
        cnt = float(min(window, PAST_LEN + 1))
        dlt = wsum / cnt - u[:, cols]
        yp_groups.append(_dot(dlt.astype(BF16), wpool_ref[g]))
    yp = jnp.concatenate(yp_groups, axis=1) * pscale_ref[...]
    ym = _dot(hn.astype(BF16), wmo_ref[...])
    yx = _dot(ox_ref[...].astype(BF16), wxo_ref[...])
    mix = (jax.nn.sigmoid(g0_ref[...]) * ym + jax.nn.sigmoid(g1_ref[...]) * yp
           + jax.nn.sigmoid(g2_ref[...]) * yx)
    y = ALPHA * x + _dot(mix.astype(BF16), wo_ref[...])
    x1_ref[...] = _layer_norm(y, lng_ref[...], lnb_ref[...])


def _merge_step(x, readout, s_rep, den_rep, dec_rep, proj_qkvo, mh_g, ox, proj_rest, pool_state,
                w_m_out, w_pool, pool_scale, w_x_out, w_o, ln_g, ln_b, layer, tile):
    rows, d = x.shape
    st_shape = pool_state.shape[2:]

    def lw(shape):
        return pl.BlockSpec((None,) + shape, lambda i: (layer,) + (0,) * len(shape))

    def rows_at(col):
        return pl.BlockSpec((tile, d), lambda i: (i, col))

    return pl.pallas_call(
        _merge_step_kernel,
        grid=(rows // tile,),
        in_specs=[rows_at(0), rows_at(0), rows_at(0), rows_at(0), rows_at(0),
                  rows_at(2), rows_at(3), lw((1, d)), rows_at(0),
                  rows_at(REST_U), rows_at(REST_G0), rows_at(REST_G1), rows_at(REST_G2),
                  pl.BlockSpec((None, tile) + st_shape, lambda i: (layer, i, 0, 0)),
                  lw((d, d)), lw((len(POOL_WINDOWS), POOL_GROUP_DIM, POOL_GROUP_DIM)),
                  lw((1, d)), lw((d, d)), lw((d, d)), lw((1, d)), lw((1, d))],
        out_specs=[rows_at(0), pl.BlockSpec((tile,) + st_shape, lambda i: (i, 0, 0))],
        out_shape=[jax.ShapeDtypeStruct((rows, d), F32),
                   jax.ShapeDtypeStruct((rows,) + st_shape, F32)],
        compiler_params=_params(1),
        name="merge_step",
    )(x, readout, s_rep, den_rep, dec_rep, proj_qkvo, proj_qkvo, mh_g, ox,
      proj_rest, proj_rest, proj_rest, proj_rest, pool_state, w_m_out, w_pool,
      pool_scale, w_x_out, w_o, ln_g, ln_b)


def kernel(x_prompt, mem_prompt, x_sample, cache_mem_k, cache_mem_v, state_C, state_n, state_m,
           state_pool, state_conv, ln_in_g, ln_in_b, w_in, b_in, mh_g, w_m_out, w_pool,
           pool_scale, w_mem_kv, w_x_out, w_o, ln1_g, ln1_b, w_up, conv_w, conv_b, w_down,
           ln2_g, ln2_b):
    batch, seq, d = x_prompt.shape
    nseq = x_sample.shape[0]
    n_mem = mem_prompt.shape[1]
    depth = w_in.shape[0]
    d_ff = w_down.shape[1]
    wide = N_HEADS * d
    gate_off = wide
    rest_off = wide + 2 * N_HEADS
    assert d == N_HEADS * HEAD_DIM and d_ff % FF_CHUNK == 0
    assert all(seq % tile == 0 for tile in (MLSTM_TILE, MIX_TILE, FF_TILE))
    assert depth == DEPTH and x_sample.shape[1] == 1

    def row(v):
        return v.reshape(depth, 1, v.shape[-1])

    w_qkvo = w_in.astype(BF16)
    w_rest = w_qkvo[:, :, rest_off:]
    b_qkvo, b_rest = row(b_in[:, :wide]), row(b_in[:, rest_off:])
    lane_pad = ((0, 0), (0, 0), (0, LANES - N_HEADS))
    w_if = jnp.concatenate(
        [jnp.pad(w_in[:, :, gate_off:gate_off + N_HEADS], lane_pad),
         jnp.pad(w_in[:, :, gate_off + N_HEADS:rest_off], lane_pad)], axis=2).astype(BF16)
    b_if = row(jnp.concatenate(
        [jnp.pad(b_in[:, gate_off:gate_off + N_HEADS], lane_pad[1:]),
         jnp.pad(b_in[:, gate_off + N_HEADS:rest_off], lane_pad[1:])], axis=1))
    w_m_out_b, w_pool_b, w_x_out_b, w_o_b = (w.astype(BF16) for w in (w_m_out, w_pool, w_x_out, w_o))
    w_up_b, w_down_b = w_up.astype(BF16), w_down.astype(BF16)
    w_kv_b = w_mem_kv.astype(BF16)
    mh_g_r, pool_scale_r, conv_b_r = row(mh_g), row(pool_scale), row(conv_b)
    ln1_g_r, ln1_b_r, ln2_g_r, ln2_b_r = row(ln1_g), row(ln1_b), row(ln2_g), row(ln2_b)

    xp = x_prompt.reshape(batch * seq, d)
    xs = x_sample.reshape(nseq, d)
    mem2d = mem_prompt.reshape(batch * n_mem, d)

    xp = _layer_norm_rows(xp, ln_in_g.reshape(1, d), ln_in_b.reshape(1, d), 2 * TIME_TILE)
    xs = _layer_norm_rows(xs, ln_in_g.reshape(1, d), ln_in_b.reshape(1, d), nseq)

    outs = {name: [] for name in ("pC", "pn", "pm", "ppool", "pconv", "sn", "sm", "spool", "sconv")}
    p_mem_k, p_mem_v, mk_b, mv_b = _mem_kv(mem2d, w_kv_b, batch)
    s_c = None
    for l in range(depth):
        proj_qkvo = _matmul_bias(xs, w_qkvo, b_qkvo, l, nseq, d, "proj_sample")
        proj_rest = _matmul_bias(xs, w_rest, b_rest, l, nseq, d, "proj_sample")
        kw_s, dec_s, s_rep, den_rep, s_n, s_m = _mlstm_pre(proj_qkvo, xs, w_if, b_if, state_n,
                                                          state_m, l, nseq // 2)

        ymg, p_c, p_n, p_m, ox = _mlstm_prompt(xp, batch, w_qkvo, b_qkvo, w_if, b_if, w_rest,
                                               b_rest, mh_g_r, w_m_out_b, proj_rest,
                                               cache_mem_k, cache_mem_v, l)
        x1, p_pool, readout, s_c = _mix_prompt(xp, ymg, mk_b, mv_b, batch, w_rest, b_rest,
                                               w_pool_b, pool_scale_r, w_x_out_b, w_o_b, ln1_g_r,
                                               ln1_b_r, proj_qkvo, kw_s, dec_s, state_C, s_c, l)
        xp, p_conv = _ffn_prompt(x1, batch, w_up_b, conv_w, conv_b_r, w_down_b, ln2_g_r,
                                 ln2_b_r, l)
        outs["pC"].append(p_c)
        outs["pn"].append(p_n)
        outs["pm"].append(p_m[:, 0, :N_HEADS])
        outs["ppool"].append(p_pool[:, 2 * SUBLANES - POOL_BUF:, :])
        outs["pconv"].append(p_conv[:, SUBLANES - (CONV_WIDTH - 1):, :])

        x1s, s_pool = _merge_step(xs, readout, s_rep, den_rep, dec_s, proj_qkvo, mh_g_r, ox,
                                  proj_rest, state_pool, w_m_out_b, w_pool_b, pool_scale_r,
                                  w_x_out_b, w_o_b, ln1_g_r, ln1_b_r, l, nseq // 2)
        xs, s_conv = _ffn_sample(x1s, state_conv, w_up_b, conv_w, conv_b_r, w_down_b, ln2_g_r,
                                 ln2_b_r, l, nseq // 2)
        outs["sn"].append(s_n)
        outs["sm"].append(s_m)
        outs["spool"].append(s_pool)
        outs["sconv"].append(s_conv)

    stacked = {k: jnp.stack(v) for k, v in outs.items()}
    return (xp.reshape(batch, seq, d), xs.reshape(nseq, 1, d),
            stacked["pC"], stacked["pn"], stacked["pm"], stacked["ppool"], stacked["pconv"],
            p_mem_k, p_mem_v,
            s_c, stacked["sn"], stacked["sm"], stacked["spool"], stacked["sconv"])
```

```python
import jax
import jax.numpy as jnp
from jax import lax
from jax.experimental import pallas as pl
from jax.experimental.pallas import tpu as pltpu

F32 = jnp.float32
BF16 = jnp.bfloat16

N_HEADS = 4
HEAD_DIM = 256
POOL_WINDOWS = (2, 4, 8, 16)
POOL_GROUP_DIM = 256
POOL_BUF = 15
CONV_WIDTH = 3
DEPTH = 4
PAST_LEN = 16384
ALPHA = (2.0 * DEPTH) ** 0.25
LN_EPS = 1e-5
NEG = -1e30

SUBLANES = 8
LANES = 128
TIME_TILE = 256
MLSTM_TILE = 512
HEAD_LOOKAHEAD = 1
FF_CHUNK = 256
MIX_TILE = 256
FF_TILE = 512
FF_SUB_ROWS = 256
FF_LOOKAHEAD = 3
VMEM_LIMIT = 56 * 1024 * 1024

REST_U, REST_XQ, REST_G0, REST_G1, REST_G2 = range(5)


def _params(n_axes):
    return pltpu.CompilerParams(
        dimension_semantics=("arbitrary",) * n_axes, vmem_limit_bytes=VMEM_LIMIT)


def _dot(a, b):
    return jnp.dot(a, b, preferred_element_type=F32)


def _dot_nt(a, b):
    return lax.dot_general(a, b, (((1,), (1,)), ((), ())), preferred_element_type=F32)


def _dot_tn(a, b):
    return lax.dot_general(a, b, (((0,), (0,)), ((), ())), preferred_element_type=F32)


def _layer_norm(x, g, b):
    mu = jnp.mean(x, axis=-1, keepdims=True)
    xc = x - mu
    var = jnp.mean(xc * xc, axis=-1, keepdims=True)
    return xc * lax.rsqrt(var + LN_EPS) * g + b


def _head_norm(x):
    mu = jnp.mean(x, axis=-1, keepdims=True)
    xc = x - mu
    var = jnp.mean(xc * xc, axis=-1, keepdims=True)
    return xc * lax.rsqrt(var + LN_EPS)


def _log_sigmoid(x):
    return jnp.minimum(x, 0.0) - jnp.log1p(jnp.exp(-jnp.abs(x)))


def _scan_rows(x, op):
    n = x.shape[0]
    rows = lax.broadcasted_iota(jnp.int32, x.shape, 0)
    shift = 1
    while shift < n:
        x = jnp.where(rows >= shift, op(x, pltpu.roll(x, shift, 0)), x)
        shift *= 2
    return x


def _hs(h):
    return slice(h * HEAD_DIM, (h + 1) * HEAD_DIM)


def _ln_kernel(x_ref, g_ref, b_ref, o_ref):
    o_ref[...] = _layer_norm(x_ref[...], g_ref[...], b_ref[...])


def _layer_norm_rows(x, g, b, tile):
    rows, d = x.shape
    return pl.pallas_call(
        _ln_kernel,
        grid=(rows // tile,),
        in_specs=[pl.BlockSpec((tile, d), lambda i: (i, 0)),
                  pl.BlockSpec((1, d), lambda i: (0, 0)),
                  pl.BlockSpec((1, d), lambda i: (0, 0))],
        out_specs=pl.BlockSpec((tile, d), lambda i: (i, 0)),
        out_shape=jax.ShapeDtypeStruct((rows, d), F32),
        compiler_params=_params(1),
        name="ln_rows",
    )(x, g, b)


def _mm_kernel(x_ref, w_ref, b_ref, o_ref):
    o_ref[...] = _dot(x_ref[...].astype(BF16), w_ref[...]) + b_ref[...]


def _matmul_bias(x, w_all, b_all, layer, tm, tn, name):
    rows, kdim = x.shape
    n = b_all.shape[-1]
    return pl.pallas_call(
        _mm_kernel,
        grid=(n // tn, rows // tm),
        in_specs=[pl.BlockSpec((tm, kdim), lambda j, i: (i, 0)),
                  pl.BlockSpec((None, kdim, tn), lambda j, i: (layer, 0, j)),
                  pl.BlockSpec((None, 1, tn), lambda j, i: (layer, 0, j))],
        out_specs=pl.BlockSpec((tm, tn), lambda j, i: (i, j)),
        out_shape=jax.ShapeDtypeStruct((rows, n), F32),
        compiler_params=_params(2),
        name=name,
    )(x, w_all, b_all)


def _mem_kv_kernel(x_ref, w_ref, k_ref, v_ref, kb_ref, vb_ref):
    d = N_HEADS * HEAD_DIM
    res = _dot(x_ref[...].astype(BF16), w_ref[...])
    for h in range(N_HEADS):
        k_ref[:, h, :] = res[:, _hs(h)]
        v_ref[:, h, :] = res[:, d + h * HEAD_DIM:d + (h + 1) * HEAD_DIM]
    kb_ref[...] = res[:, :d].astype(BF16)
    vb_ref[...] = res[:, d:].astype(BF16)


def _mem_kv(mem2d, w_kv, batch):
    rows, d = mem2d.shape
    n_mem = rows // batch
    depth = w_kv.shape[0]
    out_spec = pl.BlockSpec((None, None, n_mem, N_HEADS, HEAD_DIM), lambda l, b: (l, b, 0, 0, 0))
    out_sds = jax.ShapeDtypeStruct((depth, batch, n_mem, N_HEADS, HEAD_DIM), F32)
    dense_spec = pl.BlockSpec((None, n_mem, d), lambda l, b: (l, b, 0))
    dense_sds = jax.ShapeDtypeStruct((depth, rows, d), BF16)
    return pl.pallas_call(
        _mem_kv_kernel,
        grid=(depth, batch),
        in_specs=[pl.BlockSpec((n_mem, d), lambda l, b: (b, 0)),
                  pl.BlockSpec((None, d, 2 * d), lambda l, b: (l, 0, 0))],
        out_specs=[out_spec, out_spec, dense_spec, dense_spec],
        out_shape=[out_sds, out_sds, dense_sds, dense_sds],
        compiler_params=_params(2),
        name="mem_kv",
    )(mem2d, w_kv)


def _sample_attn_scores(q_row, keys):
    sub = lax.broadcasted_iota(jnp.int32, (SUBLANES, HEAD_DIM), 0)
    q = jnp.zeros((SUBLANES, HEAD_DIM), F32)
    for h in range(N_HEADS):
        q = jnp.where(sub == h, jnp.broadcast_to(q_row(h), (SUBLANES, HEAD_DIM)), q)
    keys2d = keys.reshape(keys.shape[0] * N_HEADS, HEAD_DIM).astype(BF16)
    return _dot_nt(q.astype(BF16), keys2d)


def _sample_attn_softmax(s):
    s = s * (HEAD_DIM ** -0.5)
    head = lax.broadcasted_iota(jnp.int32, s.shape, 0) % SUBLANES
    col_head = lax.broadcasted_iota(jnp.int32, s.shape, 1) % N_HEADS
    s = jnp.where((col_head == head) | (head >= N_HEADS), s, -jnp.inf)
    p = jnp.exp(s - jnp.max(s, axis=-1, keepdims=True))
    return p / jnp.sum(p, axis=-1, keepdims=True)


def _sample_attn_output(p, vals):
    vals2d = vals.reshape(vals.shape[0] * N_HEADS, HEAD_DIM).astype(BF16)
    return _dot(p.astype(BF16), vals2d)


def _mlstm_prompt_kernel(x_ref, wqkvo_ref, bqkvo_ref, wif_ref, bif_ref, wg0_ref, bg0_ref,
                         mhg_ref, wmo_ref, xq_ref, sk_ref, sv_ref,
                         y_ref, c_ref, n_ref, m_ref, ox_ref, s_scr):
    t = pl.program_id(1)
    d = N_HEADS * HEAD_DIM
    ride = sk_ref.shape[0]
    seq0 = (pl.program_id(0) * pl.num_programs(1) + t) * ride

    @pl.when(t == 0)
    def _():
        c_ref[...] = jnp.zeros(c_ref.shape, F32)
        n_ref[...] = jnp.zeros(n_ref.shape, F32)
        m_ref[...] = jnp.zeros(m_ref.shape, F32)

    for i in range(ride):
        s_scr[i * SUBLANES:(i + 1) * SUBLANES, :] = _sample_attn_scores(
            lambda h, i=i: xq_ref[pl.ds(seq0 + i, 1), _hs(h)], sk_ref[i])

    lc = TIME_TILE
    causal = (lax.broadcasted_iota(jnp.int32, (lc, lc), 1)
              <= lax.broadcasted_iota(jnp.int32, (lc, lc), 0))

    for chunk, r0 in enumerate(range(0, x_ref.shape[0], lc)):
        rows = slice(r0, r0 + lc)
        xb = x_ref[rows, :].astype(BF16)
        gates = _dot(xb, wif_ref[...]) + bif_ref[...]
        gi, gf = gates[:, 0:LANES], gates[:, LANES:2 * LANES]
        lf = _log_sigmoid(gf)
        bcum = _scan_rows(lf, jnp.add)
        a = gi - bcum
        m0 = m_ref[0]
        mt = bcum + jnp.maximum(m0, _scan_rows(a, jnp.maximum))
        inter = jnp.exp(m0 + bcum - mt)
        bm = bcum - mt
        emt = jnp.exp(-mt)
        b_last = bcum[lc - 1:lc, :]
        m_last = mt[lc - 1:lc, :]
        ws = jnp.exp(a + b_last - m_last)
        decay = jnp.exp(m0 + b_last - m_last)
        a_t = a.T

        def head_proj(h, xb=xb):
            def proj(part):
                cols = slice(part * d + h * HEAD_DIM, part * d + (h + 1) * HEAD_DIM)
                return _dot(xb, wqkvo_ref[:, cols]) + bqkvo_ref[:, cols]
            return proj(0), proj(1) * (HEAD_DIM ** -0.5), proj(2), proj(3)

        hn_heads = []
        projs = [head_proj(h) for h in range(HEAD_LOOKAHEAD)]
        for h in range(N_HEADS):
            if h + HEAD_LOOKAHEAD < N_HEADS:
                projs.append(head_proj(h + HEAD_LOOKAHEAD))
            elif h + HEAD_LOOKAHEAD == N_HEADS:
                g0 = _dot(xb, wg0_ref[...]) + bg0_ref[...]
            q, k, v, o = projs[h]
            qb = q.astype(BF16)
            vb = v.astype(BF16)
            col = slice(h, h + 1)
            logd = a_t[h:h + 1, :] + bm[:, col]
            dmat = jnp.exp(jnp.where(causal, logd, NEG))
            s = _dot_nt(qb, k.astype(BF16)) * dmat
            c_old = c_ref[0, h]
            n_old = n_ref[0, h:h + 1, :]
            inter_h = inter[:, col]
            num = _dot(s.astype(BF16), vb) + inter_h * _dot(qb, c_old.astype(BF16))
            den = (jnp.sum(s, axis=1, keepdims=True)
                   + inter_h * jnp.sum(q * n_old, axis=1, keepdims=True))
            ht = num / jnp.maximum(jnp.abs(den), emt[:, col])
            kw = k * ws[:, col]
            c_ref[0, h] = decay[:, col] * c_old + _dot_tn(kw.astype(BF16), vb)
            n_ref[0, h:h + 1, :] = decay[:, col] * n_old + jnp.sum(kw, axis=0, keepdims=True)
            hn = _head_norm(jax.nn.sigmoid(o) * ht) * mhg_ref[:, _hs(h)]
            hn_heads.append(hn.astype(BF16))
            if chunk == 0 and h == 0:
                s_scr[...] = _sample_attn_softmax(s_scr[...])
        m_ref[0] = m_last

        if chunk == 0:
            for i in range(ride):
                out = _sample_attn_output(s_scr[i * SUBLANES:(i + 1) * SUBLANES, :], sv_ref[i])
                for h in range(N_HEADS):
                    ox_ref[pl.ds(seq0 + i, 1), _hs(h)] = out[h:h + 1, :]

        ym = _dot(jnp.concatenate(hn_heads, axis=1), wmo_ref[...])
        y_ref[rows, :] = jax.nn.sigmoid(g0) * ym


def _mlstm_prompt(x2d, batch, w_qkvo, b_qkvo, w_if, b_if, w_rest, b_rest, mh_g, w_m_out,
                  proj_rest_s, mem_k, mem_v, layer):
    rows, d = x2d.shape
    nt = rows // batch // MLSTM_TILE
    wide = N_HEADS * d
    nseq, n_mem = mem_k.shape[1], mem_k.shape[2]
    ride = nseq // (batch * nt)
    assert ride * batch * nt == nseq

    def lw(shape, col):
        return pl.BlockSpec((None,) + shape, lambda b, t: (layer, 0, col),
                            pipeline_mode=pl.Buffered(1))

    kv_spec = pl.BlockSpec((None, ride, n_mem, N_HEADS, HEAD_DIM),
                           lambda b, t: (layer, b * nt + t, 0, 0, 0))
    return pl.pallas_call(
        _mlstm_prompt_kernel,
        grid=(batch, nt),
        in_specs=[pl.BlockSpec((MLSTM_TILE, d), lambda b, t: (b * nt + t, 0)),
                  lw((d, wide), 0), lw((1, wide), 0),
                  lw((d, 2 * LANES), 0), lw((1, 2 * LANES), 0),
                  lw((d, d), REST_G0), lw((1, d), REST_G0),
                  lw((1, d), 0), lw((d, d), 0),
                  pl.BlockSpec((nseq, d), lambda b, t: (0, REST_XQ)), kv_spec, kv_spec],
        out_specs=[pl.BlockSpec((MLSTM_TILE, d), lambda b, t: (b * nt + t, 0)),
                   pl.BlockSpec((1, N_HEADS, HEAD_DIM, HEAD_DIM), lambda b, t: (b, 0, 0, 0)),
                   pl.BlockSpec((1, N_HEADS, HEAD_DIM), lambda b, t: (b, 0, 0)),
                   pl.BlockSpec((1, 1, LANES), lambda b, t: (b, 0, 0)),
                   pl.BlockSpec((nseq, d), lambda b, t: (0, 0))],
        out_shape=[jax.ShapeDtypeStruct((rows, d), F32),
                   jax.ShapeDtypeStruct((batch, N_HEADS, HEAD_DIM, HEAD_DIM), F32),
                   jax.ShapeDtypeStruct((batch, N_HEADS, HEAD_DIM), F32),
                   jax.ShapeDtypeStruct((batch, 1, LANES), F32),
                   jax.ShapeDtypeStruct((nseq, d), F32)],
        scratch_shapes=[pltpu.VMEM((ride * SUBLANES, N_HEADS * n_mem), F32)],
        compiler_params=_params(2),
        name="mlstm_prompt",
    )(x2d, w_qkvo, b_qkvo, w_if, b_if, w_rest, b_rest, mh_g, w_m_out, proj_rest_s, mem_k, mem_v)


def _pool_window_sums(ext, window):
    s, span = ext, 1
    while span < window:
        s = s + pltpu.roll(s, span, 0)
        span *= 2
    return s


def _mix_prompt_kernel(x_ref, ymg_ref, mk_ref, mv_ref, wu_ref, bu_ref, wxq_ref, bxq_ref,
                       wg1_ref, bg1_ref, wg2_ref, bg2_ref, wpool_ref, pscale_ref, wxo_ref,
                       wo_ref, lng_ref, lnb_ref, sq_ref, sv_ref, skw_ref, sdec_ref, sc_ref,
                       *refs):
    x1_ref, pool_ref, sr_ref, scn_ref, ext_scr = refs[-5:]
    t = pl.program_id(1)
    tt = x_ref.shape[0]
    carry = 2 * SUBLANES
    ride = sc_ref.shape[0]
    seq0 = (pl.program_id(0) * pl.num_programs(1) + t) * ride

    @pl.when(t == 0)
    def _():
        ext_scr[0:carry, :] = jnp.zeros((carry, ext_scr.shape[1]), F32)

    for sub, r0 in enumerate(range(0, tt, TIME_TILE)):
        rows = slice(r0, r0 + TIME_TILE)
        for i in range(sub * ride // (tt // TIME_TILE), (sub + 1) * ride // (tt // TIME_TILE)):
            _sample_memory_update(seq0 + i, i, sq_ref, sv_ref, skw_ref, sdec_ref, sc_ref,
                                  scn_ref, sr_ref)

        x = x_ref[rows, :]
        xb = x.astype(BF16)

        u = _dot(xb, wu_ref[...]) + bu_ref[...]
        xq = (_dot(xb, wxq_ref[...]) + bxq_ref[...]).astype(BF16)

        ext_scr[carry + r0:carry + r0 + TIME_TILE, :] = u
        pos = t * tt + r0 + lax.broadcasted_iota(jnp.int32, (TIME_TILE, 1), 0)
        scores = [_dot_nt(xq[:, _hs(h)], mk_ref[:, _hs(h)]) for h in range(N_HEADS)]
        yp_groups = []
        for g, window in enumerate(POOL_WINDOWS):
            cols = slice(g * POOL_GROUP_DIM, (g + 1) * POOL_GROUP_DIM)
            ext = ext_scr[r0:r0 + carry + TIME_TILE, cols]
            wsum = _pool_window_sums(ext, window)[carry:, :]
            cnt = jnp.minimum(window, pos + 1).astype(F32)
            dlt = wsum / cnt - u[:, cols]
            yp_groups.append(_dot(dlt.astype(BF16), wpool_ref[g]))
        if r0 + TIME_TILE == tt:
            ext_scr[0:carry, :] = u[TIME_TILE - carry:, :]
            pool_ref[0] = u[TIME_TILE - carry:, :]
        g1 = _dot(xb, wg1_ref[...]) + bg1_ref[...]

        heads = []
        for h in range(N_HEADS):
            s = scores[h] * (HEAD_DIM ** -0.5)
            p = jnp.exp(s - jnp.max(s, axis=-1, keepdims=True))
            p = p / jnp.sum(p, axis=-1, keepdims=True)
            heads.append(_dot(p.astype(BF16), mv_ref[:, _hs(h)]).astype(BF16))
        g2 = _dot(xb, wg2_ref[...]) + bg2_ref[...]
        yp = jnp.concatenate(yp_groups, axis=1) * pscale_ref[...]
        yx = _dot(jnp.concatenate(heads, axis=1), wxo_ref[...])
        mix = ymg_ref[rows, :] + jax.nn.sigmoid(g1) * yp + jax.nn.sigmoid(g2) * yx
        y = ALPHA * x + _dot(mix.astype(BF16), wo_ref[...])
        x1_ref[rows, :] = _layer_norm(y, lng_ref[...], lnb_ref[...])


def _mix_prompt(x2d, ymg, mk, mv, batch, w_rest, b_rest, w_pool, pool_scale, w_x_out, w_o,
                ln_g, ln_b, proj_qkvo_s, kw_s, dec_s, state_c, prev_c, layer):
    rows, d = x2d.shape
    nt = rows // batch // MIX_TILE
    n_mem = mk.shape[1] // batch
    carry = 2 * SUBLANES
    depth, nseq = state_c.shape[:2]
    ride = nseq // (batch * nt)
    assert ride * batch * nt == nseq and ride % (MIX_TILE // TIME_TILE) == 0

    def lw(shape, col):
        return pl.BlockSpec((None,) + shape, lambda b, t: (layer, 0, col),
                            pipeline_mode=pl.Buffered(1))

    def sample_rows(col):
        return pl.BlockSpec((nseq, d), lambda b, t: (0, col))

    row_spec = pl.BlockSpec((MIX_TILE, d), lambda b, t: (b * nt + t, 0))
    mem_spec = pl.BlockSpec((None, n_mem, d), lambda b, t: (layer, b, 0))
    c_block = (ride, N_HEADS, HEAD_DIM, HEAD_DIM)
    c_spec = pl.BlockSpec((None,) + c_block, lambda b, t: (layer, b * nt + t, 0, 0, 0))
    if prev_c is None:
        c_out_spec = pl.BlockSpec((depth,) + c_block, lambda b, t: (0, b * nt + t, 0, 0, 0))
        chain_specs, chain_args, aliases = [], [], {}
    else:
        c_out_spec = c_spec
        chain_specs, chain_args, aliases = [pl.BlockSpec(memory_space=pl.ANY)], [prev_c], {23: 3}
    return pl.pallas_call(
        _mix_prompt_kernel,
        grid=(batch, nt),
        in_specs=[row_spec, row_spec, mem_spec, mem_spec,
                  lw((d, d), REST_U), lw((1, d), REST_U),
                  lw((d, d), REST_XQ), lw((1, d), REST_XQ),
                  lw((d, d), REST_G1), lw((1, d), REST_G1),
                  lw((d, d), REST_G2), lw((1, d), REST_G2),
                  pl.BlockSpec((None, len(POOL_WINDOWS), POOL_GROUP_DIM, POOL_GROUP_DIM),
                               lambda b, t: (layer, 0, 0, 0)),
                  lw((1, d), 0), lw((d, d), 0), lw((d, d), 0), lw((1, d), 0), lw((1, d), 0),
                  sample_rows(0), sample_rows(2), sample_rows(0), sample_rows(0), c_spec]
        + chain_specs,
        out_specs=[row_spec, pl.BlockSpec((1, carry, d), lambda b, t: (b, 0, 0)),
                   sample_rows(0), c_out_spec],
        out_shape=[jax.ShapeDtypeStruct((rows, d), F32),
                   jax.ShapeDtypeStruct((batch, carry, d), F32),
                   jax.ShapeDtypeStruct((nseq, d), F32),
                   jax.ShapeDtypeStruct(state_c.shape, F32)],
        input_output_aliases=aliases,
        scratch_shapes=[pltpu.VMEM((carry + MIX_TILE, d), F32)],
        compiler_params=_params(2),
        name="mix_prompt",
    )(x2d, ymg, mk, mv, w_rest, b_rest, w_rest, b_rest, w_rest, b_rest, w_rest, b_rest,
      w_pool, pool_scale, w_x_out, w_o, ln_g, ln_b,
      proj_qkvo_s, proj_qkvo_s, kw_s, dec_s, state_c, *chain_args)


def _ffn_chunk(xb, j, d_ff, wup_ref, cw_ref, cb_ref, wdn_ref, conv_taps):
    halves = _ffn_up(xb, j, d_ff, wup_ref)
    return _ffn_gate_down(halves, j, d_ff, cw_ref, cb_ref, wdn_ref, conv_taps), halves


def _ffn_up(xb, j, d_ff, wup_ref):
    return [_dot(xb, wup_ref[:, base + j * FF_CHUNK:base + (j + 1) * FF_CHUNK])
            for base in (0, d_ff)]


def _ffn_gate_down(halves, j, d_ff, cw_ref, cb_ref, wdn_ref, conv_taps):
    conv = []
    for base, hup in zip((0, d_ff), halves):
        cols = slice(base + j * FF_CHUNK, base + (j + 1) * FF_CHUNK)
        prev2, prev1 = conv_taps(hup, cols)
        conv.append(prev2 * cw_ref[0:1, cols] + prev1 * cw_ref[1:2, cols]
                    + hup * cw_ref[2:3, cols] + cb_ref[:, cols])
    act = jax.nn.gelu(conv[0]) * conv[1]
    return _dot(act.astype(BF16), wdn_ref[j * FF_CHUNK:(j + 1) * FF_CHUNK, :])


def _ffn_prompt_kernel(x_ref, wup_ref, cw_ref, cb_ref, wdn_ref, lng_ref, lnb_ref,
                       o_ref, conv_ref, ext_scr):
    t = pl.program_id(1)
    tt = x_ref.shape[0]
    d_ff = wdn_ref.shape[0]
    n_chunks = d_ff // FF_CHUNK

    @pl.when(t == 0)
    def _():
        ext_scr[0:SUBLANES, :] = jnp.zeros((SUBLANES, ext_scr.shape[1]), F32)

    items = [(r0, j) for r0 in range(0, tt, FF_SUB_ROWS) for j in range(n_chunks)]
    xs = {r0: x_ref[r0:r0 + FF_SUB_ROWS, :] for r0 in range(0, tt, FF_SUB_ROWS)}
    xbs = {r0: xs[r0].astype(BF16) for r0 in xs}

    def up(item):
        r0, j = item
        return _ffn_up(xbs[r0], j, d_ff, wup_ref)

    ups = [up(item) for item in items[:FF_LOOKAHEAD]]
    acc = None
    for i, (r0, j) in enumerate(items):
        if i + FF_LOOKAHEAD < len(items):
            ups.append(up(items[i + FF_LOOKAHEAD]))
        halves = ups[i]
        ups[i] = None
        lo = SUBLANES + r0

        def conv_taps(hup, cols):
            ext_scr[lo:lo + FF_SUB_ROWS, cols] = hup
            return (ext_scr[lo - 2:lo - 2 + FF_SUB_ROWS, cols],
                    ext_scr[lo - 1:lo - 1 + FF_SUB_ROWS, cols])

        part = _ffn_gate_down(halves, j, d_ff, cw_ref, cb_ref, wdn_ref, conv_taps)
        acc = part if j == 0 else part + acc
        if r0 + FF_SUB_ROWS == tt:
            for base, hup in zip((0, d_ff), halves):
                cols = slice(base + j * FF_CHUNK, base + (j + 1) * FF_CHUNK)
                ext_scr[0:SUBLANES, cols] = hup[FF_SUB_ROWS - SUBLANES:, :]
                conv_ref[0, :, cols] = hup[FF_SUB_ROWS - SUBLANES:, :]
        if j == n_chunks - 1:
            o_ref[r0:r0 + FF_SUB_ROWS, :] = _layer_norm(ALPHA * xs[r0] + acc, lng_ref[...],
                                                       lnb_ref[...])


def _ffn_prompt(x2d, batch, w_up, conv_w, conv_b, w_down, ln_g, ln_b, layer):
    rows, d = x2d.shape
    nt = rows // batch // FF_TILE
    d_ff = w_down.shape[1]

    def lw(shape):
        return pl.BlockSpec((None,) + shape, lambda b, t: (layer, 0, 0),
                            pipeline_mode=pl.Buffered(1))

    row_spec = pl.BlockSpec((FF_TILE, d), lambda b, t: (b * nt + t, 0))
    return pl.pallas_call(
        _ffn_prompt_kernel,
        grid=(batch, nt),
        in_specs=[row_spec, lw((d, 2 * d_ff)), lw((CONV_WIDTH, 2 * d_ff)), lw((1, 2 * d_ff)),
                  lw((d_ff, d)), lw((1, d)), lw((1, d))],
        out_specs=[row_spec, pl.BlockSpec((1, SUBLANES, 2 * d_ff), lambda b, t: (b, 0, 0))],
        out_shape=[jax.ShapeDtypeStruct((rows, d), F32),
                   jax.ShapeDtypeStruct((batch, SUBLANES, 2 * d_ff), F32)],
        scratch_shapes=[pltpu.VMEM((SUBLANES + FF_TILE, 2 * d_ff), F32)],
        compiler_params=_params(2),
        name="ffn_prompt",
    )(x2d, w_up, conv_w, conv_b, w_down, ln_g, ln_b)


def _ffn_sample_kernel(x_ref, st_ref, wup_ref, cw_ref, cb_ref, wdn_ref, lng_ref, lnb_ref,
                       o_ref, stn_ref):
    d_ff = wdn_ref.shape[0]
    x = x_ref[...]
    xb = x.astype(BF16)

    def conv_taps(hup, cols):
        return st_ref[:, 0, cols], st_ref[:, 1, cols]

    acc = None
    for j in range(d_ff // FF_CHUNK):
        part, halves = _ffn_chunk(xb, j, d_ff, wup_ref, cw_ref, cb_ref, wdn_ref, conv_taps)
        for base, hup in zip((0, d_ff), halves):
            cols = slice(base + j * FF_CHUNK, base + (j + 1) * FF_CHUNK)
            stn_ref[:, 0, cols] = st_ref[:, 1, cols]
            stn_ref[:, 1, cols] = hup
        acc = part if acc is None else acc + part
    o_ref[...] = _layer_norm(ALPHA * x + acc, lng_ref[...], lnb_ref[...])


def _ffn_sample(x, conv_state, w_up, conv_w, conv_b, w_down, ln_g, ln_b, layer, tile):
    rows, d = x.shape
    d_ff = w_down.shape[1]
    st_shape = conv_state.shape[2:]

    def lw(shape):
        return pl.BlockSpec((None,) + shape, lambda i: (layer, 0, 0))

    return pl.pallas_call(
        _ffn_sample_kernel,
        grid=(rows // tile,),
        in_specs=[pl.BlockSpec((tile, d), lambda i: (i, 0)),
                  pl.BlockSpec((None, tile) + st_shape, lambda i: (layer, i, 0, 0)),
                  lw((d, 2 * d_ff)), lw((CONV_WIDTH, 2 * d_ff)), lw((1, 2 * d_ff)),
                  lw((d_ff, d)), lw((1, d)), lw((1, d))],
        out_specs=[pl.BlockSpec((tile, d), lambda i: (i, 0)),
                   pl.BlockSpec((tile,) + st_shape, lambda i: (i, 0, 0))],
        out_shape=[jax.ShapeDtypeStruct((rows, d), F32),
                   jax.ShapeDtypeStruct((rows,) + st_shape, F32)],
        compiler_params=_params(1),
        name="ffn_sample",
    )(x, conv_state, w_up, conv_w, conv_b, w_down, ln_g, ln_b)


def _mlstm_pre_kernel(q_ref, k_ref, x_ref, wif_ref, bif_ref, n_ref, m_ref,
                      kw_ref, dec_ref, s_ref, den_ref, nn_ref, mn_ref):
    gates = _dot(x_ref[...].astype(BF16), wif_ref[...]) + bif_ref[...]
    gi = gates[:, 0:N_HEADS]
    lf = _log_sigmoid(gates[:, LANES:LANES + N_HEADS])
    m0 = m_ref[...]
    mt = jnp.maximum(m0 + lf, gi)
    inter_all = jnp.exp(m0 + lf - mt)
    dm_all = jnp.exp(gi - mt)
    emt_all = jnp.exp(-mt)
    mn_ref[...] = mt
    for h in range(N_HEADS):
        col = slice(h, h + 1)
        q = q_ref[:, _hs(h)]
        k = k_ref[:, _hs(h)] * (HEAD_DIM ** -0.5)
        n_old = n_ref[:, h, :]
        inter, dm = inter_all[:, col], dm_all[:, col]
        s = jnp.sum(q * k, axis=1, keepdims=True) * dm
        den = s + inter * jnp.sum(q * n_old, axis=1, keepdims=True)
        kw = k * dm
        kw_ref[:, _hs(h)] = kw
        dec_ref[:, _hs(h)] = jnp.broadcast_to(inter, kw.shape)
        s_ref[:, _hs(h)] = jnp.broadcast_to(s, kw.shape)
        den_ref[:, _hs(h)] = jnp.broadcast_to(jnp.maximum(jnp.abs(den), emt_all[:, col]), kw.shape)
        nn_ref[:, h, :] = inter * n_old + kw


def _mlstm_pre(proj_qkvo, x, w_if, b_if, state_n, state_m, layer, tile):
    nseq = proj_qkvo.shape[0]
    d = N_HEADS * HEAD_DIM

    def rows_at(col):
        return pl.BlockSpec((tile, d), lambda i: (i, col))

    wide_sds = jax.ShapeDtypeStruct((nseq, d), F32)
    return pl.pallas_call(
        _mlstm_pre_kernel,
        grid=(nseq // tile,),
        in_specs=[rows_at(0), rows_at(1), rows_at(0),
                  pl.BlockSpec((None, d, 2 * LANES), lambda i: (layer, 0, 0)),
                  pl.BlockSpec((None, 1, 2 * LANES), lambda i: (layer, 0, 0)),
                  pl.BlockSpec((None, tile, N_HEADS, HEAD_DIM), lambda i: (layer, i, 0, 0)),
                  pl.BlockSpec((None, tile, N_HEADS), lambda i: (layer, i, 0))],
        out_specs=[rows_at(0)] * 4 + [pl.BlockSpec((tile, N_HEADS, HEAD_DIM), lambda i: (i, 0, 0)),
                                      pl.BlockSpec((tile, N_HEADS), lambda i: (i, 0))],
        out_shape=[wide_sds] * 4 + [jax.ShapeDtypeStruct((nseq, N_HEADS, HEAD_DIM), F32),
                                    jax.ShapeDtypeStruct((nseq, N_HEADS), F32)],
        compiler_params=_params(1),
        name="mlstm_pre",
    )(proj_qkvo, proj_qkvo, x, w_if, b_if, state_n, state_m)


def _sample_memory_update(seq, i, q_ref, v_ref, kw_ref, dec_ref, c_ref, cn_ref, r_ref):
    mxu_rows = 2 * SUBLANES
    row0 = lax.broadcasted_iota(jnp.int32, (mxu_rows, HEAD_DIM), 0) == 0
    row = pl.ds(seq, 1)
    for h in range(N_HEADS):
        qrow = jnp.broadcast_to(q_ref[row, _hs(h)], (mxu_rows, HEAD_DIM)).astype(BF16)
        c_old = c_ref[i, h]
        r_ref[row, _hs(h)] = _dot(qrow, c_old.astype(BF16))[0:1, :]
        kw = jnp.where(row0, jnp.broadcast_to(kw_ref[row, _hs(h)], (mxu_rows, HEAD_DIM)), 0.0)
        vrow = jnp.broadcast_to(v_ref[row, _hs(h)], (mxu_rows, HEAD_DIM))
        c_new = dec_ref[row, _hs(h)] * c_old + _dot_tn(kw.astype(BF16), vrow.astype(BF16))
        if len(cn_ref.shape) == 5:
            for layer_slot in range(cn_ref.shape[0]):
                cn_ref[layer_slot, i, h] = c_new
        else:
            cn_ref[i, h] = c_new


def _merge_step_kernel(x_ref, r_ref, s_ref, den_ref, dec_ref, v_ref, o_ref, mhg_ref, ox_ref,
                       u_ref, g0_ref, g1_ref, g2_ref, pool_ref,
                       wmo_ref, wpool_ref, pscale_ref, wxo_ref, wo_ref, lng_ref, lnb_ref,
                       x1_ref, pooln_ref):
    x = x_ref[...]
    u = u_ref[...]
    ht = (s_ref[...] * v_ref[...] + dec_ref[...] * r_ref[...]) / den_ref[...]
    hc = jax.nn.sigmoid(o_ref[...]) * ht
    hn = jnp.concatenate([_head_norm(hc[:, _hs(h)]) for h in range(N_HEADS)], axis=1)
    hn = hn * mhg_ref[...]
    for r in range(1, POOL_BUF):
        pooln_ref[:, r - 1, :] = pool_ref[:, r, :]
    pooln_ref[:, POOL_BUF - 1, :] = u
    yp_groups = []
    for g, window in enumerate(POOL_WINDOWS):
        cols = slice(g * POOL_GROUP_DIM, (g + 1) * POOL_GROUP_DIM)
        wsum = u[:, cols]
        for back in range(1, window):
            wsum = wsum + pool_ref[:, POOL_BUF - back, cols]
        cnt = float(min(window, PAST_LEN + 1))
        dlt = wsum / cnt - u[:, cols]
        yp_groups.append(_dot(dlt.astype(BF16), wpool_ref[g]))
    yp = jnp.concatenate(yp_groups, axis=1) * pscale_ref[...]
    ym = _dot(hn.astype(BF16), wmo_ref[...])
    yx = _dot(ox_ref[...].astype(BF16), wxo_ref[...])
    mix = (jax.nn.sigmoid(g0_ref[...]) * ym + jax.nn.sigmoid(g1_ref[...]) * yp
           + jax.nn.sigmoid(g2_ref[...]) * yx)
    y = ALPHA * x + _dot(mix.astype(BF16), wo_ref[...])
    x1_ref[...] = _layer_norm(y, lng_ref[...], lnb_ref[...])


def _merge_step(x, readout, s_rep, den_rep, dec_rep, proj_qkvo, mh_g, ox, proj_rest, pool_state,
                w_m_out, w_pool, pool_scale, w_x_out, w_o, ln_g, ln_b, layer, tile):
    rows, d = x.shape
    st_shape = pool_state.shape[2:]

    def lw(shape):
        return pl.BlockSpec((None,) + shape, lambda i: (layer,) + (0,) * len(shape))

    def rows_at(col):
        return pl.BlockSpec((tile, d), lambda i: (i, col))

    return pl.pallas_call(
        _merge_step_kernel,
        grid=(rows // tile,),
        in_specs=[rows_at(0), rows_at(0), rows_at(0), rows_at(0), rows_at(0),
                  rows_at(2), rows_at(3), lw((1, d)), rows_at(0),
                  rows_at(REST_U), rows_at(REST_G0), rows_at(REST_G1), rows_at(REST_G2),
                  pl.BlockSpec((None, tile) + st_shape, lambda i: (layer, i, 0, 0)),
                  lw((d, d)), lw((len(POOL_WINDOWS), POOL_GROUP_DIM, POOL_GROUP_DIM)),
                  lw((1, d)), lw((d, d)), lw((d, d)), lw((1, d)), lw((1, d))],
        out_specs=[rows_at(0), pl.BlockSpec((tile,) + st_shape, lambda i: (i, 0, 0))],
        out_shape=[jax.ShapeDtypeStruct((rows, d), F32),
                   jax.ShapeDtypeStruct((rows,) + st_shape, F32)],
        compiler_params=_params(1),
        name="merge_step",
    )(x, readout, s_rep, den_rep, dec_rep, proj_qkvo, proj_qkvo, mh_g, ox,
      proj_rest, proj_rest, proj_rest, proj_rest, pool_state, w_m_out, w_pool,
      pool_scale, w_x_out, w_o, ln_g, ln_b)


def kernel(x_prompt, mem_prompt, x_sample, cache_mem_k, cache_mem_v, state_C, state_n, state_m,
           state_pool, state_conv, ln_in_g, ln_in_b, w_in, b_in, mh_g, w_m_out, w_pool,
           pool_scale, w_mem_kv, w_x_out, w_o, ln1_g, ln1_b, w_up, conv_w, conv_b, w_down,
           ln2_g, ln2_b):
    batch, seq, d = x_prompt.shape
    nseq = x_sample.shape[0]
    n_mem = mem_prompt.shape[1]
    depth = w_in.shape[0]
    d_ff = w_down.shape[1]
    wide = N_HEADS * d
    gate_off = wide
    rest_off = wide + 2 * N_HEADS
    assert d == N_HEADS * HEAD_DIM and d_ff % FF_CHUNK == 0
    assert all(seq % tile == 0 for tile in (MLSTM_TILE, MIX_TILE, FF_TILE))
    assert depth == DEPTH and x_sample.shape[1] == 1

    def row(v):
        return v.reshape(depth, 1, v.shape[-1])

    w_qkvo = w_in.astype(BF16)
    w_rest = w_qkvo[:, :, rest_off:]
    b_qkvo, b_rest = row(b_in[:, :wide]), row(b_in[:, rest_off:])
    lane_pad = ((0, 0), (0, 0), (0, LANES - N_HEADS))
    w_if = jnp.concatenate(
        [jnp.pad(w_in[:, :, gate_off:gate_off + N_HEADS], lane_pad),
         jnp.pad(w_in[:, :, gate_off + N_HEADS:rest_off], lane_pad)], axis=2).astype(BF16)
    b_if = row(jnp.concatenate(
        [jnp.pad(b_in[:, gate_off:gate_off + N_HEADS], lane_pad[1:]),
         jnp.pad(b_in[:, gate_off + N_HEADS:rest_off], lane_pad[1:])], axis=1))
    w_m_out_b, w_pool_b, w_x_out_b, w_o_b = (w.astype(BF16) for w in (w_m_out, w_pool, w_x_out, w_o))
    w_up_b, w_down_b = w_up.astype(BF16), w_down.astype(BF16)
    w_kv_b = w_mem_kv.astype(BF16)
    mh_g_r, pool_scale_r, conv_b_r = row(mh_g), row(pool_scale), row(conv_b)
    ln1_g_r, ln1_b_r, ln2_g_r, ln2_b_r = row(ln1_g), row(ln1_b), row(ln2_g), row(ln2_b)

    xp = x_prompt.reshape(batch * seq, d)
    xs = x_sample.reshape(nseq, d)
    mem2d = mem_prompt.reshape(batch * n_mem, d)

    xp = _layer_norm_rows(xp, ln_in_g.reshape(1, d), ln_in_b.reshape(1, d), 2 * TIME_TILE)
    xs = _layer_norm_rows(xs, ln_in_g.reshape(1, d), ln_in_b.reshape(1, d), nseq)

    outs = {name: [] for name in ("pC", "pn", "pm", "ppool", "pconv", "sn", "sm", "spool", "sconv")}
    p_mem_k, p_mem_v, mk_b, mv_b = _mem_kv(mem2d, w_kv_b, batch)
    s_c = None
    for l in range(depth):
        proj_qkvo = _matmul_bias(xs, w_qkvo, b_qkvo, l, nseq, d, "proj_sample")
        proj_rest = _matmul_bias(xs, w_rest, b_rest, l, nseq, d, "proj_sample")
        kw_s, dec_s, s_rep, den_rep, s_n, s_m = _mlstm_pre(proj_qkvo, xs, w_if, b_if, state_n,
                                                          state_m, l, nseq // 2)

        ymg, p_c, p_n, p_m, ox = _mlstm_prompt(xp, batch, w_qkvo, b_qkvo, w_if, b_if, w_rest,
                                               b_rest, mh_g_r, w_m_out_b, proj_rest,
                                               cache_mem_k, cache_mem_v, l)
        x1, p_pool, readout, s_c = _mix_prompt(xp, ymg, mk_b, mv_b, batch, w_rest, b_rest,
                                               w_pool_b, pool_scale_r, w_x_out_b, w_o_b, ln1_g_r,
                                               ln1_b_r, proj_qkvo, kw_s, dec_s, state_C, s_c, l)
        xp, p_conv = _ffn_prompt(x1, batch, w_up_b, conv_w, conv_b_r, w_down_b, ln2_g_r,
                                 ln2_b_r, l)
        outs["pC"].append(p_c)
        outs["pn"].append(p_n)
        outs["pm"].append(p_m[:, 0, :N_HEADS])
        outs["ppool"].append(p_pool[:, 2 * SUBLANES - POOL_BUF:, :])
        outs["pconv"].append(p_conv[:, SUBLANES - (CONV_WIDTH - 1):, :])

        x1s, s_pool = _merge_step(xs, readout, s_rep, den_rep, dec_s, proj_qkvo, mh_g_r, ox,
                                  proj_rest, state_pool, w_m_out_b, w_pool_b, pool_scale_r,
                                  w_x_out_b, w_o_b, ln1_g_r, ln1_b_r, l, nseq // 2)
        xs, s_conv = _ffn_sample(x1s, state_conv, w_up_b, conv_w, conv_b_r, w_down_b, ln2_g_r,
                                 ln2_b_r, l, nseq // 2)
        outs["sn"].append(s_n)
        outs["sm"].append(s_m)
        outs["spool"].append(s_pool)
        outs["sconv"].append(s_conv)

    stacked = {k: jnp.stack(v) for k, v in outs.items()}
    return (xp.reshape(batch, seq, d), xs.reshape(nseq, 1, d),
            stacked["pC"], stacked["pn"], stacked["pm"], stacked["ppool"], stacked["pconv"],
            p_mem_k, p_mem_v,
            s_c, stacked["sn"], stacked["sm"], stacked["spool"], stacked["sconv"])
```

```python
import jax
import jax.numpy as jnp
from jax import lax
from jax.experimental import pallas as pl
from jax.experimental.pallas import tpu as pltpu

F32 = jnp.float32
BF16 = jnp.bfloat16

N_HEADS = 4
HEAD_DIM = 256
POOL_WINDOWS = (2, 4, 8, 16)
POOL_GROUP_DIM = 256
POOL_BUF = 15
CONV_WIDTH = 3
DEPTH = 4
PAST_LEN = 16384
ALPHA = (2.0 * DEPTH) ** 0.25
LN_EPS = 1e-5
NEG = -1e30

SUBLANES = 8
LANES = 128
TIME_TILE = 256
MLSTM_TILE = 512
HEAD_LOOKAHEAD = 1
FF_CHUNK = 256
MIX_TILE = 512
MIX_TILE_FIRST = 256
FF_TILE = 512
FF_SUB_ROWS = 256
FF_LOOKAHEAD = 4
VMEM_LIMIT = 56 * 1024 * 1024

REST_U, REST_XQ, REST_G0, REST_G1, REST_G2 = range(5)


def _params(n_axes):
    return pltpu.CompilerParams(
        dimension_semantics=("arbitrary",) * n_axes, vmem_limit_bytes=VMEM_LIMIT)


def _dot(a, b):
    return jnp.dot(a, b, preferred_element_type=F32)


def _dot_nt(a, b):
    return lax.dot_general(a, b, (((1,), (1,)), ((), ())), preferred_element_type=F32)


def _dot_tn(a, b):
    return lax.dot_general(a, b, (((0,), (0,)), ((), ())), preferred_element_type=F32)


def _layer_norm(x, g, b):
    mu = jnp.mean(x, axis=-1, keepdims=True)
    xc = x - mu
    var = jnp.mean(xc * xc, axis=-1, keepdims=True)
    return xc * lax.rsqrt(var + LN_EPS) * g + b


def _head_norm(x):
    mu = jnp.mean(x, axis=-1, keepdims=True)
    xc = x - mu
    var = jnp.mean(xc * xc, axis=-1, keepdims=True)
    return xc * lax.rsqrt(var + LN_EPS)


def _log_sigmoid(x):
    return jnp.minimum(x, 0.0) - jnp.log1p(jnp.exp(-jnp.abs(x)))


def _scan_rows(x, op):
    n = x.shape[0]
    rows = lax.broadcasted_iota(jnp.int32, x.shape, 0)
    shift = 1
    while shift < n:
        x = jnp.where(rows >= shift, op(x, pltpu.roll(x, shift, 0)), x)
        shift *= 2
    return x


def _hs(h):
    return slice(h * HEAD_DIM, (h + 1) * HEAD_DIM)


def _ln_kernel(x_ref, g_ref, b_ref, o_ref):
    o_ref[...] = _layer_norm(x_ref[...], g_ref[...], b_ref[...])


def _layer_norm_rows(x, g, b, tile):
    rows, d = x.shape
    return pl.pallas_call(
        _ln_kernel,
        grid=(rows // tile,),
        in_specs=[pl.BlockSpec((tile, d), lambda i: (i, 0)),
                  pl.BlockSpec((1, d), lambda i: (0, 0)),
                  pl.BlockSpec((1, d), lambda i: (0, 0))],
        out_specs=pl.BlockSpec((tile, d), lambda i: (i, 0)),
        out_shape=jax.ShapeDtypeStruct((rows, d), F32),
        compiler_params=_params(1),
        name="ln_rows",
    )(x, g, b)


def _mm_kernel(x_ref, w_ref, b_ref, o_ref):
    o_ref[...] = _dot(x_ref[...].astype(BF16), w_ref[...]) + b_ref[...]


def _matmul_bias(x, w_all, b_all, layer, tm, tn, name):
    rows, kdim = x.shape
    n = b_all.shape[-1]
    return pl.pallas_call(
        _mm_kernel,
        grid=(n // tn, rows // tm),
        in_specs=[pl.BlockSpec((tm, kdim), lambda j, i: (i, 0)),
                  pl.BlockSpec((None, kdim, tn), lambda j, i: (layer, 0, j)),
                  pl.BlockSpec((None, 1, tn), lambda j, i: (layer, 0, j))],
        out_specs=pl.BlockSpec((tm, tn), lambda j, i: (i, j)),
        out_shape=jax.ShapeDtypeStruct((rows, n), F32),
        compiler_params=_params(2),
        name=name,
    )(x, w_all, b_all)


def _mem_kv_kernel(x_ref, w_ref, k_ref, v_ref, kb_ref, vb_ref):
    d = N_HEADS * HEAD_DIM
    res = _dot(x_ref[...].astype(BF16), w_ref[...])
    for h in range(N_HEADS):
        k_ref[:, h, :] = res[:, _hs(h)]
        v_ref[:, h, :] = res[:, d + h * HEAD_DIM:d + (h + 1) * HEAD_DIM]
    kb_ref[...] = res[:, :d].astype(BF16)
    vb_ref[...] = res[:, d:].astype(BF16)


def _mem_kv(mem2d, w_kv, batch):
    rows, d = mem2d.shape
    n_mem = rows // batch
    depth = w_kv.shape[0]
    out_spec = pl.BlockSpec((None, None, n_mem, N_HEADS, HEAD_DIM), lambda l, b: (l, b, 0, 0, 0))
    out_sds = jax.ShapeDtypeStruct((depth, batch, n_mem, N_HEADS, HEAD_DIM), F32)
    dense_spec = pl.BlockSpec((None, n_mem, d), lambda l, b: (l, b, 0))
    dense_sds = jax.ShapeDtypeStruct((depth, rows, d), BF16)
    return pl.pallas_call(
        _mem_kv_kernel,
        grid=(depth, batch),
        in_specs=[pl.BlockSpec((n_mem, d), lambda l, b: (b, 0)),
                  pl.BlockSpec((None, d, 2 * d), lambda l, b: (l, 0, 0))],
        out_specs=[out_spec, out_spec, dense_spec, dense_spec],
        out_shape=[out_sds, out_sds, dense_sds, dense_sds],
        compiler_params=_params(2),
        name="mem_kv",
    )(mem2d, w_kv)


def _sample_attn_scores(q_row, keys):
    sub = lax.broadcasted_iota(jnp.int32, (SUBLANES, HEAD_DIM), 0)
    q = jnp.zeros((SUBLANES, HEAD_DIM), F32)
    for h in range(N_HEADS):
        q = jnp.where(sub == h, jnp.broadcast_to(q_row(h), (SUBLANES, HEAD_DIM)), q)
    keys2d = keys.reshape(keys.shape[0] * N_HEADS, HEAD_DIM).astype(BF16)
    return _dot_nt(q.astype(BF16), keys2d)


def _sample_attn_softmax(s):
    s = s * (HEAD_DIM ** -0.5)
    head = lax.broadcasted_iota(jnp.int32, s.shape, 0) % SUBLANES
    col_head = lax.broadcasted_iota(jnp.int32, s.shape, 1) % N_HEADS
    s = jnp.where((col_head == head) | (head >= N_HEADS), s, -jnp.inf)
    p = jnp.exp(s - jnp.max(s, axis=-1, keepdims=True))
    return p / jnp.sum(p, axis=-1, keepdims=True)


def _sample_attn_output(p, vals):
    vals2d = vals.reshape(vals.shape[0] * N_HEADS, HEAD_DIM).astype(BF16)
    return _dot(p.astype(BF16), vals2d)


def _mlstm_prompt_kernel(x_ref, wqkvo_ref, bqkvo_ref, wif_ref, bif_ref, wg0_ref, bg0_ref,
                         mhg_ref, wmo_ref, xq_ref, sk_ref, sv_ref,
                         y_ref, c_ref, n_ref, m_ref, ox_ref, s_scr):
    t = pl.program_id(1)
    d = N_HEADS * HEAD_DIM
    ride = sk_ref.shape[0]
    seq0 = (pl.program_id(0) * pl.num_programs(1) + t) * ride

    @pl.when(t == 0)
    def _():
        c_ref[...] = jnp.zeros(c_ref.shape, F32)
        n_ref[...] = jnp.zeros(n_ref.shape, F32)
        m_ref[...] = jnp.zeros(m_ref.shape, F32)

    for i in range(ride):
        s_scr[i * SUBLANES:(i + 1) * SUBLANES, :] = _sample_attn_scores(
            lambda h, i=i: xq_ref[pl.ds(seq0 + i, 1), _hs(h)], sk_ref[i])

    lc = TIME_TILE
    causal = (lax.broadcasted_iota(jnp.int32, (lc, lc), 1)
              <= lax.broadcasted_iota(jnp.int32, (lc, lc), 0))

    for chunk, r0 in enumerate(range(0, x_ref.shape[0], lc)):
        rows = slice(r0, r0 + lc)
        xb = x_ref[rows, :].astype(BF16)
        gates = _dot(xb, wif_ref[...]) + bif_ref[...]
        gi, gf = gates[:, 0:LANES], gates[:, LANES:2 * LANES]
        lf = _log_sigmoid(gf)
        bcum = _scan_rows(lf, jnp.add)
        a = gi - bcum
        m0 = m_ref[0]
        mt = bcum + jnp.maximum(m0, _scan_rows(a, jnp.maximum))
        inter = jnp.exp(m0 + bcum - mt)
        bm = bcum - mt
        emt = jnp.exp(-mt)
        b_last = bcum[lc - 1:lc, :]
        m_last = mt[lc - 1:lc, :]
        ws = jnp.exp(a + b_last - m_last)
        decay = jnp.exp(m0 + b_last - m_last)
        a_t = a.T

        def head_proj(h, xb=xb):
            def proj(part):
                cols = slice(part * d + h * HEAD_DIM, part * d + (h + 1) * HEAD_DIM)
                return _dot(xb, wqkvo_ref[:, cols]) + bqkvo_ref[:, cols]
            return proj(0), proj(1) * (HEAD_DIM ** -0.5), proj(2), proj(3)

        hn_heads = []
        projs = [head_proj(h) for h in range(HEAD_LOOKAHEAD)]
        for h in range(N_HEADS):
            if h + HEAD_LOOKAHEAD < N_HEADS:
                projs.append(head_proj(h + HEAD_LOOKAHEAD))
            elif h + HEAD_LOOKAHEAD == N_HEADS:
                g0 = _dot(xb, wg0_ref[...]) + bg0_ref[...]
            q, k, v, o = projs[h]
            qb = q.astype(BF16)
            vb = v.astype(BF16)
            col = slice(h, h + 1)
            logd = a_t[h:h + 1, :] + bm[:, col]
            dmat = jnp.exp(jnp.where(causal, logd, NEG))
            s = _dot_nt(qb, k.astype(BF16)) * dmat
            c_old = c_ref[0, h]
            n_old = n_ref[0, h:h + 1, :]
            inter_h = inter[:, col]
            num = _dot(s.astype(BF16), vb) + inter_h * _dot(qb, c_old.astype(BF16))
            den = (jnp.sum(s, axis=1, keepdims=True)
                   + inter_h * jnp.sum(q * n_old, axis=1, keepdims=True))
            ht = num / jnp.maximum(jnp.abs(den), emt[:, col])
            kw = k * ws[:, col]
            c_ref[0, h] = decay[:, col] * c_old + _dot_tn(kw.astype(BF16), vb)
            n_ref[0, h:h + 1, :] = decay[:, col] * n_old + jnp.sum(kw, axis=0, keepdims=True)
            hn = _head_norm(jax.nn.sigmoid(o) * ht) * mhg_ref[:, _hs(h)]
            hn_heads.append(hn.astype(BF16))
            if chunk == 0 and h == 0:
                s_scr[...] = _sample_attn_softmax(s_scr[...])
        m_ref[0] = m_last

        if chunk == 0:
            for i in range(ride):
                out = _sample_attn_output(s_scr[i * SUBLANES:(i + 1) * SUBLANES, :], sv_ref[i])
                for h in range(N_HEADS):
                    ox_ref[pl.ds(seq0 + i, 1), _hs(h)] = out[h:h + 1, :]

        ym = _dot(jnp.concatenate(hn_heads, axis=1), wmo_ref[...])
        y_ref[rows, :] = jax.nn.sigmoid(g0) * ym


def _mlstm_prompt(x2d, batch, w_qkvo, b_qkvo, w_if, b_if, w_rest, b_rest, mh_g, w_m_out,
                  proj_rest_s, mem_k, mem_v, layer):
    rows, d = x2d.shape
    nt = rows // batch // MLSTM_TILE
    wide = N_HEADS * d
    nseq, n_mem = mem_k.shape[1], mem_k.shape[2]
    ride = nseq // (batch * nt)
    assert ride * batch * nt == nseq

    def lw(shape, col):
        return pl.BlockSpec((None,) + shape, lambda b, t: (layer, 0, col),
                            pipeline_mode=pl.Buffered(1))

    kv_spec = pl.BlockSpec((None, ride, n_mem, N_HEADS, HEAD_DIM),
                           lambda b, t: (layer, b * nt + t, 0, 0, 0))
    return pl.pallas_call(
        _mlstm_prompt_kernel,
        grid=(batch, nt),
        in_specs=[pl.BlockSpec((MLSTM_TILE, d), lambda b, t: (b * nt + t, 0)),
                  lw((d, wide), 0), lw((1, wide), 0),
                  lw((d, 2 * LANES), 0), lw((1, 2 * LANES), 0),
                  lw((d, d), REST_G0), lw((1, d), REST_G0),
                  lw((1, d), 0), lw((d, d), 0),
                  pl.BlockSpec((nseq, d), lambda b, t: (0, REST_XQ)), kv_spec, kv_spec],
        out_specs=[pl.BlockSpec((MLSTM_TILE, d), lambda b, t: (b * nt + t, 0)),
                   pl.BlockSpec((1, N_HEADS, HEAD_DIM, HEAD_DIM), lambda b, t: (b, 0, 0, 0)),
                   pl.BlockSpec((1, N_HEADS, HEAD_DIM), lambda b, t: (b, 0, 0)),
                   pl.BlockSpec((1, 1, LANES), lambda b, t: (b, 0, 0)),
                   pl.BlockSpec((nseq, d), lambda b, t: (0, 0))],
        out_shape=[jax.ShapeDtypeStruct((rows, d), F32),
                   jax.ShapeDtypeStruct((batch, N_HEADS, HEAD_DIM, HEAD_DIM), F32),
                   jax.ShapeDtypeStruct((batch, N_HEADS, HEAD_DIM), F32),
                   jax.ShapeDtypeStruct((batch, 1, LANES), F32),
                   jax.ShapeDtypeStruct((nseq, d), F32)],
        scratch_shapes=[pltpu.VMEM((ride * SUBLANES, N_HEADS * n_mem), F32)],
        compiler_params=_params(2),
        name="mlstm_prompt",
    )(x2d, w_qkvo, b_qkvo, w_if, b_if, w_rest, b_rest, mh_g, w_m_out, proj_rest_s, mem_k, mem_v)


def _pool_window_sums(ext, window):
    s, span = ext, 1
    while span < window:
        s = s + pltpu.roll(s, span, 0)
        span *= 2
    return s


def _mix_prompt_kernel(x_ref, ymg_ref, mk_ref, mv_ref, wu_ref, bu_ref, wxq_ref, bxq_ref,
                       wg1_ref, bg1_ref, wg2_ref, bg2_ref, wpool_ref, pscale_ref, wxo_ref,
                       wo_ref, lng_ref, lnb_ref, sq_ref, sv_ref, skw_ref, sdec_ref, sc_ref,
                       *refs):
    x1_ref, pool_ref, sr_ref, scn_ref, ext_scr = refs[-5:]
    t = pl.program_id(1)
    tt = x_ref.shape[0]
    carry = 2 * SUBLANES
    ride = sc_ref.shape[0]
    seq0 = (pl.program_id(0) * pl.num_programs(1) + t) * ride

    @pl.when(t == 0)
    def _():
        ext_scr[0:carry, :] = jnp.zeros((carry, ext_scr.shape[1]), F32)

    for sub, r0 in enumerate(range(0, tt, TIME_TILE)):
        rows = slice(r0, r0 + TIME_TILE)
        for i in range(sub * ride // (tt // TIME_TILE), (sub + 1) * ride // (tt // TIME_TILE)):
            _sample_memory_update(seq0 + i, i, sq_ref, sv_ref, skw_ref, sdec_ref, sc_ref,
                                  scn_ref, sr_ref)

        x = x_ref[rows, :]
        xb = x.astype(BF16)

        u = _dot(xb, wu_ref[...]) + bu_ref[...]
        xq = (_dot(xb, wxq_ref[...]) + bxq_ref[...]).astype(BF16)

        ext_scr[carry + r0:carry + r0 + TIME_TILE, :] = u
        pos = t * tt + r0 + lax.broadcasted_iota(jnp.int32, (TIME_TILE, 1), 0)
        scores = [_dot_nt(xq[:, _hs(h)], mk_ref[:, _hs(h)]) for h in range(N_HEADS)]
        yp_groups = []
        for g, window in enumerate(POOL_WINDOWS):
            cols = slice(g * POOL_GROUP_DIM, (g + 1) * POOL_GROUP_DIM)
            ext = ext_scr[r0:r0 + carry + TIME_TILE, cols]
            wsum = _pool_window_sums(ext, window)[carry:, :]
            cnt = jnp.minimum(window, pos + 1).astype(F32)
            dlt = wsum / cnt - u[:, cols]
            yp_groups.append(_dot(dlt.astype(BF16), wpool_ref[g]))
        if r0 + TIME_TILE == tt:
            ext_scr[0:carry, :] = u[TIME_TILE - carry:, :]
            pool_ref[0] = u[TIME_TILE - carry:, :]
        g1 = _dot(xb, wg1_ref[...]) + bg1_ref[...]

        heads = []
        for h in range(N_HEADS):
            s = scores[h] * (HEAD_DIM ** -0.5)
            p = jnp.exp(s - jnp.max(s, axis=-1, keepdims=True))
            p = p / jnp.sum(p, axis=-1, keepdims=True)
            heads.append(_dot(p.astype(BF16), mv_ref[:, _hs(h)]).astype(BF16))
        g2 = _dot(xb, wg2_ref[...]) + bg2_ref[...]
        yp = jnp.concatenate(yp_groups, axis=1) * pscale_ref[...]
        yx = _dot(jnp.concatenate(heads, axis=1), wxo_ref[...])
        mix = ymg_ref[rows, :] + jax.nn.sigmoid(g1) * yp + jax.nn.sigmoid(g2) * yx
        y = ALPHA * x + _dot(mix.astype(BF16), wo_ref[...])
        x1_ref[rows, :] = _layer_norm(y, lng_ref[...], lnb_ref[...])


def _mix_prompt(x2d, ymg, mk, mv, batch, w_rest, b_rest, w_pool, pool_scale, w_x_out, w_o,
                ln_g, ln_b, proj_qkvo_s, kw_s, dec_s, state_c, prev_c, layer):
    rows, d = x2d.shape
    tile = MIX_TILE_FIRST if prev_c is None else MIX_TILE
    nt = rows // batch // tile
    n_mem = mk.shape[1] // batch
    carry = 2 * SUBLANES
    depth, nseq = state_c.shape[:2]
    ride = nseq // (batch * nt)
    assert ride * batch * nt == nseq and ride % (tile // TIME_TILE) == 0

    def lw(shape, col):
        return pl.BlockSpec((None,) + shape, lambda b, t: (layer, 0, col),
                            pipeline_mode=pl.Buffered(1))

    def sample_rows(col):
        return pl.BlockSpec((nseq, d), lambda b, t: (0, col))

    row_spec = pl.BlockSpec((tile, d), lambda b, t: (b * nt + t, 0))
    mem_spec = pl.BlockSpec((None, n_mem, d), lambda b, t: (layer, b, 0))
    c_block = (ride, N_HEADS, HEAD_DIM, HEAD_DIM)
    c_spec = pl.BlockSpec((None,) + c_block, lambda b, t: (layer, b * nt + t, 0, 0, 0))
    if prev_c is None:
        c_out_spec = pl.BlockSpec((depth,) + c_block, lambda b, t: (0, b * nt + t, 0, 0, 0))
        chain_specs, chain_args, aliases = [], [], {}
    else:
        c_out_spec = c_spec
        chain_specs, chain_args, aliases = [pl.BlockSpec(memory_space=pl.ANY)], [prev_c], {23: 3}
    return pl.pallas_call(
        _mix_prompt_kernel,
        grid=(batch, nt),
        in_specs=[row_spec, row_spec, mem_spec, mem_spec,
                  lw((d, d), REST_U), lw((1, d), REST_U),
                  lw((d, d), REST_XQ), lw((1, d), REST_XQ),
                  lw((d, d), REST_G1), lw((1, d), REST_G1),
                  lw((d, d), REST_G2), lw((1, d), REST_G2),
                  pl.BlockSpec((None, len(POOL_WINDOWS), POOL_GROUP_DIM, POOL_GROUP_DIM),
                               lambda b, t: (layer, 0, 0, 0)),
                  lw((1, d), 0), lw((d, d), 0), lw((d, d), 0), lw((1, d), 0), lw((1, d), 0),
                  sample_rows(0), sample_rows(2), sample_rows(0), sample_rows(0), c_spec]
        + chain_specs,
        out_specs=[row_spec, pl.BlockSpec((1, carry, d), lambda b, t: (b, 0, 0)),
                   sample_rows(0), c_out_spec],
        out_shape=[jax.ShapeDtypeStruct((rows, d), F32),
                   jax.ShapeDtypeStruct((batch, carry, d), F32),
                   jax.ShapeDtypeStruct((nseq, d), F32),
                   jax.ShapeDtypeStruct(state_c.shape, F32)],
        input_output_aliases=aliases,
        scratch_shapes=[pltpu.VMEM((carry + tile, d), F32)],
        compiler_params=_params(2),
        name="mix_prompt",
    )(x2d, ymg, mk, mv, w_rest, b_rest, w_rest, b_rest, w_rest, b_rest, w_rest, b_rest,
      w_pool, pool_scale, w_x_out, w_o, ln_g, ln_b,
      proj_qkvo_s, proj_qkvo_s, kw_s, dec_s, state_c, *chain_args)


def _ffn_chunk(xb, j, d_ff, wup_ref, cw_ref, cb_ref, wdn_ref, conv_taps):
    halves = _ffn_up(xb, j, d_ff, wup_ref)
    return _ffn_gate_down(halves, j, d_ff, cw_ref, cb_ref, wdn_ref, conv_taps), halves


def _ffn_up(xb, j, d_ff, wup_ref):
    return [_dot(xb, wup_ref[:, base + j * FF_CHUNK:base + (j + 1) * FF_CHUNK])
            for base in (0, d_ff)]


def _ffn_gate_down(halves, j, d_ff, cw_ref, cb_ref, wdn_ref, conv_taps):
    conv = []
    for base, hup in zip((0, d_ff), halves):
        cols = slice(base + j * FF_CHUNK, base + (j + 1) * FF_CHUNK)
        prev2, prev1 = conv_taps(hup, cols)
        conv.append(prev2 * cw_ref[0:1, cols] + prev1 * cw_ref[1:2, cols]
                    + hup * cw_ref[2:3, cols] + cb_ref[:, cols])
    act = jax.nn.gelu(conv[0]) * conv[1]
    return _dot(act.astype(BF16), wdn_ref[j * FF_CHUNK:(j + 1) * FF_CHUNK, :])


def _ffn_prompt_kernel(x_ref, wup_ref, cw_ref, cb_ref, wdn_ref, lng_ref, lnb_ref,
                       o_ref, conv_ref, ext_scr):
    t = pl.program_id(1)
    tt = x_ref.shape[0]
    d_ff = wdn_ref.shape[0]
    n_chunks = d_ff // FF_CHUNK

    @pl.when(t == 0)
    def _():
        ext_scr[0:SUBLANES, :] = jnp.zeros((SUBLANES, ext_scr.shape[1]), F32)

    items = [(r0, j) for r0 in range(0, tt, FF_SUB_ROWS) for j in range(n_chunks)]
    xs = {r0: x_ref[r0:r0 + FF_SUB_ROWS, :] for r0 in range(0, tt, FF_SUB_ROWS)}
    xbs = {r0: xs[r0].astype(BF16) for r0 in xs}

    def up(item):
        r0, j = item
        return _ffn_up(xbs[r0], j, d_ff, wup_ref)

    ups = [up(item) for item in items[:FF_LOOKAHEAD]]
    acc = None
    for i, (r0, j) in enumerate(items):
        if i + FF_LOOKAHEAD < len(items):
            ups.append(up(items[i + FF_LOOKAHEAD]))
        halves = ups[i]
        ups[i] = None
        lo = SUBLANES + r0

        def conv_taps(hup, cols):
            ext_scr[lo:lo + FF_SUB_ROWS, cols] = hup
            return (ext_scr[lo - 2:lo - 2 + FF_SUB_ROWS, cols],
                    ext_scr[lo - 1:lo - 1 + FF_SUB_ROWS, cols])

        part = _ffn_gate_down(halves, j, d_ff, cw_ref, cb_ref, wdn_ref, conv_taps)
        acc = part if j == 0 else part + acc
        if r0 + FF_SUB_ROWS == tt:
            for base, hup in zip((0, d_ff), halves):
                cols = slice(base + j * FF_CHUNK, base + (j + 1) * FF_CHUNK)
                ext_scr[0:SUBLANES, cols] = hup[FF_SUB_ROWS - SUBLANES:, :]
                conv_ref[0, :, cols] = hup[FF_SUB_ROWS - SUBLANES:, :]
        if j == n_chunks - 1:
            o_ref[r0:r0 + FF_SUB_ROWS, :] = _layer_norm(ALPHA * xs[r0] + acc, lng_ref[...],
                                                       lnb_ref[...])


def _ffn_prompt(x2d, batch, w_up, conv_w, conv_b, w_down, ln_g, ln_b, layer):
    rows, d = x2d.shape
    nt = rows // batch // FF_TILE
    d_ff = w_down.shape[1]

    def lw(shape):
        return pl.BlockSpec((None,) + shape, lambda b, t: (layer, 0, 0),
                            pipeline_mode=pl.Buffered(1))

    row_spec = pl.BlockSpec((FF_TILE, d), lambda b, t: (b * nt + t, 0))
    return pl.pallas_call(
        _ffn_prompt_kernel,
        grid=(batch, nt),
        in_specs=[row_spec, lw((d, 2 * d_ff)), lw((CONV_WIDTH, 2 * d_ff)), lw((1, 2 * d_ff)),
                  lw((d_ff, d)), lw((1, d)), lw((1, d))],
        out_specs=[row_spec, pl.BlockSpec((1, SUBLANES, 2 * d_ff), lambda b, t: (b, 0, 0))],
        out_shape=[jax.ShapeDtypeStruct((rows, d), F32),
                   jax.ShapeDtypeStruct((batch, SUBLANES, 2 * d_ff), F32)],
        scratch_shapes=[pltpu.VMEM((SUBLANES + FF_TILE, 2 * d_ff), F32)],
        compiler_params=_params(2),
        name="ffn_prompt",
    )(x2d, w_up, conv_w, conv_b, w_down, ln_g, ln_b)


def _ffn_sample_kernel(x_ref, st_ref, wup_ref, cw_ref, cb_ref, wdn_ref, lng_ref, lnb_ref,
                       o_ref, stn_ref):
    d_ff = wdn_ref.shape[0]
    x = x_ref[...]
    xb = x.astype(BF16)

    def conv_taps(hup, cols):
        return st_ref[:, 0, cols], st_ref[:, 1, cols]

    acc = None
    for j in range(d_ff // FF_CHUNK):
        part, halves = _ffn_chunk(xb, j, d_ff, wup_ref, cw_ref, cb_ref, wdn_ref, conv_taps)
        for base, hup in zip((0, d_ff), halves):
            cols = slice(base + j * FF_CHUNK, base + (j + 1) * FF_CHUNK)
            stn_ref[:, 0, cols] = st_ref[:, 1, cols]
            stn_ref[:, 1, cols] = hup
        acc = part if acc is None else acc + part
    o_ref[...] = _layer_norm(ALPHA * x + acc, lng_ref[...], lnb_ref[...])


def _ffn_sample(x, conv_state, w_up, conv_w, conv_b, w_down, ln_g, ln_b, layer, tile):
    rows, d = x.shape
    d_ff = w_down.shape[1]
    st_shape = conv_state.shape[2:]

    def lw(shape):
        return pl.BlockSpec((None,) + shape, lambda i: (layer, 0, 0))

    return pl.pallas_call(
        _ffn_sample_kernel,
        grid=(rows // tile,),
        in_specs=[pl.BlockSpec((tile, d), lambda i: (i, 0)),
                  pl.BlockSpec((None, tile) + st_shape, lambda i: (layer, i, 0, 0)),
                  lw((d, 2 * d_ff)), lw((CONV_WIDTH, 2 * d_ff)), lw((1, 2 * d_ff)),
                  lw((d_ff, d)), lw((1, d)), lw((1, d))],
        out_specs=[pl.BlockSpec((tile, d), lambda i: (i, 0)),
                   pl.BlockSpec((tile,) + st_shape, lambda i: (i, 0, 0))],
        out_shape=[jax.ShapeDtypeStruct((rows, d), F32),
                   jax.ShapeDtypeStruct((rows,) + st_shape, F32)],
        compiler_params=_params(1),
        name="ffn_sample",
    )(x, conv_state, w_up, conv_w, conv_b, w_down, ln_g, ln_b)


def _mlstm_pre_kernel(q_ref, k_ref, x_ref, wif_ref, bif_ref, n_ref, m_ref,
                      kw_ref, dec_ref, s_ref, den_ref, nn_ref, mn_ref):
    gates = _dot(x_ref[...].astype(BF16), wif_ref[...]) + bif_ref[...]
    gi = gates[:, 0:N_HEADS]
    lf = _log_sigmoid(gates[:, LANES:LANES + N_HEADS])
    m0 = m_ref[...]
    mt = jnp.maximum(m0 + lf, gi)
    inter_all = jnp.exp(m0 + lf - mt)
    dm_all = jnp.exp(gi - mt)
    emt_all = jnp.exp(-mt)
    mn_ref[...] = mt
    for h in range(N_HEADS):
        col = slice(h, h + 1)
        q = q_ref[:, _hs(h)]
        k = k_ref[:, _hs(h)] * (HEAD_DIM ** -0.5)
        n_old = n_ref[:, h, :]
        inter, dm = inter_all[:, col], dm_all[:, col]
        s = jnp.sum(q * k, axis=1, keepdims=True) * dm
        den = s + inter * jnp.sum(q * n_old, axis=1, keepdims=True)
        kw = k * dm
        kw_ref[:, _hs(h)] = kw
        dec_ref[:, _hs(h)] = jnp.broadcast_to(inter, kw.shape)
        s_ref[:, _hs(h)] = jnp.broadcast_to(s, kw.shape)
        den_ref[:, _hs(h)] = jnp.broadcast_to(jnp.maximum(jnp.abs(den), emt_all[:, col]), kw.shape)
        nn_ref[:, h, :] = inter * n_old + kw


def _mlstm_pre(proj_qkvo, x, w_if, b_if, state_n, state_m, layer, tile):
    nseq = proj_qkvo.shape[0]
    d = N_HEADS * HEAD_DIM

    def rows_at(col):
        return pl.BlockSpec((tile, d), lambda i: (i, col))

    wide_sds = jax.ShapeDtypeStruct((nseq, d), F32)
    return pl.pallas_call(
        _mlstm_pre_kernel,
        grid=(nseq // tile,),
        in_specs=[rows_at(0), rows_at(1), rows_at(0),
                  pl.BlockSpec((None, d, 2 * LANES), lambda i: (layer, 0, 0)),
                  pl.BlockSpec((None, 1, 2 * LANES), lambda i: (layer, 0, 0)),
                  pl.BlockSpec((None, tile, N_HEADS, HEAD_DIM), lambda i: (layer, i, 0, 0)),
                  pl.BlockSpec((None, tile, N_HEADS), lambda i: (layer, i, 0))],
        out_specs=[rows_at(0)] * 4 + [pl.BlockSpec((tile, N_HEADS, HEAD_DIM), lambda i: (i, 0, 0)),
                                      pl.BlockSpec((tile, N_HEADS), lambda i: (i, 0))],
        out_shape=[wide_sds] * 4 + [jax.ShapeDtypeStruct((nseq, N_HEADS, HEAD_DIM), F32),
                                    jax.ShapeDtypeStruct((nseq, N_HEADS), F32)],
        compiler_params=_params(1),
        name="mlstm_pre",
    )(proj_qkvo, proj_qkvo, x, w_if, b_if, state_n, state_m)


def _sample_memory_update(seq, i, q_ref, v_ref, kw_ref, dec_ref, c_ref, cn_ref, r_ref):
    mxu_rows = 2 * SUBLANES
    row0 = lax.broadcasted_iota(jnp.int32, (mxu_rows, HEAD_DIM), 0) == 0
    row = pl.ds(seq, 1)
    for h in range(N_HEADS):
        qrow = jnp.broadcast_to(q_ref[row, _hs(h)], (mxu_rows, HEAD_DIM)).astype(BF16)
        c_old = c_ref[i, h]
        r_ref[row, _hs(h)] = _dot(qrow, c_old.astype(BF16))[0:1, :]
        kw = jnp.where(row0, jnp.broadcast_to(kw_ref[row, _hs(h)], (mxu_rows, HEAD_DIM)), 0.0)
        vrow = jnp.broadcast_to(v_ref[row, _hs(h)], (mxu_rows, HEAD_DIM))
        c_new = dec_ref[row, _hs(h)] * c_old + _dot_tn(kw.astype(BF16), vrow.astype(BF16))
        if len(cn_ref.shape) == 5:
            for layer_slot in range(cn_ref.shape[0]):
                cn_ref[layer_slot, i, h] = c_new
        else:
            cn_ref[i, h] = c_new


def _merge_step_kernel(x_ref, r_ref, s_ref, den_ref, dec_ref, v_ref, o_ref, mhg_ref, ox_ref,
                       u_ref, g0_ref, g1_ref, g2_ref, pool_ref,
                       wmo_ref, wpool_ref, pscale_ref, wxo_ref, wo_ref, lng_ref, lnb_ref,
                       x1_ref, pooln_ref):
    x = x_ref[...]
    u = u_ref[...]
    ht = (s_ref[...] * v_ref[...] + dec_ref[...] * r_ref[...]) / den_ref[...]
    hc = jax.nn.sigmoid(o_ref[...]) * ht
    hn = jnp.concatenate([_head_norm(hc[:, _hs(h)]) for h in range(N_HEADS)], axis=1)
    hn = hn * mhg_ref[...]
    for r in range(1, POOL_BUF):
        pooln_ref[:, r - 1, :] = pool_ref[:, r, :]
    pooln_ref[:, POOL_BUF - 1, :] = u
    yp_groups = []
    for g, window in enumerate(POOL_WINDOWS):
        cols = slice(g * POOL_GROUP_DIM, (g + 1) * POOL_GROUP_DIM)
        wsum = u[:, cols]
        for back in range(1, window):
            wsum = wsum + pool_ref[:, POOL_BUF - back, cols]
        cnt = float(min(window, PAST_LEN + 1))
        dlt = wsum / cnt - u[:, cols]
        yp_groups.append(_dot(dlt.astype(BF16), wpool_ref[g]))
    yp = jnp.concatenate(yp_groups, axis=1) * pscale_ref[...]
    ym = _dot(hn.astype(BF16), wmo_ref[...])
    yx = _dot(ox_ref[...].astype(BF16), wxo_ref[...])
    mix = (jax.nn.sigmoid(g0_ref[...]) * ym + jax.nn.sigmoid(g1_ref[...]) * yp
           + jax.nn.sigmoid(g2_ref[...]) * yx)
    y = ALPHA * x + _dot(mix.astype(BF16), wo_ref[...])
    x1_ref[...] = _layer_norm(y, lng_ref[...], lnb_ref[...])


def _merge_step(x, readout, s_rep, den_rep, dec_rep, proj_qkvo, mh_g, ox, proj_rest, pool_state,
                w_m_out, w_pool, pool_scale, w_x_out, w_o, ln_g, ln_b, layer, tile):
    rows, d = x.shape
    st_shape = pool_state.shape[2:]

    def lw(shape):
        return pl.BlockSpec((None,) + shape, lambda i: (layer,) + (0,) * len(shape))

    def rows_at(col):
        return pl.BlockSpec((tile, d), lambda i: (i, col))

    return pl.pallas_call(
        _merge_step_kernel,
        grid=(rows // tile,),
        in_specs=[rows_at(0), rows_at(0), rows_at(0), rows_at(0), rows_at(0),
                  rows_at(2), rows_at(3), lw((1, d)), rows_at(0),
                  rows_at(REST_U), rows_at(REST_G0), rows_at(REST_G1), rows_at(REST_G2),
                  pl.BlockSpec((None, tile) + st_shape, lambda i: (layer, i, 0, 0)),
                  lw((d, d)), lw((len(POOL_WINDOWS), POOL_GROUP_DIM, POOL_GROUP_DIM)),
                  lw((1, d)), lw((d, d)), lw((d, d)), lw((1, d)), lw((1, d))],
        out_specs=[rows_at(0), pl.BlockSpec((tile,) + st_shape, lambda i: (i, 0, 0))],
        out_shape=[jax.ShapeDtypeStruct((rows, d), F32),
                   jax.ShapeDtypeStruct((rows,) + st_shape, F32)],
        compiler_params=_params(1),
        name="merge_step",
    )(x, readout, s_rep, den_rep, dec_rep, proj_qkvo, proj_qkvo, mh_g, ox,
      proj_rest, proj_rest, proj_rest, proj_rest, pool_state, w_m_out, w_pool,
      pool_scale, w_x_out, w_o, ln_g, ln_b)


def kernel(x_prompt, mem_prompt, x_sample, cache_mem_k, cache_mem_v, state_C, state_n, state_m,
           state_pool, state_conv, ln_in_g, ln_in_b, w_in, b_in, mh_g, w_m_out, w_pool,
           pool_scale, w_mem_kv, w_x_out, w_o, ln1_g, ln1_b, w_up, conv_w, conv_b, w_down,
           ln2_g, ln2_b):
    batch, seq, d = x_prompt.shape
    nseq = x_sample.shape[0]
    n_mem = mem_prompt.shape[1]
    depth = w_in.shape[0]
    d_ff = w_down.shape[1]
    wide = N_HEADS * d
    gate_off = wide
    rest_off = wide + 2 * N_HEADS
    assert d == N_HEADS * HEAD_DIM and d_ff % FF_CHUNK == 0
    assert all(seq % tile == 0 for tile in (MLSTM_TILE, MIX_TILE, MIX_TILE_FIRST, FF_TILE))
    assert depth == DEPTH and x_sample.shape[1] == 1

    def row(v):
        return v.reshape(depth, 1, v.shape[-1])

    w_qkvo = w_in.astype(BF16)
    w_rest = w_qkvo[:, :, rest_off:]
    b_qkvo, b_rest = row(b_in[:, :wide]), row(b_in[:, rest_off:])
    lane_pad = ((0, 0), (0, 0), (0, LANES - N_HEADS))
    w_if = jnp.concatenate(
        [jnp.pad(w_in[:, :, gate_off:gate_off + N_HEADS], lane_pad),
         jnp.pad(w_in[:, :, gate_off + N_HEADS:rest_off], lane_pad)], axis=2).astype(BF16)
    b_if = row(jnp.concatenate(
        [jnp.pad(b_in[:, gate_off:gate_off + N_HEADS], lane_pad[1:]),
         jnp.pad(b_in[:, gate_off + N_HEADS:rest_off], lane_pad[1:])], axis=1))
    w_m_out_b, w_pool_b, w_x_out_b, w_o_b = (w.astype(BF16) for w in (w_m_out, w_pool, w_x_out, w_o))
    w_up_b, w_down_b = w_up.astype(BF16), w_down.astype(BF16)
    w_kv_b = w_mem_kv.astype(BF16)
    mh_g_r, pool_scale_r, conv_b_r = row(mh_g), row(pool_scale), row(conv_b)
    ln1_g_r, ln1_b_r, ln2_g_r, ln2_b_r = row(ln1_g), row(ln1_b), row(ln2_g), row(ln2_b)

    xp = x_prompt.reshape(batch * seq, d)
    xs = x_sample.reshape(nseq, d)
    mem2d = mem_prompt.reshape(batch * n_mem, d)

    xp = _layer_norm_rows(xp, ln_in_g.reshape(1, d), ln_in_b.reshape(1, d), 2 * TIME_TILE)
    xs = _layer_norm_rows(xs, ln_in_g.reshape(1, d), ln_in_b.reshape(1, d), nseq)

    outs = {name: [] for name in ("pC", "pn", "pm", "ppool", "pconv", "sn", "sm", "spool", "sconv")}
    p_mem_k, p_mem_v, mk_b, mv_b = _mem_kv(mem2d, w_kv_b, batch)
    s_c = None
    for l in range(depth):
        proj_qkvo = _matmul_bias(xs, w_qkvo, b_qkvo, l, nseq, d, "proj_sample")
        proj_rest = _matmul_bias(xs, w_rest, b_rest, l, nseq, d, "proj_sample")
        kw_s, dec_s, s_rep, den_rep, s_n, s_m = _mlstm_pre(proj_qkvo, xs, w_if, b_if, state_n,
                                                          state_m, l, nseq // 2)

        ymg, p_c, p_n, p_m, ox = _mlstm_prompt(xp, batch, w_qkvo, b_qkvo, w_if, b_if, w_rest,
                                               b_rest, mh_g_r, w_m_out_b, proj_rest,
                                               cache_mem_k, cache_mem_v, l)
        x1, p_pool, readout, s_c = _mix_prompt(xp, ymg, mk_b, mv_b, batch, w_rest, b_rest,
                                               w_pool_b, pool_scale_r, w_x_out_b, w_o_b, ln1_g_r,
                                               ln1_b_r, proj_qkvo, kw_s, dec_s, state_C, s_c, l)
        xp, p_conv = _ffn_prompt(x1, batch, w_up_b, conv_w, conv_b_r, w_down_b, ln2_g_r,
                                 ln2_b_r, l)
        outs["pC"].append(p_c)
        outs["pn"].append(p_n)
        outs["pm"].append(p_m[:, 0, :N_HEADS])
        outs["ppool"].append(p_pool[:, 2 * SUBLANES - POOL_BUF:, :])
        outs["pconv"].append(p_conv[:, SUBLANES - (CONV_WIDTH - 1):, :])

        x1s, s_pool = _merge_step(xs, readout, s_rep, den_rep, dec_s, proj_qkvo, mh_g_r, ox,
                                  proj_rest, state_pool, w_m_out_b, w_pool_b, pool_scale_r,
                                  w_x_out_b, w_o_b, ln1_g_r, ln1_b_r, l, nseq // 2)
        xs, s_conv = _ffn_sample(x1s, state_conv, w_up_b, conv_w, conv_b_r, w_down_b, ln2_g_r,
                                 ln2_b_r, l, nseq // 2)
        outs["sn"].append(s_n)
        outs["sm"].append(s_m)
        outs["spool"].append(s_pool)
        outs["sconv"].append(s_conv)

    stacked = {k: jnp.stack(v) for k, v in outs.items()}
    return (xp.reshape(batch, seq, d), xs.reshape(nseq, 1, d),
            stacked["pC"], stacked["pn"], stacked["pm"], stacked["ppool"], stacked["pconv"],
            p_mem_k, p_mem_v,
            s_c, stacked["sn"], stacked["sm"], stacked["spool"], stacked["sconv"])
```

```python
import jax
import jax.numpy as jnp
from jax import lax
from jax.experimental import pallas as pl
from jax.experimental.pallas import tpu as pltpu

F32 = jnp.float32
BF16 = jnp.bfloat16

N_HEADS = 4
HEAD_DIM = 256
POOL_WINDOWS = (2, 4, 8, 16)
POOL_GROUP_DIM = 256
POOL_BUF = 15
CONV_WIDTH = 3
DEPTH = 4
PAST_LEN = 16384
ALPHA = (2.0 * DEPTH) ** 0.25
LN_EPS = 1e-5
NEG = -1e30

SUBLANES = 8
LANES = 128
TIME_TILE = 256
MLSTM_TILE = 512
FF_CHUNK = 256
MIX_TILE = 512
MIX_TILE_FIRST = 256
FF_TILE = 512
FF_SUB_ROWS = 256
FF_LOOKAHEAD = 4
VMEM_LIMIT = 56 * 1024 * 1024

REST_U, REST_XQ, REST_G0, REST_G1, REST_G2 = range(5)


def _params(n_axes):
    return pltpu.CompilerParams(
        dimension_semantics=("arbitrary",) * n_axes, vmem_limit_bytes=VMEM_LIMIT)


def _dot(a, b):
    return jnp.dot(a, b, preferred_element_type=F32)


def _dot_nt(a, b):
    return lax.dot_general(a, b, (((1,), (1,)), ((), ())), preferred_element_type=F32)


def _dot_tn(a, b):
    return lax.dot_general(a, b, (((0,), (0,)), ((), ())), preferred_element_type=F32)


def _layer_norm(x, g, b):
    mu = jnp.mean(x, axis=-1, keepdims=True)
    xc = x - mu
    var = jnp.mean(xc * xc, axis=-1, keepdims=True)
    return xc * lax.rsqrt(var + LN_EPS) * g + b


def _head_norm(x):
    mu = jnp.mean(x, axis=-1, keepdims=True)
    xc = x - mu
    var = jnp.mean(xc * xc, axis=-1, keepdims=True)
    return xc * lax.rsqrt(var + LN_EPS)


def _log_sigmoid(x):
    return jnp.minimum(x, 0.0) - jnp.log1p(jnp.exp(-jnp.abs(x)))


def _scan_rows(x, op):
    n = x.shape[0]
    rows = lax.broadcasted_iota(jnp.int32, x.shape, 0)
    shift = 1
    while shift < n:
        x = jnp.where(rows >= shift, op(x, pltpu.roll(x, shift, 0)), x)
        shift *= 2
    return x


def _hs(h):
    return slice(h * HEAD_DIM, (h + 1) * HEAD_DIM)


def _ln_kernel(x_ref, g_ref, b_ref, o_ref):
    o_ref[...] = _layer_norm(x_ref[...], g_ref[...], b_ref[...])


def _layer_norm_rows(x, g, b, tile):
    rows, d = x.shape
    return pl.pallas_call(
        _ln_kernel,
        grid=(rows // tile,),
        in_specs=[pl.BlockSpec((tile, d), lambda i: (i, 0)),
                  pl.BlockSpec((1, d), lambda i: (0, 0)),
                  pl.BlockSpec((1, d), lambda i: (0, 0))],
        out_specs=pl.BlockSpec((tile, d), lambda i: (i, 0)),
        out_shape=jax.ShapeDtypeStruct((rows, d), F32),
        compiler_params=_params(1),
        name="ln_rows",
    )(x, g, b)


def _mm_kernel(x_ref, w_ref, b_ref, o_ref):
    o_ref[...] = _dot(x_ref[...].astype(BF16), w_ref[...]) + b_ref[...]


def _matmul_bias(x, w_all, b_all, layer, tm, tn, name):
    rows, kdim = x.shape
    n = b_all.shape[-1]
    return pl.pallas_call(
        _mm_kernel,
        grid=(n // tn, rows // tm),
        in_specs=[pl.BlockSpec((tm, kdim), lambda j, i: (i, 0)),
                  pl.BlockSpec((None, kdim, tn), lambda j, i: (layer, 0, j)),
                  pl.BlockSpec((None, 1, tn), lambda j, i: (layer, 0, j))],
        out_specs=pl.BlockSpec((tm, tn), lambda j, i: (i, j)),
        out_shape=jax.ShapeDtypeStruct((rows, n), F32),
        compiler_params=_params(2),
        name=name,
    )(x, w_all, b_all)


def _mem_kv_kernel(x_ref, w_ref, k_ref, v_ref, kb_ref, vb_ref):
    d = N_HEADS * HEAD_DIM
    res = _dot(x_ref[...].astype(BF16), w_ref[...])
    for h in range(N_HEADS):
        k_ref[:, h, :] = res[:, _hs(h)]
        v_ref[:, h, :] = res[:, d + h * HEAD_DIM:d + (h + 1) * HEAD_DIM]
    kb_ref[...] = res[:, :d].astype(BF16)
    vb_ref[...] = res[:, d:].astype(BF16)


def _mem_kv(mem2d, w_kv, batch):
    rows, d = mem2d.shape
    n_mem = rows // batch
    depth = w_kv.shape[0]
    out_spec = pl.BlockSpec((None, None, n_mem, N_HEADS, HEAD_DIM), lambda l, b: (l, b, 0, 0, 0))
    out_sds = jax.ShapeDtypeStruct((depth, batch, n_mem, N_HEADS, HEAD_DIM), F32)
    dense_spec = pl.BlockSpec((None, n_mem, d), lambda l, b: (l, b, 0))
    dense_sds = jax.ShapeDtypeStruct((depth, rows, d), BF16)
    return pl.pallas_call(
        _mem_kv_kernel,
        grid=(depth, batch),
        in_specs=[pl.BlockSpec((n_mem, d), lambda l, b: (b, 0)),
                  pl.BlockSpec((None, d, 2 * d), lambda l, b: (l, 0, 0))],
        out_specs=[out_spec, out_spec, dense_spec, dense_spec],
        out_shape=[out_sds, out_sds, dense_sds, dense_sds],
        compiler_params=_params(2),
        name="mem_kv",
    )(mem2d, w_kv)


def _sample_attn_scores(q_row, keys):
    sub = lax.broadcasted_iota(jnp.int32, (SUBLANES, HEAD_DIM), 0)
    q = jnp.zeros((SUBLANES, HEAD_DIM), F32)
    for h in range(N_HEADS):
        q = jnp.where(sub == h, jnp.broadcast_to(q_row(h), (SUBLANES, HEAD_DIM)), q)
    keys2d = keys.reshape(keys.shape[0] * N_HEADS, HEAD_DIM).astype(BF16)
    return _dot_nt(q.astype(BF16), keys2d)


def _sample_attn_softmax(s):
    s = s * (HEAD_DIM ** -0.5)
    head = lax.broadcasted_iota(jnp.int32, s.shape, 0) % SUBLANES
    col_head = lax.broadcasted_iota(jnp.int32, s.shape, 1) % N_HEADS
    s = jnp.where((col_head == head) | (head >= N_HEADS), s, -jnp.inf)
    p = jnp.exp(s - jnp.max(s, axis=-1, keepdims=True))
    return p / jnp.sum(p, axis=-1, keepdims=True)


def _sample_attn_output(p, vals):
    vals2d = vals.reshape(vals.shape[0] * N_HEADS, HEAD_DIM).astype(BF16)
    return _dot(p.astype(BF16), vals2d)


def _mlstm_prompt_kernel(x_ref, wqkvo_ref, bqkvo_ref, wif_ref, bif_ref, wg0_ref, bg0_ref,
                         mhg_ref, wmo_ref, xq_ref, sk_ref, sv_ref,
                         y_ref, c_ref, n_ref, m_ref, ox_ref, s_scr):
    t = pl.program_id(1)
    d = N_HEADS * HEAD_DIM
    ride = sk_ref.shape[0]
    seq0 = (pl.program_id(0) * pl.num_programs(1) + t) * ride

    @pl.when(t == 0)
    def _():
        c_ref[...] = jnp.zeros(c_ref.shape, F32)
        n_ref[...] = jnp.zeros(n_ref.shape, F32)
        m_ref[...] = jnp.zeros(m_ref.shape, F32)

    for i in range(ride):
        s_scr[i * SUBLANES:(i + 1) * SUBLANES, :] = _sample_attn_scores(
            lambda h, i=i: xq_ref[pl.ds(seq0 + i, 1), _hs(h)], sk_ref[i])

    lc = TIME_TILE
    causal = (lax.broadcasted_iota(jnp.int32, (lc, lc), 1)
              <= lax.broadcasted_iota(jnp.int32, (lc, lc), 0))

    chunks = []
    m_run = m_ref[0]
    for r0 in range(0, x_ref.shape[0], lc):
        rows = slice(r0, r0 + lc)
        xb = x_ref[rows, :].astype(BF16)
        gates = _dot(xb, wif_ref[...]) + bif_ref[...]
        gi, gf = gates[:, 0:LANES], gates[:, LANES:2 * LANES]
        lf = _log_sigmoid(gf)
        bcum = _scan_rows(lf, jnp.add)
        a = gi - bcum
        mt = bcum + jnp.maximum(m_run, _scan_rows(a, jnp.maximum))
        b_last = bcum[lc - 1:lc, :]
        m_last = mt[lc - 1:lc, :]
        chunks.append(dict(
            rows=rows, xb=xb, inter=jnp.exp(m_run + bcum - mt), bm=bcum - mt, emt=jnp.exp(-mt),
            ws=jnp.exp(a + b_last - m_last), decay=jnp.exp(m_run + b_last - m_last), a_t=a.T,
            hn_heads=[]))
        m_run = m_last
    m_ref[0] = m_run

    def head_proj(item):
        c, h = item
        xb = chunks[c]["xb"]

        def proj(part):
            cols = slice(part * d + h * HEAD_DIM, part * d + (h + 1) * HEAD_DIM)
            return _dot(xb, wqkvo_ref[:, cols]) + bqkvo_ref[:, cols]
        return proj(0), proj(1) * (HEAD_DIM ** -0.5), proj(2), proj(3)

    items = [(c, h) for c in range(len(chunks)) for h in range(N_HEADS)]
    ahead = head_proj(items[0])
    for idx, (c, h) in enumerate(items):
        ck = chunks[c]
        q, k, v, o = ahead
        if idx + 1 < len(items):
            ahead = head_proj(items[idx + 1])
        if h == N_HEADS - 1:
            g0 = _dot(ck["xb"], wg0_ref[...]) + bg0_ref[...]
        qb = q.astype(BF16)
        vb = v.astype(BF16)
        col = slice(h, h + 1)
        logd = ck["a_t"][h:h + 1, :] + ck["bm"][:, col]
        dmat = jnp.exp(jnp.where(causal, logd, NEG))
        s = _dot_nt(qb, k.astype(BF16)) * dmat
        c_old = c_ref[0, h]
        n_old = n_ref[0, h:h + 1, :]
        inter_h = ck["inter"][:, col]
        num = _dot(s.astype(BF16), vb) + inter_h * _dot(qb, c_old.astype(BF16))
        den = (jnp.sum(s, axis=1, keepdims=True)
               + inter_h * jnp.sum(q * n_old, axis=1, keepdims=True))
        ht = num / jnp.maximum(jnp.abs(den), ck["emt"][:, col])
        kw = k * ck["ws"][:, col]
        decay_h = ck["decay"][:, col]
        c_ref[0, h] = decay_h * c_old + _dot_tn(kw.astype(BF16), vb)
        n_ref[0, h:h + 1, :] = decay_h * n_old + jnp.sum(kw, axis=0, keepdims=True)
        hn = _head_norm(jax.nn.sigmoid(o) * ht) * mhg_ref[:, _hs(h)]
        ck["hn_heads"].append(hn.astype(BF16))
        if idx == 0:
            s_scr[...] = _sample_attn_softmax(s_scr[...])
        if h == N_HEADS - 1:
            if c == 0:
                for i in range(ride):
                    out = _sample_attn_output(s_scr[i * SUBLANES:(i + 1) * SUBLANES, :],
                                              sv_ref[i])
                    for hh in range(N_HEADS):
                        ox_ref[pl.ds(seq0 + i, 1), _hs(hh)] = out[hh:hh + 1, :]
            ym = _dot(jnp.concatenate(ck["hn_heads"], axis=1), wmo_ref[...])
            y_ref[ck["rows"], :] = jax.nn.sigmoid(g0) * ym


def _mlstm_prompt(x2d, batch, w_qkvo, b_qkvo, w_if, b_if, w_rest, b_rest, mh_g, w_m_out,
                  proj_rest_s, mem_k, mem_v, layer):
    rows, d = x2d.shape
    nt = rows // batch // MLSTM_TILE
    wide = N_HEADS * d
    nseq, n_mem = mem_k.shape[1], mem_k.shape[2]
    ride = nseq // (batch * nt)
    assert ride * batch * nt == nseq

    def lw(shape, col):
        return pl.BlockSpec((None,) + shape, lambda b, t: (layer, 0, col),
                            pipeline_mode=pl.Buffered(1))

    kv_spec = pl.BlockSpec((None, ride, n_mem, N_HEADS, HEAD_DIM),
                           lambda b, t: (layer, b * nt + t, 0, 0, 0))
    return pl.pallas_call(
        _mlstm_prompt_kernel,
        grid=(batch, nt),
        in_specs=[pl.BlockSpec((MLSTM_TILE, d), lambda b, t: (b * nt + t, 0)),
                  lw((d, wide), 0), lw((1, wide), 0),
                  lw((d, 2 * LANES), 0), lw((1, 2 * LANES), 0),
                  lw((d, d), REST_G0), lw((1, d), REST_G0),
                  lw((1, d), 0), lw((d, d), 0),
                  pl.BlockSpec((nseq, d), lambda b, t: (0, REST_XQ)), kv_spec, kv_spec],
        out_specs=[pl.BlockSpec((MLSTM_TILE, d), lambda b, t: (b * nt + t, 0)),
                   pl.BlockSpec((1, N_HEADS, HEAD_DIM, HEAD_DIM), lambda b, t: (b, 0, 0, 0)),
                   pl.BlockSpec((1, N_HEADS, HEAD_DIM), lambda b, t: (b, 0, 0)),
                   pl.BlockSpec((1, 1, LANES), lambda b, t: (b, 0, 0)),
                   pl.BlockSpec((nseq, d), lambda b, t: (0, 0))],
        out_shape=[jax.ShapeDtypeStruct((rows, d), F32),
                   jax.ShapeDtypeStruct((batch, N_HEADS, HEAD_DIM, HEAD_DIM), F32),
                   jax.ShapeDtypeStruct((batch, N_HEADS, HEAD_DIM), F32),
                   jax.ShapeDtypeStruct((batch, 1, LANES), F32),
                   jax.ShapeDtypeStruct((nseq, d), F32)],
        scratch_shapes=[pltpu.VMEM((ride * SUBLANES, N_HEADS * n_mem), F32)],
        compiler_params=_params(2),
        name="mlstm_prompt",
    )(x2d, w_qkvo, b_qkvo, w_if, b_if, w_rest, b_rest, mh_g, w_m_out, proj_rest_s, mem_k, mem_v)


def _pool_window_sums(ext, window):
    s, span = ext, 1
    while span < window:
        s = s + pltpu.roll(s, span, 0)
        span *= 2
    return s


def _mix_prompt_kernel(x_ref, ymg_ref, mk_ref, mv_ref, wu_ref, bu_ref, wxq_ref, bxq_ref,
                       wg1_ref, bg1_ref, wg2_ref, bg2_ref, wpool_ref, pscale_ref, wxo_ref,
                       wo_ref, lng_ref, lnb_ref, sq_ref, sv_ref, skw_ref, sdec_ref, sc_ref,
                       *refs):
    x1_ref, pool_ref, sr_ref, scn_ref, ext_scr = refs[-5:]
    t = pl.program_id(1)
    tt = x_ref.shape[0]
    carry = 2 * SUBLANES
    ride = sc_ref.shape[0]
    seq0 = (pl.program_id(0) * pl.num_programs(1) + t) * ride

    @pl.when(t == 0)
    def _():
        ext_scr[0:carry, :] = jnp.zeros((carry, ext_scr.shape[1]), F32)

    for sub, r0 in enumerate(range(0, tt, TIME_TILE)):
        rows = slice(r0, r0 + TIME_TILE)
        for i in range(sub * ride // (tt // TIME_TILE), (sub + 1) * ride // (tt // TIME_TILE)):
            _sample_memory_update(seq0 + i, i, sq_ref, sv_ref, skw_ref, sdec_ref, sc_ref,
                                  scn_ref, sr_ref)

        x = x_ref[rows, :]
        xb = x.astype(BF16)

        u = _dot(xb, wu_ref[...]) + bu_ref[...]
        xq = (_dot(xb, wxq_ref[...]) + bxq_ref[...]).astype(BF16)

        ext_scr[carry + r0:carry + r0 + TIME_TILE, :] = u
        pos = t * tt + r0 + lax.broadcasted_iota(jnp.int32, (TIME_TILE, 1), 0)
        scores = [_dot_nt(xq[:, _hs(h)], mk_ref[:, _hs(h)]) for h in range(N_HEADS)]
        yp_groups = []
        for g, window in enumerate(POOL_WINDOWS):
            cols = slice(g * POOL_GROUP_DIM, (g + 1) * POOL_GROUP_DIM)
            ext = ext_scr[r0:r0 + carry + TIME_TILE, cols]
            wsum = _pool_window_sums(ext, window)[carry:, :]
            cnt = jnp.minimum(window, pos + 1).astype(F32)
            dlt = wsum / cnt - u[:, cols]
            yp_groups.append(_dot(dlt.astype(BF16), wpool_ref[g]))
        if r0 + TIME_TILE == tt:
            ext_scr[0:carry, :] = u[TIME_TILE - carry:, :]
            pool_ref[0] = u[TIME_TILE - carry:, :]
        g1 = _dot(xb, wg1_ref[...]) + bg1_ref[...]

        heads = []
        for h in range(N_HEADS):
            s = scores[h] * (HEAD_DIM ** -0.5)
            p = jnp.exp(s - jnp.max(s, axis=-1, keepdims=True))
            p = p / jnp.sum(p, axis=-1, keepdims=True)
            heads.append(_dot(p.astype(BF16), mv_ref[:, _hs(h)]).astype(BF16))
        g2 = _dot(xb, wg2_ref[...]) + bg2_ref[...]
        yp = jnp.concatenate(yp_groups, axis=1) * pscale_ref[...]
        yx = _dot(jnp.concatenate(heads, axis=1), wxo_ref[...])
        mix = ymg_ref[rows, :] + jax.nn.sigmoid(g1) * yp + jax.nn.sigmoid(g2) * yx
        y = ALPHA * x + _dot(mix.astype(BF16), wo_ref[...])
        x1_ref[rows, :] = _layer_norm(y, lng_ref[...], lnb_ref[...])


def _mix_prompt(x2d, ymg, mk, mv, batch, w_rest, b_rest, w_pool, pool_scale, w_x_out, w_o,
                ln_g, ln_b, proj_qkvo_s, kw_s, dec_s, state_c, prev_c, layer):
    rows, d = x2d.shape
    tile = MIX_TILE_FIRST if prev_c is None else MIX_TILE
    nt = rows // batch // tile
    n_mem = mk.shape[1] // batch
    carry = 2 * SUBLANES
    depth, nseq = state_c.shape[:2]
    ride = nseq // (batch * nt)
    assert ride * batch * nt == nseq and ride % (tile // TIME_TILE) == 0

    def lw(shape, col):
        return pl.BlockSpec((None,) + shape, lambda b, t: (layer, 0, col),
                            pipeline_mode=pl.Buffered(1))

    def sample_rows(col):
        return pl.BlockSpec((nseq, d), lambda b, t: (0, col))

    row_spec = pl.BlockSpec((tile, d), lambda b, t: (b * nt + t, 0))
    mem_spec = pl.BlockSpec((None, n_mem, d), lambda b, t: (layer, b, 0))
    c_block = (ride, N_HEADS, HEAD_DIM, HEAD_DIM)
    c_spec = pl.BlockSpec((None,) + c_block, lambda b, t: (layer, b * nt + t, 0, 0, 0))
    if prev_c is None:
        c_out_spec = pl.BlockSpec((depth,) + c_block, lambda b, t: (0, b * nt + t, 0, 0, 0))
        chain_specs, chain_args, aliases = [], [], {}
    else:
        c_out_spec = c_spec
        chain_specs, chain_args, aliases = [pl.BlockSpec(memory_space=pl.ANY)], [prev_c], {23: 3}
    return pl.pallas_call(
        _mix_prompt_kernel,
        grid=(batch, nt),
        in_specs=[row_spec, row_spec, mem_spec, mem_spec,
                  lw((d, d), REST_U), lw((1, d), REST_U),
                  lw((d, d), REST_XQ), lw((1, d), REST_XQ),
                  lw((d, d), REST_G1), lw((1, d), REST_G1),
                  lw((d, d), REST_G2), lw((1, d), REST_G2),
                  pl.BlockSpec((None, len(POOL_WINDOWS), POOL_GROUP_DIM, POOL_GROUP_DIM),
                               lambda b, t: (layer, 0, 0, 0)),
                  lw((1, d), 0), lw((d, d), 0), lw((d, d), 0), lw((1, d), 0), lw((1, d), 0),
                  sample_rows(0), sample_rows(2), sample_rows(0), sample_rows(0), c_spec]
        + chain_specs,
        out_specs=[row_spec, pl.BlockSpec((1, carry, d), lambda b, t: (b, 0, 0)),
                   sample_rows(0), c_out_spec],
        out_shape=[jax.ShapeDtypeStruct((rows, d), F32),
                   jax.ShapeDtypeStruct((batch, carry, d), F32),
                   jax.ShapeDtypeStruct((nseq, d), F32),
                   jax.ShapeDtypeStruct(state_c.shape, F32)],
        input_output_aliases=aliases,
        scratch_shapes=[pltpu.VMEM((carry + tile, d), F32)],
        compiler_params=_params(2),
        name="mix_prompt",
    )(x2d, ymg, mk, mv, w_rest, b_rest, w_rest, b_rest, w_rest, b_rest, w_rest, b_rest,
      w_pool, pool_scale, w_x_out, w_o, ln_g, ln_b,
      proj_qkvo_s, proj_qkvo_s, kw_s, dec_s, state_c, *chain_args)


def _ffn_chunk(xb, j, d_ff, wup_ref, cw_ref, cb_ref, wdn_ref, conv_taps):
    halves = _ffn_up(xb, j, d_ff, wup_ref)
    return _ffn_gate_down(halves, j, d_ff, cw_ref, cb_ref, wdn_ref, conv_taps), halves


def _ffn_up(xb, j, d_ff, wup_ref):
    return [_dot(xb, wup_ref[:, base + j * FF_CHUNK:base + (j + 1) * FF_CHUNK])
            for base in (0, d_ff)]


def _ffn_gate_down(halves, j, d_ff, cw_ref, cb_ref, wdn_ref, conv_taps):
    conv = []
    for base, hup in zip((0, d_ff), halves):
        cols = slice(base + j * FF_CHUNK, base + (j + 1) * FF_CHUNK)
        prev2, prev1 = conv_taps(hup, cols)
        conv.append(prev2 * cw_ref[0:1, cols] + prev1 * cw_ref[1:2, cols]
                    + hup * cw_ref[2:3, cols] + cb_ref[:, cols])
    act = jax.nn.gelu(conv[0]) * conv[1]
    return _dot(act.astype(BF16), wdn_ref[j * FF_CHUNK:(j + 1) * FF_CHUNK, :])


def _ffn_prompt_kernel(x_ref, wup_ref, cw_ref, cb_ref, wdn_ref, lng_ref, lnb_ref,
                       o_ref, conv_ref, ext_scr):
    t = pl.program_id(1)
    tt = x_ref.shape[0]
    d_ff = wdn_ref.shape[0]
    n_chunks = d_ff // FF_CHUNK

    @pl.when(t == 0)
    def _():
        ext_scr[0:SUBLANES, :] = jnp.zeros((SUBLANES, ext_scr.shape[1]), F32)

    items = [(r0, j) for r0 in range(0, tt, FF_SUB_ROWS) for j in range(n_chunks)]
    xs = {r0: x_ref[r0:r0 + FF_SUB_ROWS, :] for r0 in range(0, tt, FF_SUB_ROWS)}
    xbs = {r0: xs[r0].astype(BF16) for r0 in xs}

    def up(item):
        r0, j = item
        return _ffn_up(xbs[r0], j, d_ff, wup_ref)

    ups = [up(item) for item in items[:FF_LOOKAHEAD]]
    acc = None
    for i, (r0, j) in enumerate(items):
        if i + FF_LOOKAHEAD < len(items):
            ups.append(up(items[i + FF_LOOKAHEAD]))
        halves = ups[i]
        ups[i] = None
        lo = SUBLANES + r0

        def conv_taps(hup, cols):
            ext_scr[lo:lo + FF_SUB_ROWS, cols] = hup
            return (ext_scr[lo - 2:lo - 2 + FF_SUB_ROWS, cols],
                    ext_scr[lo - 1:lo - 1 + FF_SUB_ROWS, cols])

        part = _ffn_gate_down(halves, j, d_ff, cw_ref, cb_ref, wdn_ref, conv_taps)
        acc = part if j == 0 else part + acc
        if r0 + FF_SUB_ROWS == tt:
            for base, hup in zip((0, d_ff), halves):
                cols = slice(base + j * FF_CHUNK, base + (j + 1) * FF_CHUNK)
                ext_scr[0:SUBLANES, cols] = hup[FF_SUB_ROWS - SUBLANES:, :]
                conv_ref[0, :, cols] = hup[FF_SUB_ROWS - SUBLANES:, :]
        if j == n_chunks - 1:
            o_ref[r0:r0 + FF_SUB_ROWS, :] = _layer_norm(ALPHA * xs[r0] + acc, lng_ref[...],
                                                       lnb_ref[...])


def _ffn_prompt(x2d, batch, w_up, conv_w, conv_b, w_down, ln_g, ln_b, layer):
    rows, d = x2d.shape
    nt = rows // batch // FF_TILE
    d_ff = w_down.shape[1]

    def lw(shape):
        return pl.BlockSpec((None,) + shape, lambda b, t: (layer, 0, 0),
                            pipeline_mode=pl.Buffered(1))

    row_spec = pl.BlockSpec((FF_TILE, d), lambda b, t: (b * nt + t, 0))
    return pl.pallas_call(
        _ffn_prompt_kernel,
        grid=(batch, nt),
        in_specs=[row_spec, lw((d, 2 * d_ff)), lw((CONV_WIDTH, 2 * d_ff)), lw((1, 2 * d_ff)),
                  lw((d_ff, d)), lw((1, d)), lw((1, d))],
        out_specs=[row_spec, pl.BlockSpec((1, SUBLANES, 2 * d_ff), lambda b, t: (b, 0, 0))],
        out_shape=[jax.ShapeDtypeStruct((rows, d), F32),
                   jax.ShapeDtypeStruct((batch, SUBLANES, 2 * d_ff), F32)],
        scratch_shapes=[pltpu.VMEM((SUBLANES + FF_TILE, 2 * d_ff), F32)],
        compiler_params=_params(2),
        name="ffn_prompt",
    )(x2d, w_up, conv_w, conv_b, w_down, ln_g, ln_b)


def _ffn_sample_kernel(x_ref, st_ref, wup_ref, cw_ref, cb_ref, wdn_ref, lng_ref, lnb_ref,
                       o_ref, stn_ref):
    d_ff = wdn_ref.shape[0]
    x = x_ref[...]
    xb = x.astype(BF16)

    def conv_taps(hup, cols):
        return st_ref[:, 0, cols], st_ref[:, 1, cols]

    acc = None
    for j in range(d_ff // FF_CHUNK):
        part, halves = _ffn_chunk(xb, j, d_ff, wup_ref, cw_ref, cb_ref, wdn_ref, conv_taps)
        for base, hup in zip((0, d_ff), halves):
            cols = slice(base + j * FF_CHUNK, base + (j + 1) * FF_CHUNK)
            stn_ref[:, 0, cols] = st_ref[:, 1, cols]
            stn_ref[:, 1, cols] = hup
        acc = part if acc is None else acc + part
    o_ref[...] = _layer_norm(ALPHA * x + acc, lng_ref[...], lnb_ref[...])


def _ffn_sample(x, conv_state, w_up, conv_w, conv_b, w_down, ln_g, ln_b, layer, tile):
    rows, d = x.shape
    d_ff = w_down.shape[1]
    st_shape = conv_state.shape[2:]

    def lw(shape):
        return pl.BlockSpec((None,) + shape, lambda i: (layer, 0, 0))

    return pl.pallas_call(
        _ffn_sample_kernel,
        grid=(rows // tile,),
        in_specs=[pl.BlockSpec((tile, d), lambda i: (i, 0)),
                  pl.BlockSpec((None, tile) + st_shape, lambda i: (layer, i, 0, 0)),
                  lw((d, 2 * d_ff)), lw((CONV_WIDTH, 2 * d_ff)), lw((1, 2 * d_ff)),
                  lw((d_ff, d)), lw((1, d)), lw((1, d))],
        out_specs=[pl.BlockSpec((tile, d), lambda i: (i, 0)),
                   pl.BlockSpec((tile,) + st_shape, lambda i: (i, 0, 0))],
        out_shape=[jax.ShapeDtypeStruct((rows, d), F32),
                   jax.ShapeDtypeStruct((rows,) + st_shape, F32)],
        compiler_params=_params(1),
        name="ffn_sample",
    )(x, conv_state, w_up, conv_w, conv_b, w_down, ln_g, ln_b)


def _mlstm_pre_kernel(q_ref, k_ref, x_ref, wif_ref, bif_ref, n_ref, m_ref,
                      kw_ref, dec_ref, s_ref, den_ref, nn_ref, mn_ref):
    gates = _dot(x_ref[...].astype(BF16), wif_ref[...]) + bif_ref[...]
    gi = gates[:, 0:N_HEADS]
    lf = _log_sigmoid(gates[:, LANES:LANES + N_HEADS])
    m0 = m_ref[...]
    mt = jnp.maximum(m0 + lf, gi)
    inter_all = jnp.exp(m0 + lf - mt)
    dm_all = jnp.exp(gi - mt)
    emt_all = jnp.exp(-mt)
    mn_ref[...] = mt
    for h in range(N_HEADS):
        col = slice(h, h + 1)
        q = q_ref[:, _hs(h)]
        k = k_ref[:, _hs(h)] * (HEAD_DIM ** -0.5)
        n_old = n_ref[:, h, :]
        inter, dm = inter_all[:, col], dm_all[:, col]
        s = jnp.sum(q * k, axis=1, keepdims=True) * dm
        den = s + inter * jnp.sum(q * n_old, axis=1, keepdims=True)
        kw = k * dm
        kw_ref[:, _hs(h)] = kw
        dec_ref[:, _hs(h)] = jnp.broadcast_to(inter, kw.shape)
        s_ref[:, _hs(h)] = jnp.broadcast_to(s, kw.shape)
        den_ref[:, _hs(h)] = jnp.broadcast_to(jnp.maximum(jnp.abs(den), emt_all[:, col]), kw.shape)
        nn_ref[:, h, :] = inter * n_old + kw


def _mlstm_pre(proj_qkvo, x, w_if, b_if, state_n, state_m, layer, tile):
    nseq = proj_qkvo.shape[0]
    d = N_HEADS * HEAD_DIM

    def rows_at(col):
        return pl.BlockSpec((tile, d), lambda i: (i, col))

    wide_sds = jax.ShapeDtypeStruct((nseq, d), F32)
    return pl.pallas_call(
        _mlstm_pre_kernel,
        grid=(nseq // tile,),
        in_specs=[rows_at(0), rows_at(1), rows_at(0),
                  pl.BlockSpec((None, d, 2 * LANES), lambda i: (layer, 0, 0)),
                  pl.BlockSpec((None, 1, 2 * LANES), lambda i: (layer, 0, 0)),
                  pl.BlockSpec((None, tile, N_HEADS, HEAD_DIM), lambda i: (layer, i, 0, 0)),
                  pl.BlockSpec((None, tile, N_HEADS), lambda i: (layer, i, 0))],
        out_specs=[rows_at(0)] * 4 + [pl.BlockSpec((tile, N_HEADS, HEAD_DIM), lambda i: (i, 0, 0)),
                                      pl.BlockSpec((tile, N_HEADS), lambda i: (i, 0))],
        out_shape=[wide_sds] * 4 + [jax.ShapeDtypeStruct((nseq, N_HEADS, HEAD_DIM), F32),
                                    jax.ShapeDtypeStruct((nseq, N_HEADS), F32)],
        compiler_params=_params(1),
        name="mlstm_pre",
    )(proj_qkvo, proj_qkvo, x, w_if, b_if, state_n, state_m)


def _sample_memory_update(seq, i, q_ref, v_ref, kw_ref, dec_ref, c_ref, cn_ref, r_ref):
    mxu_rows = 2 * SUBLANES
    row0 = lax.broadcasted_iota(jnp.int32, (mxu_rows, HEAD_DIM), 0) == 0
    row = pl.ds(seq, 1)
    for h in range(N_HEADS):
        qrow = jnp.broadcast_to(q_ref[row, _hs(h)], (mxu_rows, HEAD_DIM)).astype(BF16)
        c_old = c_ref[i, h]
        r_ref[row, _hs(h)] = _dot(qrow, c_old.astype(BF16))[0:1, :]
        kw = jnp.where(row0, jnp.broadcast_to(kw_ref[row, _hs(h)], (mxu_rows, HEAD_DIM)), 0.0)
        vrow = jnp.broadcast_to(v_ref[row, _hs(h)], (mxu_rows, HEAD_DIM))
        c_new = dec_ref[row, _hs(h)] * c_old + _dot_tn(kw.astype(BF16), vrow.astype(BF16))
        if len(cn_ref.shape) == 5:
            for layer_slot in range(cn_ref.shape[0]):
                cn_ref[layer_slot, i, h] = c_new
        else:
            cn_ref[i, h] = c_new


def _merge_step_kernel(x_ref, r_ref, s_ref, den_ref, dec_ref, v_ref, o_ref, mhg_ref, ox_ref,
                       u_ref, g0_ref, g1_ref, g2_ref, pool_ref,
                       wmo_ref, wpool_ref, pscale_ref, wxo_ref, wo_ref, lng_ref, lnb_ref,
                       x1_ref, pooln_ref):
    x = x_ref[...]
    u = u_ref[...]
    ht = (s_ref[...] * v_ref[...] + dec_ref[...] * r_ref[...]) / den_ref[...]
    hc = jax.nn.sigmoid(o_ref[...]) * ht
    hn = jnp.concatenate([_head_norm(hc[:, _hs(h)]) for h in range(N_HEADS)], axis=1)
    hn = hn * mhg_ref[...]
    for r in range(1, POOL_BUF):
        pooln_ref[:, r - 1, :] = pool_ref[:, r, :]
    pooln_ref[:, POOL_BUF - 1, :] = u
    yp_groups = []
    for g, window in enumerate(POOL_WINDOWS):
        cols = slice(g * POOL_GROUP_DIM, (g + 1) * POOL_GROUP_DIM)
        wsum = u[:, cols]
        for back in range(1, window):
            wsum = wsum + pool_ref[:, POOL_BUF - back, cols]
        cnt = float(min(window, PAST_LEN + 1))
        dlt = wsum / cnt - u[:, cols]
        yp_groups.append(_dot(dlt.astype(BF16), wpool_ref[g]))
    yp = jnp.concatenate(yp_groups, axis=1) * pscale_ref[...]
    ym = _dot(hn.astype(BF16), wmo_ref[...])
    yx = _dot(ox_ref[...].astype(BF16), wxo_ref[...])
    mix = (jax.nn.sigmoid(g0_ref[...]) * ym + jax.nn.sigmoid(g1_ref[...]) * yp
           + jax.nn.sigmoid(g2_ref[...]) * yx)
    y = ALPHA * x + _dot(mix.astype(BF16), wo_ref[...])
    x1_ref[...] = _layer_norm(y, lng_ref[...], lnb_ref[...])


def _merge_step(x, readout, s_rep, den_rep, dec_rep, proj_qkvo, mh_g, ox, proj_rest, pool_state,
                w_m_out, w_pool, pool_scale, w_x_out, w_o, ln_g, ln_b, layer, tile):
    rows, d = x.shape
    st_shape = pool_state.shape[2:]

    def lw(shape):
        return pl.BlockSpec((None,) + shape, lambda i: (layer,) + (0,) * len(shape))

    def rows_at(col):
        return pl.BlockSpec((tile, d), lambda i: (i, col))

    return pl.pallas_call(
        _merge_step_kernel,
        grid=(rows // tile,),
        in_specs=[rows_at(0), rows_at(0), rows_at(0), rows_at(0), rows_at(0),
                  rows_at(2), rows_at(3), lw((1, d)), rows_at(0),
                  rows_at(REST_U), rows_at(REST_G0), rows_at(REST_G1), rows_at(REST_G2),
                  pl.BlockSpec((None, tile) + st_shape, lambda i: (layer, i, 0, 0)),
                  lw((d, d)), lw((len(POOL_WINDOWS), POOL_GROUP_DIM, POOL_GROUP_DIM)),
                  lw((1, d)), lw((d, d)), lw((d, d)), lw((1, d)), lw((1, d))],
        out_specs=[rows_at(0), pl.BlockSpec((tile,) + st_shape, lambda i: (i, 0, 0))],
        out_shape=[jax.ShapeDtypeStruct((rows, d), F32),
                   jax.ShapeDtypeStruct((rows,) + st_shape, F32)],
        compiler_params=_params(1),
        name="merge_step",
    )(x, readout, s_rep, den_rep, dec_rep, proj_qkvo, proj_qkvo, mh_g, ox,
      proj_rest, proj_rest, proj_rest, proj_rest, pool_state, w_m_out, w_pool,
      pool_scale, w_x_out, w_o, ln_g, ln_b)


def kernel(x_prompt, mem_prompt, x_sample, cache_mem_k, cache_mem_v, state_C, state_n, state_m,
           state_pool, state_conv, ln_in_g, ln_in_b, w_in, b_in, mh_g, w_m_out, w_pool,
           pool_scale, w_mem_kv, w_x_out, w_o, ln1_g, ln1_b, w_up, conv_w, conv_b, w_down,
           ln2_g, ln2_b):
    batch, seq, d = x_prompt.shape
    nseq = x_sample.shape[0]
    n_mem = mem_prompt.shape[1]
    depth = w_in.shape[0]
    d_ff = w_down.shape[1]
    wide = N_HEADS * d
    gate_off = wide
    rest_off = wide + 2 * N_HEADS
    assert d == N_HEADS * HEAD_DIM and d_ff % FF_CHUNK == 0
    assert all(seq % tile == 0 for tile in (MLSTM_TILE, MIX_TILE, MIX_TILE_FIRST, FF_TILE))
    assert depth == DEPTH and x_sample.shape[1] == 1

    def row(v):
        return v.reshape(depth, 1, v.shape[-1])

    w_qkvo = w_in.astype(BF16)
    w_rest = w_qkvo[:, :, rest_off:]
    b_qkvo, b_rest = row(b_in[:, :wide]), row(b_in[:, rest_off:])
    lane_pad = ((0, 0), (0, 0), (0, LANES - N_HEADS))
    w_if = jnp.concatenate(
        [jnp.pad(w_in[:, :, gate_off:gate_off + N_HEADS], lane_pad),
         jnp.pad(w_in[:, :, gate_off + N_HEADS:rest_off], lane_pad)], axis=2).astype(BF16)
    b_if = row(jnp.concatenate(
        [jnp.pad(b_in[:, gate_off:gate_off + N_HEADS], lane_pad[1:]),
         jnp.pad(b_in[:, gate_off + N_HEADS:rest_off], lane_pad[1:])], axis=1))
    w_m_out_b, w_pool_b, w_x_out_b, w_o_b = (w.astype(BF16) for w in (w_m_out, w_pool, w_x_out, w_o))
    w_up_b, w_down_b = w_up.astype(BF16), w_down.astype(BF16)
    w_kv_b = w_mem_kv.astype(BF16)
    mh_g_r, pool_scale_r, conv_b_r = row(mh_g), row(pool_scale), row(conv_b)
    ln1_g_r, ln1_b_r, ln2_g_r, ln2_b_r = row(ln1_g), row(ln1_b), row(ln2_g), row(ln2_b)

    xp = x_prompt.reshape(batch * seq, d)
    xs = x_sample.reshape(nseq, d)
    mem2d = mem_prompt.reshape(batch * n_mem, d)

    xp = _layer_norm_rows(xp, ln_in_g.reshape(1, d), ln_in_b.reshape(1, d), 2 * TIME_TILE)
    xs = _layer_norm_rows(xs, ln_in_g.reshape(1, d), ln_in_b.reshape(1, d), nseq)

    outs = {name: [] for name in ("pC", "pn", "pm", "ppool", "pconv", "sn", "sm", "spool", "sconv")}
    p_mem_k, p_mem_v, mk_b, mv_b = _mem_kv(mem2d, w_kv_b, batch)
    s_c = None
    for l in range(depth):
        proj_qkvo = _matmul_bias(xs, w_qkvo, b_qkvo, l, nseq, d, "proj_sample")
        proj_rest = _matmul_bias(xs, w_rest, b_rest, l, nseq, d, "proj_sample")
        kw_s, dec_s, s_rep, den_rep, s_n, s_m = _mlstm_pre(proj_qkvo, xs, w_if, b_if, state_n,
                                                          state_m, l, nseq // 2)

        ymg, p_c, p_n, p_m, ox = _mlstm_prompt(xp, batch, w_qkvo, b_qkvo, w_if, b_if, w_rest,
                                               b_rest, mh_g_r, w_m_out_b, proj_rest,
                                               cache_mem_k, cache_mem_v, l)
        x1, p_pool, readout, s_c = _mix_prompt(xp, ymg, mk_b, mv_b, batch, w_rest, b_rest,
                                               w_pool_b, pool_scale_r, w_x_out_b, w_o_b, ln1_g_r,
                                               ln1_b_r, proj_qkvo, kw_s, dec_s, state_C, s_c, l)
        xp, p_conv = _ffn_prompt(x1, batch, w_up_b, conv_w, conv_b_r, w_down_b, ln2_g_r,
                                 ln2_b_r, l)
        outs["pC"].append(p_c)
        outs["pn"].append(p_n)
        outs["pm"].append(p_m[:, 0, :N_HEADS])
        outs["ppool"].append(p_pool[:, 2 * SUBLANES - POOL_BUF:, :])
        outs["pconv"].append(p_conv[:, SUBLANES - (CONV_WIDTH - 1):, :])

        x1s, s_pool = _merge_step(xs, readout, s_rep, den_rep, dec_s, proj_qkvo, mh_g_r, ox,
                                  proj_rest, state_pool, w_m_out_b, w_pool_b, pool_scale_r,
                                  w_x_out_b, w_o_b, ln1_g_r, ln1_b_r, l, nseq // 2)
        xs, s_conv = _ffn_sample(x1s, state_conv, w_up_b, conv_w, conv_b_r, w_down_b, ln2_g_r,
                                 ln2_b_r, l, nseq // 2)
        outs["sn"].append(s_n)
        outs["sm"].append(s_m)
        outs["spool"].append(s_pool)
        outs["sconv"].append(s_conv)

    stacked = {k: jnp.stack(v) for k, v in outs.items()}
    return (xp.reshape(batch, seq, d), xs.reshape(nseq, 1, d),
            stacked["pC"], stacked["pn"], stacked["pm"], stacked["ppool"], stacked["pconv"],
            p_mem_k, p_mem_v,
            s_c, stacked["sn"], stacked["sm"], stacked["spool"], stacked["sconv"])
```

```python
import jax
import jax.numpy as jnp
from jax import lax
from jax.experimental import pallas as pl
from jax.experimental.pallas import tpu as pltpu

F32 = jnp.float32
BF16 = jnp.bfloat16

N_HEADS = 4
HEAD_DIM = 256
POOL_WINDOWS = (2, 4, 8, 16)
POOL_GROUP_DIM = 256
POOL_BUF = 15
CONV_WIDTH = 3
DEPTH = 4
PAST_LEN = 16384
ALPHA = (2.0 * DEPTH) ** 0.25
LN_EPS = 1e-5
NEG = -1e30

SUBLANES = 8
LANES = 128
TIME_TILE = 256
MLSTM_TILE = 512
HEAD_LOOKAHEAD = 1
FF_CHUNK = 256
MIX_TILE = 512
MIX_TILE_FIRST = 256
FF_TILE = 512
FF_SUB_ROWS = 256
FF_LOOKAHEAD = 4
VMEM_LIMIT = 56 * 1024 * 1024

REST_U, REST_XQ, REST_G0, REST_G1, REST_G2 = range(5)


def _params(n_axes):
    return pltpu.CompilerParams(
        dimension_semantics=("arbitrary",) * n_axes, vmem_limit_bytes=VMEM_LIMIT)


def _dot(a, b):
    return jnp.dot(a, b, preferred_element_type=F32)


def _dot_nt(a, b):
    return lax.dot_general(a, b, (((1,), (1,)), ((), ())), preferred_element_type=F32)


def _dot_tn(a, b):
    return lax.dot_general(a, b, (((0,), (0,)), ((), ())), preferred_element_type=F32)


def _layer_norm(x, g, b):
    mu = jnp.mean(x, axis=-1, keepdims=True)
    xc = x - mu
    var = jnp.mean(xc * xc, axis=-1, keepdims=True)
    return xc * lax.rsqrt(var + LN_EPS) * g + b


def _head_norm(x):
    mu = jnp.mean(x, axis=-1, keepdims=True)
    xc = x - mu
    var = jnp.mean(xc * xc, axis=-1, keepdims=True)
    return xc * lax.rsqrt(var + LN_EPS)


def _log_sigmoid(x):
    return jnp.minimum(x, 0.0) - jnp.log1p(jnp.exp(-jnp.abs(x)))


def _scan_rows(x, op):
    n = x.shape[0]
    rows = lax.broadcasted_iota(jnp.int32, x.shape, 0)
    shift = 1
    while shift < n:
        x = jnp.where(rows >= shift, op(x, pltpu.roll(x, shift, 0)), x)
        shift *= 2
    return x


def _hs(h):
    return slice(h * HEAD_DIM, (h + 1) * HEAD_DIM)


def _ln_kernel(x_ref, g_ref, b_ref, o_ref):
    o_ref[...] = _layer_norm(x_ref[...], g_ref[...], b_ref[...])


def _layer_norm_rows(x, g, b, tile):
    rows, d = x.shape
    return pl.pallas_call(
        _ln_kernel,
        grid=(rows // tile,),
        in_specs=[pl.BlockSpec((tile, d), lambda i: (i, 0)),
                  pl.BlockSpec((1, d), lambda i: (0, 0)),
                  pl.BlockSpec((1, d), lambda i: (0, 0))],
        out_specs=pl.BlockSpec((tile, d), lambda i: (i, 0)),
        out_shape=jax.ShapeDtypeStruct((rows, d), F32),
        compiler_params=_params(1),
        name="ln_rows",
    )(x, g, b)


def _mm_kernel(x_ref, w_ref, b_ref, o_ref):
    o_ref[...] = _dot(x_ref[...].astype(BF16), w_ref[...]) + b_ref[...]


def _matmul_bias(x, w_all, b_all, layer, tm, tn, name):
    rows, kdim = x.shape
    n = b_all.shape[-1]
    return pl.pallas_call(
        _mm_kernel,
        grid=(n // tn, rows // tm),
        in_specs=[pl.BlockSpec((tm, kdim), lambda j, i: (i, 0)),
                  pl.BlockSpec((None, kdim, tn), lambda j, i: (layer, 0, j)),
                  pl.BlockSpec((None, 1, tn), lambda j, i: (layer, 0, j))],
        out_specs=pl.BlockSpec((tm, tn), lambda j, i: (i, j)),
        out_shape=jax.ShapeDtypeStruct((rows, n), F32),
        compiler_params=_params(2),
        name=name,
    )(x, w_all, b_all)


def _mem_kv_kernel(x_ref, w_ref, k_ref, v_ref, kb_ref, vb_ref):
    d = N_HEADS * HEAD_DIM
    res = _dot(x_ref[...].astype(BF16), w_ref[...])
    for h in range(N_HEADS):
        k_ref[:, h, :] = res[:, _hs(h)]
        v_ref[:, h, :] = res[:, d + h * HEAD_DIM:d + (h + 1) * HEAD_DIM]
    kb_ref[...] = res[:, :d].astype(BF16)
    vb_ref[...] = res[:, d:].astype(BF16)


def _mem_kv(mem2d, w_kv, batch):
    rows, d = mem2d.shape
    n_mem = rows // batch
    depth = w_kv.shape[0]
    out_spec = pl.BlockSpec((None, None, n_mem, N_HEADS, HEAD_DIM), lambda l, b: (l, b, 0, 0, 0))
    out_sds = jax.ShapeDtypeStruct((depth, batch, n_mem, N_HEADS, HEAD_DIM), F32)
    dense_spec = pl.BlockSpec((None, n_mem, d), lambda l, b: (l, b, 0))
    dense_sds = jax.ShapeDtypeStruct((depth, rows, d), BF16)
    return pl.pallas_call(
        _mem_kv_kernel,
        grid=(depth, batch),
        in_specs=[pl.BlockSpec((n_mem, d), lambda l, b: (b, 0)),
                  pl.BlockSpec((None, d, 2 * d), lambda l, b: (l, 0, 0))],
        out_specs=[out_spec, out_spec, dense_spec, dense_spec],
        out_shape=[out_sds, out_sds, dense_sds, dense_sds],
        compiler_params=_params(2),
        name="mem_kv",
    )(mem2d, w_kv)


def _sample_attn_scores(q_row, keys):
    sub = lax.broadcasted_iota(jnp.int32, (SUBLANES, HEAD_DIM), 0)
    q = jnp.zeros((SUBLANES, HEAD_DIM), F32)
    for h in range(N_HEADS):
        q = jnp.where(sub == h, jnp.broadcast_to(q_row(h), (SUBLANES, HEAD_DIM)), q)
    keys2d = keys.reshape(keys.shape[0] * N_HEADS, HEAD_DIM).astype(BF16)
    return _dot_nt(q.astype(BF16), keys2d)


def _sample_attn_softmax(s):
    s = s * (HEAD_DIM ** -0.5)
    head = lax.broadcasted_iota(jnp.int32, s.shape, 0) % SUBLANES
    col_head = lax.broadcasted_iota(jnp.int32, s.shape, 1) % N_HEADS
    s = jnp.where((col_head == head) | (head >= N_HEADS), s, -jnp.inf)
    p = jnp.exp(s - jnp.max(s, axis=-1, keepdims=True))
    return p / jnp.sum(p, axis=-1, keepdims=True)


def _sample_attn_output(p, vals):
    vals2d = vals.reshape(vals.shape[0] * N_HEADS, HEAD_DIM).astype(BF16)
    return _dot(p.astype(BF16), vals2d)


def _mlstm_prompt_kernel(x_ref, wqkvo_ref, bqkvo_ref, wif_ref, bif_ref, wg0_ref, bg0_ref,
                         mhg_ref, wmo_ref, xq_ref, sk_ref, sv_ref,
                         y_ref, c_ref, n_ref, m_ref, ox_ref, s_scr):
    t = pl.program_id(1)
    d = N_HEADS * HEAD_DIM
    ride = sk_ref.shape[0]
    seq0 = (pl.program_id(0) * pl.num_programs(1) + t) * ride

    @pl.when(t == 0)
    def _():
        c_ref[...] = jnp.zeros(c_ref.shape, F32)
        n_ref[...] = jnp.zeros(n_ref.shape, F32)
        m_ref[...] = jnp.zeros(m_ref.shape, F32)

    lc = TIME_TILE
    causal = (lax.broadcasted_iota(jnp.int32, (lc, lc), 1)
              <= lax.broadcasted_iota(jnp.int32, (lc, lc), 0))

    for chunk, r0 in enumerate(range(0, x_ref.shape[0], lc)):
        rows = slice(r0, r0 + lc)
        xb = x_ref[rows, :].astype(BF16)
        gates = _dot(xb, wif_ref[...]) + bif_ref[...]
        gi, gf = gates[:, 0:LANES], gates[:, LANES:2 * LANES]
        lf = _log_sigmoid(gf)
        bcum = _scan_rows(lf, jnp.add)
        a = gi - bcum
        m0 = m_ref[0]
        mt = bcum + jnp.maximum(m0, _scan_rows(a, jnp.maximum))
        inter = jnp.exp(m0 + bcum - mt)
        bm = bcum - mt
        emt = jnp.exp(-mt)
        b_last = bcum[lc - 1:lc, :]
        m_last = mt[lc - 1:lc, :]
        ws = jnp.exp(a + b_last - m_last)
        decay = jnp.exp(m0 + b_last - m_last)
        a_t = a.T
        if chunk == 0:
            for i in range(ride):
                s_scr[i * SUBLANES:(i + 1) * SUBLANES, :] = _sample_attn_scores(
                    lambda h, i=i: xq_ref[pl.ds(seq0 + i, 1), _hs(h)], sk_ref[i])

        def head_proj(h, xb=xb):
            def proj(part):
                cols = slice(part * d + h * HEAD_DIM, part * d + (h + 1) * HEAD_DIM)
                return _dot(xb, wqkvo_ref[:, cols]) + bqkvo_ref[:, cols]
            return proj(0), proj(1) * (HEAD_DIM ** -0.5), proj(2), proj(3)

        hn_heads = []
        projs = [head_proj(h) for h in range(HEAD_LOOKAHEAD)]
        for h in range(N_HEADS):
            if h + HEAD_LOOKAHEAD < N_HEADS:
                projs.append(head_proj(h + HEAD_LOOKAHEAD))
            elif h + HEAD_LOOKAHEAD == N_HEADS:
                g0 = _dot(xb, wg0_ref[...]) + bg0_ref[...]
            q, k, v, o = projs[h]
            qb = q.astype(BF16)
            vb = v.astype(BF16)
            col = slice(h, h + 1)
            logd = a_t[h:h + 1, :] + bm[:, col]
            dmat = jnp.exp(jnp.where(causal, logd, NEG))
            s = _dot_nt(qb, k.astype(BF16)) * dmat
            c_old = c_ref[0, h]
            n_old = n_ref[0, h:h + 1, :]
            inter_h = inter[:, col]
            num = _dot(s.astype(BF16), vb) + inter_h * _dot(qb, c_old.astype(BF16))
            den = (jnp.sum(s, axis=1, keepdims=True)
                   + inter_h * jnp.sum(q * n_old, axis=1, keepdims=True))
            ht = num / jnp.maximum(jnp.abs(den), emt[:, col])
            kw = k * ws[:, col]
            c_ref[0, h] = decay[:, col] * c_old + _dot_tn(kw.astype(BF16), vb)
            n_ref[0, h:h + 1, :] = decay[:, col] * n_old + jnp.sum(kw, axis=0, keepdims=True)
            hn = _head_norm(jax.nn.sigmoid(o) * ht) * mhg_ref[:, _hs(h)]
            hn_heads.append(hn.astype(BF16))
            if chunk == 0 and h == 0:
                s_scr[...] = _sample_attn_softmax(s_scr[...])
        m_ref[0] = m_last

        if chunk == 0:
            for i in range(ride):
                out = _sample_attn_output(s_scr[i * SUBLANES:(i + 1) * SUBLANES, :], sv_ref[i])
                for h in range(N_HEADS):
                    ox_ref[pl.ds(seq0 + i, 1), _hs(h)] = out[h:h + 1, :]

        ym = _dot(jnp.concatenate(hn_heads, axis=1), wmo_ref[...])
        y_ref[rows, :] = jax.nn.sigmoid(g0) * ym


def _mlstm_prompt(x2d, batch, w_qkvo, b_qkvo, w_if, b_if, w_rest, b_rest, mh_g, w_m_out,
                  proj_rest_s, mem_k, mem_v, layer):
    rows, d = x2d.shape
    nt = rows // batch // MLSTM_TILE
    wide = N_HEADS * d
    nseq, n_mem = mem_k.shape[1], mem_k.shape[2]
    ride = nseq // (batch * nt)
    assert ride * batch * nt == nseq

    def lw(shape, col):
        return pl.BlockSpec((None,) + shape, lambda b, t: (layer, 0, col),
                            pipeline_mode=pl.Buffered(1))

    kv_spec = pl.BlockSpec((None, ride, n_mem, N_HEADS, HEAD_DIM),
                           lambda b, t: (layer, b * nt + t, 0, 0, 0))
    return pl.pallas_call(
        _mlstm_prompt_kernel,
        grid=(batch, nt),
        in_specs=[pl.BlockSpec((MLSTM_TILE, d), lambda b, t: (b * nt + t, 0)),
                  lw((d, wide), 0), lw((1, wide), 0),
                  lw((d, 2 * LANES), 0), lw((1, 2 * LANES), 0),
                  lw((d, d), REST_G0), lw((1, d), REST_G0),
                  lw((1, d), 0), lw((d, d), 0),
                  pl.BlockSpec((nseq, d), lambda b, t: (0, REST_XQ)), kv_spec, kv_spec],
        out_specs=[pl.BlockSpec((MLSTM_TILE, d), lambda b, t: (b * nt + t, 0)),
                   pl.BlockSpec((1, N_HEADS, HEAD_DIM, HEAD_DIM), lambda b, t: (b, 0, 0, 0)),
                   pl.BlockSpec((1, N_HEADS, HEAD_DIM), lambda b, t: (b, 0, 0)),
                   pl.BlockSpec((1, 1, LANES), lambda b, t: (b, 0, 0)),
                   pl.BlockSpec((nseq, d), lambda b, t: (0, 0))],
        out_shape=[jax.ShapeDtypeStruct((rows, d), F32),
                   jax.ShapeDtypeStruct((batch, N_HEADS, HEAD_DIM, HEAD_DIM), F32),
                   jax.ShapeDtypeStruct((batch, N_HEADS, HEAD_DIM), F32),
                   jax.ShapeDtypeStruct((batch, 1, LANES), F32),
                   jax.ShapeDtypeStruct((nseq, d), F32)],
        scratch_shapes=[pltpu.VMEM((ride * SUBLANES, N_HEADS * n_mem), F32)],
        compiler_params=_params(2),
        name="mlstm_prompt",
    )(x2d, w_qkvo, b_qkvo, w_if, b_if, w_rest, b_rest, mh_g, w_m_out, proj_rest_s, mem_k, mem_v)


def _pool_window_sums(ext, window):
    s, span = ext, 1
    while span < window:
        s = s + pltpu.roll(s, span, 0)
        span *= 2
    return s


def _mix_prompt_kernel(x_ref, ymg_ref, mk_ref, mv_ref, wu_ref, bu_ref, wxq_ref, bxq_ref,
                       wg1_ref, bg1_ref, wg2_ref, bg2_ref, wpool_ref, pscale_ref, wxo_ref,
                       wo_ref, lng_ref, lnb_ref, sq_ref, sv_ref, skw_ref, sdec_ref, sc_ref,
                       *refs):
    x1_ref, pool_ref, sr_ref, scn_ref, ext_scr = refs[-5:]
    t = pl.program_id(1)
    tt = x_ref.shape[0]
    carry = 2 * SUBLANES
    ride = sc_ref.shape[0]
    seq0 = (pl.program_id(0) * pl.num_programs(1) + t) * ride

    @pl.when(t == 0)
    def _():
        ext_scr[0:carry, :] = jnp.zeros((carry, ext_scr.shape[1]), F32)

    for sub, r0 in enumerate(range(0, tt, TIME_TILE)):
        rows = slice(r0, r0 + TIME_TILE)
        for i in range(sub * ride // (tt // TIME_TILE), (sub + 1) * ride // (tt // TIME_TILE)):
            _sample_memory_update(seq0 + i, i, sq_ref, sv_ref, skw_ref, sdec_ref, sc_ref,
                                  scn_ref, sr_ref)

        x = x_ref[rows, :]
        xb = x.astype(BF16)

        u = _dot(xb, wu_ref[...]) + bu_ref[...]
        xq = (_dot(xb, wxq_ref[...]) + bxq_ref[...]).astype(BF16)

        ext_scr[carry + r0:carry + r0 + TIME_TILE, :] = u
        pos = t * tt + r0 + lax.broadcasted_iota(jnp.int32, (TIME_TILE, 1), 0)
        scores = [_dot_nt(xq[:, _hs(h)], mk_ref[:, _hs(h)]) for h in range(N_HEADS)]
        yp_groups = []
        for g, window in enumerate(POOL_WINDOWS):
            cols = slice(g * POOL_GROUP_DIM, (g + 1) * POOL_GROUP_DIM)
            ext = ext_scr[r0:r0 + carry + TIME_TILE, cols]
            wsum = _pool_window_sums(ext, window)[carry:, :]
            cnt = jnp.minimum(window, pos + 1).astype(F32)
            dlt = wsum / cnt - u[:, cols]
            yp_groups.append(_dot(dlt.astype(BF16), wpool_ref[g]))
        if r0 + TIME_TILE == tt:
            ext_scr[0:carry, :] = u[TIME_TILE - carry:, :]
            pool_ref[0] = u[TIME_TILE - carry:, :]
        g1 = _dot(xb, wg1_ref[...]) + bg1_ref[...]

        heads = []
        for h in range(N_HEADS):
            s = scores[h] * (HEAD_DIM ** -0.5)
            p = jnp.exp(s - jnp.max(s, axis=-1, keepdims=True))
            p = p / jnp.sum(p, axis=-1, keepdims=True)
            heads.append(_dot(p.astype(BF16), mv_ref[:, _hs(h)]).astype(BF16))
        g2 = _dot(xb, wg2_ref[...]) + bg2_ref[...]
        yp = jnp.concatenate(yp_groups, axis=1) * pscale_ref[...]
        yx = _dot(jnp.concatenate(heads, axis=1), wxo_ref[...])
        mix = ymg_ref[rows, :] + jax.nn.sigmoid(g1) * yp + jax.nn.sigmoid(g2) * yx
        y = ALPHA * x + _dot(mix.astype(BF16), wo_ref[...])
        x1_ref[rows, :] = _layer_norm(y, lng_ref[...], lnb_ref[...])


def _mix_prompt(x2d, ymg, mk, mv, batch, w_rest, b_rest, w_pool, pool_scale, w_x_out, w_o,
                ln_g, ln_b, proj_qkvo_s, kw_s, dec_s, state_c, prev_c, layer):
    rows, d = x2d.shape
    tile = MIX_TILE_FIRST if prev_c is None else MIX_TILE
    nt = rows // batch // tile
    n_mem = mk.shape[1] // batch
    carry = 2 * SUBLANES
    depth, nseq = state_c.shape[:2]
    ride = nseq // (batch * nt)
    assert ride * batch * nt == nseq and ride % (tile // TIME_TILE) == 0

    def lw(shape, col):
        return pl.BlockSpec((None,) + shape, lambda b, t: (layer, 0, col),
                            pipeline_mode=pl.Buffered(1))

    def sample_rows(col):
        return pl.BlockSpec((nseq, d), lambda b, t: (0, col))

    row_spec = pl.BlockSpec((tile, d), lambda b, t: (b * nt + t, 0))
    mem_spec = pl.BlockSpec((None, n_mem, d), lambda b, t: (layer, b, 0))
    c_block = (ride, N_HEADS, HEAD_DIM, HEAD_DIM)
    c_spec = pl.BlockSpec((None,) + c_block, lambda b, t: (layer, b * nt + t, 0, 0, 0))
    if prev_c is None:
        c_out_spec = pl.BlockSpec((depth,) + c_block, lambda b, t: (0, b * nt + t, 0, 0, 0))
        chain_specs, chain_args, aliases = [], [], {}
    else:
        c_out_spec = c_spec
        chain_specs, chain_args, aliases = [pl.BlockSpec(memory_space=pl.ANY)], [prev_c], {23: 3}
    return pl.pallas_call(
        _mix_prompt_kernel,
        grid=(batch, nt),
        in_specs=[row_spec, row_spec, mem_spec, mem_spec,
                  lw((d, d), REST_U), lw((1, d), REST_U),
                  lw((d, d), REST_XQ), lw((1, d), REST_XQ),
                  lw((d, d), REST_G1), lw((1, d), REST_G1),
                  lw((d, d), REST_G2), lw((1, d), REST_G2),
                  pl.BlockSpec((None, len(POOL_WINDOWS), POOL_GROUP_DIM, POOL_GROUP_DIM),
                               lambda b, t: (layer, 0, 0, 0)),
                  lw((1, d), 0), lw((d, d), 0), lw((d, d), 0), lw((1, d), 0), lw((1, d), 0),
                  sample_rows(0), sample_rows(2), sample_rows(0), sample_rows(0), c_spec]
        + chain_specs,
        out_specs=[row_spec, pl.BlockSpec((1, carry, d), lambda b, t: (b, 0, 0)),
                   sample_rows(0), c_out_spec],
        out_shape=[jax.ShapeDtypeStruct((rows, d), F32),
                   jax.ShapeDtypeStruct((batch, carry, d), F32),
                   jax.ShapeDtypeStruct((nseq, d), F32),
                   jax.ShapeDtypeStruct(state_c.shape, F32)],
        input_output_aliases=aliases,
        scratch_shapes=[pltpu.VMEM((carry + tile, d), F32)],
        compiler_params=_params(2),
        name="mix_prompt",
    )(x2d, ymg, mk, mv, w_rest, b_rest, w_rest, b_rest, w_rest, b_rest, w_rest, b_rest,
      w_pool, pool_scale, w_x_out, w_o, ln_g, ln_b,
      proj_qkvo_s, proj_qkvo_s, kw_s, dec_s, state_c, *chain_args)


def _ffn_chunk(xb, j, d_ff, wup_ref, cw_ref, cb_ref, wdn_ref, conv_taps):
    halves = _ffn_up(xb, j, d_ff, wup_ref)
    return _ffn_gate_down(halves, j, d_ff, cw_ref, cb_ref, wdn_ref, conv_taps), halves


def _ffn_up(xb, j, d_ff, wup_ref):
    return [_dot(xb, wup_ref[:, base + j * FF_CHUNK:base + (j + 1) * FF_CHUNK])
            for base in (0, d_ff)]


def _ffn_gate_down(halves, j, d_ff, cw_ref, cb_ref, wdn_ref, conv_taps):
    conv = []
    for base, hup in zip((0, d_ff), halves):
        cols = slice(base + j * FF_CHUNK, base + (j + 1) * FF_CHUNK)
        prev2, prev1 = conv_taps(hup, cols)
        conv.append(prev2 * cw_ref[0:1, cols] + prev1 * cw_ref[1:2, cols]
                    + hup * cw_ref[2:3, cols] + cb_ref[:, cols])
    act = jax.nn.gelu(conv[0]) * conv[1]
    return _dot(act.astype(BF16), wdn_ref[j * FF_CHUNK:(j + 1) * FF_CHUNK, :])


def _ffn_prompt_kernel(x_ref, wup_ref, cw_ref, cb_ref, wdn_ref, lng_ref, lnb_ref,
                       o_ref, conv_ref, ext_scr):
    t = pl.program_id(1)
    tt = x_ref.shape[0]
    d_ff = wdn_ref.shape[0]
    n_chunks = d_ff // FF_CHUNK

    @pl.when(t == 0)
    def _():
        ext_scr[0:SUBLANES, :] = jnp.zeros((SUBLANES, ext_scr.shape[1]), F32)

    items = [(r0, j) for r0 in range(0, tt, FF_SUB_ROWS) for j in range(n_chunks)]
    xs = {r0: x_ref[r0:r0 + FF_SUB_ROWS, :] for r0 in range(0, tt, FF_SUB_ROWS)}
    xbs = {r0: xs[r0].astype(BF16) for r0 in xs}

    def up(item):
        r0, j = item
        return _ffn_up(xbs[r0], j, d_ff, wup_ref)

    ups = [up(item) for item in items[:FF_LOOKAHEAD]]
    acc = None
    for i, (r0, j) in enumerate(items):
        if i + FF_LOOKAHEAD < len(items):
            ups.append(up(items[i + FF_LOOKAHEAD]))
        halves = ups[i]
        ups[i] = None
        lo = SUBLANES + r0

        def conv_taps(hup, cols):
            ext_scr[lo:lo + FF_SUB_ROWS, cols] = hup
            return (ext_scr[lo - 2:lo - 2 + FF_SUB_ROWS, cols],
                    ext_scr[lo - 1:lo - 1 + FF_SUB_ROWS, cols])

        part = _ffn_gate_down(halves, j, d_ff, cw_ref, cb_ref, wdn_ref, conv_taps)
        acc = part if j == 0 else part + acc
        if r0 + FF_SUB_ROWS == tt:
            for base, hup in zip((0, d_ff), halves):
                cols = slice(base + j * FF_CHUNK, base + (j + 1) * FF_CHUNK)
                ext_scr[0:SUBLANES, cols] = hup[FF_SUB_ROWS - SUBLANES:, :]
                conv_ref[0, :, cols] = hup[FF_SUB_ROWS - SUBLANES:, :]
        if j == n_chunks - 1:
            o_ref[r0:r0 + FF_SUB_ROWS, :] = _layer_norm(ALPHA * xs[r0] + acc, lng_ref[...],
                                                       lnb_ref[...])


def _ffn_prompt(x2d, batch, w_up, conv_w, conv_b, w_down, ln_g, ln_b, layer):
    rows, d = x2d.shape
    nt = rows // batch // FF_TILE
    d_ff = w_down.shape[1]

    def lw(shape):
        return pl.BlockSpec((None,) + shape, lambda b, t: (layer, 0, 0),
                            pipeline_mode=pl.Buffered(1))

    row_spec = pl.BlockSpec((FF_TILE, d), lambda b, t: (b * nt + t, 0))
    return pl.pallas_call(
        _ffn_prompt_kernel,
        grid=(batch, nt),
        in_specs=[row_spec, lw((d, 2 * d_ff)), lw((CONV_WIDTH, 2 * d_ff)), lw((1, 2 * d_ff)),
                  lw((d_ff, d)), lw((1, d)), lw((1, d))],
        out_specs=[row_spec, pl.BlockSpec((1, SUBLANES, 2 * d_ff), lambda b, t: (b, 0, 0))],
        out_shape=[jax.ShapeDtypeStruct((rows, d), F32),
                   jax.ShapeDtypeStruct((batch, SUBLANES, 2 * d_ff), F32)],
        scratch_shapes=[pltpu.VMEM((SUBLANES + FF_TILE, 2 * d_ff), F32)],
        compiler_params=_params(2),
        name="ffn_prompt",
    )(x2d, w_up, conv_w, conv_b, w_down, ln_g, ln_b)


def _ffn_sample_kernel(x_ref, st_ref, wup_ref, cw_ref, cb_ref, wdn_ref, lng_ref, lnb_ref,
                       o_ref, stn_ref):
    d_ff = wdn_ref.shape[0]
    x = x_ref[...]
    xb = x.astype(BF16)

    def conv_taps(hup, cols):
        return st_ref[:, 0, cols], st_ref[:, 1, cols]

    acc = None
    for j in range(d_ff // FF_CHUNK):
        part, halves = _ffn_chunk(xb, j, d_ff, wup_ref, cw_ref, cb_ref, wdn_ref, conv_taps)
        for base, hup in zip((0, d_ff), halves):
            cols = slice(base + j * FF_CHUNK, base + (j + 1) * FF_CHUNK)
            stn_ref[:, 0, cols] = st_ref[:, 1, cols]
            stn_ref[:, 1, cols] = hup
        acc = part if acc is None else acc + part
    o_ref[...] = _layer_norm(ALPHA * x + acc, lng_ref[...], lnb_ref[...])


def _ffn_sample(x, conv_state, w_up, conv_w, conv_b, w_down, ln_g, ln_b, layer, tile):
    rows, d = x.shape
    d_ff = w_down.shape[1]
    st_shape = conv_state.shape[2:]

    def lw(shape):
        return pl.BlockSpec((None,) + shape, lambda i: (layer, 0, 0))

    return pl.pallas_call(
        _ffn_sample_kernel,
        grid=(rows // tile,),
        in_specs=[pl.BlockSpec((tile, d), lambda i: (i, 0)),
                  pl.BlockSpec((None, tile) + st_shape, lambda i: (layer, i, 0, 0)),
                  lw((d, 2 * d_ff)), lw((CONV_WIDTH, 2 * d_ff)), lw((1, 2 * d_ff)),
                  lw((d_ff, d)), lw((1, d)), lw((1, d))],
        out_specs=[pl.BlockSpec((tile, d), lambda i: (i, 0)),
                   pl.BlockSpec((tile,) + st_shape, lambda i: (i, 0, 0))],
        out_shape=[jax.ShapeDtypeStruct((rows, d), F32),
                   jax.ShapeDtypeStruct((rows,) + st_shape, F32)],
        compiler_params=_params(1),
        name="ffn_sample",
    )(x, conv_state, w_up, conv_w, conv_b, w_down, ln_g, ln_b)


def _mlstm_pre_kernel(q_ref, k_ref, x_ref, wif_ref, bif_ref, n_ref, m_ref,
                      kw_ref, dec_ref, s_ref, den_ref, nn_ref, mn_ref):
    gates = _dot(x_ref[...].astype(BF16), wif_ref[...]) + bif_ref[...]
    gi = gates[:, 0:N_HEADS]
    lf = _log_sigmoid(gates[:, LANES:LANES + N_HEADS])
    m0 = m_ref[...]
    mt = jnp.maximum(m0 + lf, gi)
    inter_all = jnp.exp(m0 + lf - mt)
    dm_all = jnp.exp(gi - mt)
    emt_all = jnp.exp(-mt)
    mn_ref[...] = mt
    for h in range(N_HEADS):
        col = slice(h, h + 1)
        q = q_ref[:, _hs(h)]
        k = k_ref[:, _hs(h)] * (HEAD_DIM ** -0.5)
        n_old = n_ref[:, h, :]
        inter, dm = inter_all[:, col], dm_all[:, col]
        s = jnp.sum(q * k, axis=1, keepdims=True) * dm
        den = s + inter * jnp.sum(q * n_old, axis=1, keepdims=True)
        kw = k * dm
        kw_ref[:, _hs(h)] = kw
        dec_ref[:, _hs(h)] = jnp.broadcast_to(inter, kw.shape)
        s_ref[:, _hs(h)] = jnp.broadcast_to(s, kw.shape)
        den_ref[:, _hs(h)] = jnp.broadcast_to(jnp.maximum(jnp.abs(den), emt_all[:, col]), kw.shape)
        nn_ref[:, h, :] = inter * n_old + kw


def _mlstm_pre(proj_qkvo, x, w_if, b_if, state_n, state_m, layer, tile):
    nseq = proj_qkvo.shape[0]
    d = N_HEADS * HEAD_DIM

    def rows_at(col):
        return pl.BlockSpec((tile, d), lambda i: (i, col))

    wide_sds = jax.ShapeDtypeStruct((nseq, d), F32)
    return pl.pallas_call(
        _mlstm_pre_kernel,
        grid=(nseq // tile,),
        in_specs=[rows_at(0), rows_at(1), rows_at(0),
                  pl.BlockSpec((None, d, 2 * LANES), lambda i: (layer, 0, 0)),
                  pl.BlockSpec((None, 1, 2 * LANES), lambda i: (layer, 0, 0)),
                  pl.BlockSpec((None, tile, N_HEADS, HEAD_DIM), lambda i: (layer, i, 0, 0)),
                  pl.BlockSpec((None, tile, N_HEADS), lambda i: (layer, i, 0))],
        out_specs=[rows_at(0)] * 4 + [pl.BlockSpec((tile, N_HEADS, HEAD_DIM), lambda i: (i, 0, 0)),
                                      pl.BlockSpec((tile, N_HEADS), lambda i: (i, 0))],
        out_shape=[wide_sds] * 4 + [jax.ShapeDtypeStruct((nseq, N_HEADS, HEAD_DIM), F32),
                                    jax.ShapeDtypeStruct((nseq, N_HEADS), F32)],
        compiler_params=_params(1),
        name="mlstm_pre",
    )(proj_qkvo, proj_qkvo, x, w_if, b_if, state_n, state_m)


def _sample_memory_update(seq, i, q_ref, v_ref, kw_ref, dec_ref, c_ref, cn_ref, r_ref):
    mxu_rows = 2 * SUBLANES
    row0 = lax.broadcasted_iota(jnp.int32, (mxu_rows, HEAD_DIM), 0) == 0
    row = pl.ds(seq, 1)
    for h in range(N_HEADS):
        qrow = jnp.broadcast_to(q_ref[row, _hs(h)], (mxu_rows, HEAD_DIM)).astype(BF16)
        c_old = c_ref[i, h]
        r_ref[row, _hs(h)] = _dot(qrow, c_old.astype(BF16))[0:1, :]
        kw = jnp.where(row0, jnp.broadcast_to(kw_ref[row, _hs(h)], (mxu_rows, HEAD_DIM)), 0.0)
        vrow = jnp.broadcast_to(v_ref[row, _hs(h)], (mxu_rows, HEAD_DIM))
        c_new = dec_ref[row, _hs(h)] * c_old + _dot_tn(kw.astype(BF16), vrow.astype(BF16))
        if len(cn_ref.shape) == 5:
            for layer_slot in range(cn_ref.shape[0]):
                cn_ref[layer_slot, i, h] = c_new
        else:
            cn_ref[i, h] = c_new


def _merge_step_kernel(x_ref, r_ref, s_ref, den_ref, dec_ref, v_ref, o_ref, mhg_ref, ox_ref,
                       u_ref, g0_ref, g1_ref, g2_ref, pool_ref,
                       wmo_ref, wpool_ref, pscale_ref, wxo_ref, wo_ref, lng_ref, lnb_ref,
                       x1_ref, pooln_ref):
    x = x_ref[...]
    u = u_ref[...]
    ht = (s_ref[...] * v_ref[...] + dec_ref[...] * r_ref[...]) / den_ref[...]
    hc = jax.nn.sigmoid(o_ref[...]) * ht
    hn = jnp.concatenate([_head_norm(hc[:, _hs(h)]) for h in range(N_HEADS)], axis=1)
    hn = hn * mhg_ref[...]
    for r in range(1, POOL_BUF):
        pooln_ref[:, r - 1, :] = pool_ref[:, r, :]
    pooln_ref[:, POOL_BUF - 1, :] = u
    yp_groups = []
    for g, window in enumerate(POOL_WINDOWS):
        cols = slice(g * POOL_GROUP_DIM, (g + 1) * POOL_GROUP_DIM)
        wsum = u[:, cols]
        for back in range(1, window):
            wsum = wsum + pool_ref[:, POOL_BUF - back, cols]
        cnt = float(min(window, PAST_LEN + 1))
        dlt = wsum / cnt - u[:, cols]
        yp_groups.append(_dot(dlt.astype(BF16), wpool_ref[g]))
    yp = jnp.concatenate(yp_groups, axis=1) * pscale_ref[...]
    ym = _dot(hn.astype(BF16), wmo_ref[...])
    yx = _dot(ox_ref[...].astype(BF16), wxo_ref[...])
    mix = (jax.nn.sigmoid(g0_ref[...]) * ym + jax.nn.sigmoid(g1_ref[...]) * yp
           + jax.nn.sigmoid(g2_ref[...]) * yx)
    y = ALPHA * x + _dot(mix.astype(BF16), wo_ref[...])
    x1_ref[...] = _layer_norm(y, lng_ref[...], lnb_ref[...])


def _merge_step(x, readout, s_rep, den_rep, dec_rep, proj_qkvo, mh_g, ox, proj_rest, pool_state,
                w_m_out, w_pool, pool_scale, w_x_out, w_o, ln_g, ln_b, layer, tile):
    rows, d = x.shape
    st_shape = pool_state.shape[2:]

    def lw(shape):
        return pl.BlockSpec((None,) + shape, lambda i: (layer,) + (0,) * len(shape))

    def rows_at(col):
        return pl.BlockSpec((tile, d), lambda i: (i, col))

    return pl.pallas_call(
        _merge_step_kernel,
        grid=(rows // tile,),
        in_specs=[rows_at(0), rows_at(0), rows_at(0), rows_at(0), rows_at(0),
                  rows_at(2), rows_at(3), lw((1, d)), rows_at(0),
                  rows_at(REST_U), rows_at(REST_G0), rows_at(REST_G1), rows_at(REST_G2),
                  pl.BlockSpec((None, tile) + st_shape, lambda i: (layer, i, 0, 0)),
                  lw((d, d)), lw((len(POOL_WINDOWS), POOL_GROUP_DIM, POOL_GROUP_DIM)),
                  lw((1, d)), lw((d, d)), lw((d, d)), lw((1, d)), lw((1, d))],
        out_specs=[rows_at(0), pl.BlockSpec((tile,) + st_shape, lambda i: (i, 0, 0))],
        out_shape=[jax.ShapeDtypeStruct((rows, d), F32),
                   jax.ShapeDtypeStruct((rows,) + st_shape, F32)],
        compiler_params=_params(1),
        name="merge_step",
    )(x, readout, s_rep, den_rep, dec_rep, proj_qkvo, proj_qkvo, mh_g, ox,
      proj_rest, proj_rest, proj_rest, proj_rest, pool_state, w_m_out, w_pool,
      pool_scale, w_x_out, w_o, ln_g, ln_b)


def kernel(x_prompt, mem_prompt, x_sample, cache_mem_k, cache_mem_v, state_C, state_n, state_m,
           state_pool, state_conv, ln_in_g, ln_in_b, w_in, b_in, mh_g, w_m_out, w_pool,
           pool_scale, w_mem_kv, w_x_out, w_o, ln1_g, ln1_b, w_up, conv_w, conv_b, w_down,
           ln2_g, ln2_b):
    batch, seq, d = x_prompt.shape
    nseq = x_sample.shape[0]
    n_mem = mem_prompt.shape[1]
    depth = w_in.shape[0]
    d_ff = w_down.shape[1]
    wide = N_HEADS * d
    gate_off = wide
    rest_off = wide + 2 * N_HEADS
    assert d == N_HEADS * HEAD_DIM and d_ff % FF_CHUNK == 0
    assert all(seq % tile == 0 for tile in (MLSTM_TILE, MIX_TILE, MIX_TILE_FIRST, FF_TILE))
    assert depth == DEPTH and x_sample.shape[1] == 1

    def row(v):
        return v.reshape(depth, 1, v.shape[-1])

    w_qkvo = w_in.astype(BF16)
    w_rest = w_qkvo[:, :, rest_off:]
    b_qkvo, b_rest = row(b_in[:, :wide]), row(b_in[:, rest_off:])
    lane_pad = ((0, 0), (0, 0), (0, LANES - N_HEADS))
    w_if = jnp.concatenate(
        [jnp.pad(w_in[:, :, gate_off:gate_off + N_HEADS], lane_pad),
         jnp.pad(w_in[:, :, gate_off + N_HEADS:rest_off], lane_pad)], axis=2).astype(BF16)
    b_if = row(jnp.concatenate(
        [jnp.pad(b_in[:, gate_off:gate_off + N_HEADS], lane_pad[1:]),
         jnp.pad(b_in[:, gate_off + N_HEADS:rest_off], lane_pad[1:])], axis=1))
    w_m_out_b, w_pool_b, w_x_out_b, w_o_b = (w.astype(BF16) for w in (w_m_out, w_pool, w_x_out, w_o))
    w_up_b, w_down_b = w_up.astype(BF16), w_down.astype(BF16)
    w_kv_b = w_mem_kv.astype(BF16)
    mh_g_r, pool_scale_r, conv_b_r = row(mh_g), row(pool_scale), row(conv_b)
    ln1_g_r, ln1_b_r, ln2_g_r, ln2_b_r = row(ln1_g), row(ln1_b), row(ln2_g), row(ln2_b)

    xp = x_prompt.reshape(batch * seq, d)
    xs = x_sample.reshape(nseq, d)
    mem2d = mem_prompt.reshape(batch * n_mem, d)

    xp = _layer_norm_rows(xp, ln_in_g.reshape(1, d), ln_in_b.reshape(1, d), 2 * TIME_TILE)
    xs = _layer_norm_rows(xs, ln_in_g.reshape(1, d), ln_in_b.reshape(1, d), nseq)

    outs = {name: [] for name in ("pC", "pn", "pm", "ppool", "pconv", "sn", "sm", "spool", "sconv")}
    p_mem_k, p_mem_v, mk_b, mv_b = _mem_kv(mem2d, w_kv_b, batch)
    s_c = None
    for l in range(depth):
        proj_qkvo = _matmul_bias(xs, w_qkvo, b_qkvo, l, nseq, d, "proj_sample")
        proj_rest = _matmul_bias(xs, w_rest, b_rest, l, nseq, d, "proj_sample")
        kw_s, dec_s, s_rep, den_rep, s_n, s_m = _mlstm_pre(proj_qkvo, xs, w_if, b_if, state_n,
                                                          state_m, l, nseq // 2)

        ymg, p_c, p_n, p_m, ox = _mlstm_prompt(xp, batch, w_qkvo, b_qkvo, w_if, b_if, w_rest,
                                               b_rest, mh_g_r, w_m_out_b, proj_rest,
                                               cache_mem_k, cache_mem_v, l)
        x1, p_pool, readout, s_c = _mix_prompt(xp, ymg, mk_b, mv_b, batch, w_rest, b_rest,
                                               w_pool_b, pool_scale_r, w_x_out_b, w_o_b, ln1_g_r,
                                               ln1_b_r, proj_qkvo, kw_s, dec_s, state_C, s_c, l)
        xp, p_conv = _ffn_prompt(x1, batch, w_up_b, conv_w, conv_b_r, w_down_b, ln2_g_r,
                                 ln2_b_r, l)
        outs["pC"].append(p_c)
        outs["pn"].append(p_n)
        outs["pm"].append(p_m[:, 0, :N_HEADS])
        outs["ppool"].append(p_pool[:, 2 * SUBLANES - POOL_BUF:, :])
        outs["pconv"].append(p_conv[:, SUBLANES - (CONV_WIDTH - 1):, :])

        x1s, s_pool = _merge_step(xs, readout, s_rep, den_rep, dec_s, proj_qkvo, mh_g_r, ox,
                                  proj_rest, state_pool, w_m_out_b, w_pool_b, pool_scale_r,
                                  w_x_out_b, w_o_b, ln1_g_r, ln1_b_r, l, nseq // 2)
        xs, s_conv = _ffn_sample(x1s, state_conv, w_up_b, conv_w, conv_b_r, w_down_b, ln2_g_r,
                                 ln2_b_r, l, nseq // 2)
        outs["sn"].append(s_n)
        outs["sm"].append(s_m)
        outs["spool"].append(s_pool)
        outs["sconv"].append(s_conv)

    stacked = {k: jnp.stack(v) for k, v in outs.items()}
    return (xp.reshape(batch, seq, d), xs.reshape(nseq, 1, d),
            stacked["pC"], stacked["pn"], stacked["pm"], stacked["ppool"], stacked["pconv"],
            p_mem_k, p_mem_v,
            s_c, stacked["sn"], stacked["sm"], stacked["spool"], stacked["sconv"])
```
